```python
import math
import jax
import jax.numpy as jnp
from jax import lax
import numpy as np

D_MODEL = 1024
BATCH = 4
SEQ = 4096
DEPTH = 4
DEC_BATCH = 32
DEC_SEQ = 32
PAST_LEN = 2048

CHUNK = 64
NORM_EPS = 1e-6

A_HEADS = 4
A_DK = 128
A_DV = 128
A_W = A_HEADS * A_DV
A_COLS = 4 * A_W + 2 * A_HEADS
FORGET_BIAS_LO = 3.0
FORGET_BIAS_HI = 6.0

B_HEADS = 8
B_KV = 2
B_GROUP = B_HEADS // B_KV
B_HD = 64
B_W = B_HEADS * B_HD
B_COLS = B_W + 2 * B_KV * B_HD
WINDOW = 128
WIN_CHUNKS = WINDOW // CHUNK
NUM_BUCKETS = 32
REL_MAX_DIST = 256

C_HEADS = 8
C_HD = 64
C_W = C_HEADS * C_HD
C_DECAY_LORA = 64
C_AAA_LORA = 64
C_GATE_LORA = 128
C_COLS = 3 * C_W + C_DECAY_LORA + C_AAA_LORA + C_GATE_LORA
GN_EPS = 64e-5

N_BRANCH = 3
IN_COLS = A_COLS + B_COLS + C_COLS + N_BRANCH * D_MODEL
D_FF = -(-8 * D_MODEL // (3 * 256)) * 256

kernel_name = 'hybrid_streaming_encoder_step'


def _split(x, sizes):
    out, start = [], 0
    for s in sizes:
        out.append(x[..., start:start + s])
        start += s
    return out


def _rms_norm(x, g):
    xf = x.astype(jnp.float32)
    y = xf * lax.rsqrt(jnp.mean(xf * xf, axis=-1, keepdims=True) + NORM_EPS)
    return (y * g.astype(jnp.float32)).astype(x.dtype)


def _mlstm_chunk(carry, blk):
    c_mat, n_vec, m_prev = carry
    q, k, v, lf, li = blk
    L = q.shape[2]
    F = jnp.cumsum(lf, axis=-1)
    causal = jnp.tril(jnp.ones((L, L), dtype=bool))
    dlog = jnp.where(causal, F[..., :, None] - F[..., None, :] + li[..., None, :], -jnp.inf)
    m_inter = F + m_prev[..., None]
    m_t = jnp.maximum(m_inter, jnp.max(dlog, axis=-1))
    w = jnp.exp(dlog - m_t[..., None]) * jnp.einsum('bhtk,bhsk->bhts', q, k)
    inter = jnp.exp(m_inter - m_t)
    num = jnp.einsum('bhts,bhsv->bhtv', w, v) + inter[..., None] * jnp.einsum('bhtk,bhkv->bhtv', q, c_mat)
    den = jnp.sum(w, axis=-1) + inter * jnp.einsum('bhtk,bhk->bht', q, n_vec)
    h = num / jnp.maximum(jnp.abs(den), jnp.exp(-m_t))[..., None]
    m_new = m_t[..., -1]
    wk = jnp.exp(F[..., -1:] - F + li - m_new[..., None])
    decay = jnp.exp(F[..., -1] + m_prev - m_new)
    c_new = decay[..., None, None] * c_mat + jnp.einsum('bhs,bhsk,bhsv->bhkv', wk, k, v)
    n_new = decay[..., None] * n_vec + jnp.einsum('bhs,bhsk->bhk', wk, k)
    return (c_new, n_new, m_new), h


def _mlstm_mixer(cols, state, gate_bias, norm_g):
    Bsz, T, _ = cols.shape
    chunk = min(CHUNK, T)
    nc = T // chunk
    q, k, v, o, ig, fg = _split(cols, [A_W, A_W, A_W, A_W, A_HEADS, A_HEADS])
    gate_bias = gate_bias.astype(jnp.float32)

    def heads(t):
        return t.reshape(Bsz, T, A_HEADS, -1).transpose(0, 2, 1, 3)

    q, k, v = heads(q), heads(k) * (A_DK ** -0.5), heads(v)
    li = (ig + gate_bias[:A_HEADS]).transpose(0, 2, 1)
    lf = jax.nn.log_sigmoid(fg + gate_bias[A_HEADS:]).transpose(0, 2, 1)

    def blocks(t):
        t = t.reshape(t.shape[:2] + (nc, chunk) + t.shape[3:])
        return jnp.moveaxis(t, 2, 0)

    state = tuple(s.astype(jnp.float32) for s in state)
    state, h = lax.scan(_mlstm_chunk, state, (blocks(q), blocks(k), blocks(v), blocks(lf), blocks(li)))
    h = jnp.moveaxis(h, 0, 2).reshape(Bsz, A_HEADS, T, A_DV)
    h = h * lax.rsqrt(jnp.mean(h * h, axis=-1, keepdims=True) + NORM_EPS)
    h = h.transpose(0, 2, 1, 3).reshape(Bsz, T, A_W) * norm_g.astype(jnp.float32)
    return jax.nn.sigmoid(o) * h, state


def _rel_bucket(rel):
    half = NUM_BUCKETS // 2
    exact = half // 2
    n = jnp.abs(rel)
    far = exact + (jnp.log(jnp.maximum(n, 1).astype(jnp.float32) / exact)
                   / math.log(REL_MAX_DIST / exact) * (half - exact)).astype(jnp.int32)
    far = jnp.minimum(far, half - 1)
    return jnp.where(rel > 0, half, 0) + jnp.where(n < exact, n, far)


def _rel_bias(table, n_q, n_before):
    rel = (jnp.arange(n_before + n_q)[None, :] - n_before) - jnp.arange(n_q)[:, None]
    bias = table[_rel_bucket(rel)].astype(jnp.float32)
    return jnp.transpose(bias, (2, 0, 1)).reshape(B_KV, B_GROUP, n_q, n_before + n_q)


def _sink_attention(q, k, v, bias, valid, sinks):
    s = jnp.einsum('...qngd,...knd->...ngqk', q, k) * (B_HD ** -0.5) + bias
    if valid is not None:
        s = jnp.where(valid[..., None, None, None, :], s, -1e30)
    sk = sinks.astype(jnp.float32).reshape(B_KV, B_GROUP, 1, 1)
    mx = jnp.maximum(jnp.max(s, axis=-1, keepdims=True), sk)
    p = jnp.exp(s - mx)
    p = p / (jnp.sum(p, axis=-1, keepdims=True) + jnp.exp(sk - mx))
    return jnp.einsum('...ngqk,...knd->...qngd', p, v)


def _swa_prompt(q, k, v, table, sinks):
    Bsz, T, _ = q.shape
    nc = T // CHUNK
    pad = WIN_CHUNKS * CHUNK
    k4 = k.reshape(Bsz, T, B_KV, B_HD)
    v4 = v.reshape(Bsz, T, B_KV, B_HD)

    def band(t):
        t = jnp.pad(t, ((0, 0), (pad, 0), (0, 0), (0, 0))).reshape(Bsz, nc + WIN_CHUNKS, CHUNK, B_KV, B_HD)
        return jnp.concatenate([t[:, c:c + nc] for c in range(WIN_CHUNKS + 1)], axis=2)

    kpos = jnp.arange(nc)[:, None] * CHUNK - pad + jnp.arange(pad + CHUNK)[None, :]
    out = _sink_attention(q.reshape(Bsz, nc, CHUNK, B_KV, B_GROUP, B_HD), band(k4), band(v4),
                          _rel_bias(table, CHUNK, pad), kpos >= 0, sinks)
    keep = min(WINDOW, T)
    return out.reshape(Bsz, T, B_W), k4[:, T - keep:], v4[:, T - keep:]


def _swa_sample(q, k, v, k_cache, v_cache, table, sinks):
    Bsz, T, _ = q.shape
    n_before = k_cache.shape[1]
    kf = jnp.concatenate([k_cache.astype(jnp.float32), k.reshape(Bsz, T, B_KV, B_HD)], axis=1)
    vf = jnp.concatenate([v_cache.astype(jnp.float32), v.reshape(Bsz, T, B_KV, B_HD)], axis=1)
    out = _sink_attention(q.reshape(Bsz, T, B_KV, B_GROUP, B_HD), kf, vf,
                          _rel_bias(table, T, n_before), None, sinks)
    return out.reshape(Bsz, T, B_W), kf[:, T:], vf[:, T:]


def _rwkv_step(S, inp):
    r, w, k, v, kk, a = inp
    S = (S * w[:, :, None, :]
         + jnp.einsum('bhij,bhj->bhi', S, -kk)[..., None] * (kk * a)[:, :, None, :]
         + v[..., None] * k[:, :, None, :])
    return S, jnp.einsum('bhij,bhj->bhi', S, r)


def _rwkv_mixer(cols, shift_prev, S0, mu, w0, w_w2, a0, w_a2, w_g2, k_k, k_a, r_k, gn_g, gn_b):
    Bsz, T, _ = cols.shape
    f32 = jnp.float32
    prev = jnp.concatenate([shift_prev.astype(f32), cols[:, :-1]], axis=1)
    xm = cols + (prev - cols) * mu.astype(f32)
    r, k, v, wl, al, gl = _split(xm, [C_W, C_W, C_W, C_DECAY_LORA, C_AAA_LORA, C_GATE_LORA])
    w = -jax.nn.softplus(-(w0 + jnp.tanh(wl) @ w_w2)) - 0.5
    decay = jnp.exp(-jnp.exp(w))
    a = jax.nn.sigmoid(a0 + al @ w_a2)
    g = jax.nn.sigmoid(gl) @ w_g2

    def heads(t):
        return t.reshape(Bsz, T, C_HEADS, C_HD)

    kk = heads(k * k_k)
    kk = kk / jnp.maximum(jnp.sqrt(jnp.sum(kk * kk, axis=-1, keepdims=True)), 1e-12)
    k = k * (1.0 + (a - 1.0) * k_a)
    r_h, k_h, v_h = heads(r), heads(k), heads(v)

    def tm(t):
        return jnp.moveaxis(t, 1, 0)

    S, y = lax.scan(_rwkv_step, S0.astype(f32),
                    (tm(r_h), tm(heads(decay)), tm(k_h), tm(v_h), tm(kk), tm(heads(a))))
    y = jnp.moveaxis(y, 0, 1)
    mean = jnp.mean(y, axis=-1, keepdims=True)
    var = jnp.mean(jnp.square(y - mean), axis=-1, keepdims=True)
    y = ((y - mean) * lax.rsqrt(var + GN_EPS)).reshape(Bsz, T, C_W) * gn_g + gn_b
    bonus = jnp.sum(r_h * k_h * r_k.reshape(C_HEADS, C_HD), axis=-1, keepdims=True) * v_h
    y = (y + bonus.reshape(Bsz, T, C_W)) * g
    return y, S, cols[:, -1:]


def _layer(x, st, P, l):
    Bsz, T, _ = x.shape
    swa_k, swa_v, mc, mn, mm, rs, rshift = st
    xn = _rms_norm(x, P['norm_mix_pre'][l])
    cols = jnp.einsum('btd,dc->btc', xn, P['w_in'][l]).astype(jnp.float32)
    ca, cb, cc, cg = _split(cols, [A_COLS, B_COLS, C_COLS, N_BRANCH * D_MODEL])

    h_a, (mc, mn, mm) = _mlstm_mixer(ca, (mc, mn, mm), P['a_gate_bias'][l], P['a_norm'][l])

    qb, kb, vb = _split(cb, [B_W, B_KV * B_HD, B_KV * B_HD])
    if swa_k is None:
        h_b, swa_k, swa_v = _swa_prompt(qb, kb, vb, P['rel_bias'], P['b_sinks'][l])
    else:
        h_b, swa_k, swa_v = _swa_sample(qb, kb, vb, swa_k, swa_v, P['rel_bias'], P['b_sinks'][l])

    h_c, rs, rshift = _rwkv_mixer(cc, rshift, rs, P['c_mu'][l], P['c_w0'][l], P['c_w_w2'][l],
                                  P['c_a0'][l], P['c_w_a2'][l], P['c_w_g2'][l], P['c_k_k'][l],
                                  P['c_k_a'][l], P['c_r_k'][l], P['c_gn_g'][l], P['c_gn_b'][l])

    gates = jax.nn.sigmoid(cg).reshape(Bsz, T, N_BRANCH, D_MODEL)
    merged = (gates[:, :, 0] * (h_a @ P['w_branch_a'][l])
              + gates[:, :, 1] * (h_b @ P['w_branch_b'][l])
              + gates[:, :, 2] * (h_c @ P['w_branch_c'][l]))
    mix = (merged @ P['w_out'][l]).astype(x.dtype)
    x = x + _rms_norm(mix, P['norm_mix_post'][l])

    xn = _rms_norm(x, P['norm_ffn_pre'][l])
    gate, up = _split(jnp.einsum('btd,df->btf', xn, P['w_ffn_in'][l]), [D_FF, D_FF])
    f = jnp.einsum('btf,fd->btd', jax.nn.silu(gate) * up, P['w_ffn_out'][l]).astype(x.dtype)
    x = x + _rms_norm(f, P['norm_ffn_post'][l])
    return x, (swa_k, swa_v, mc, mn, mm, rs, rshift)


def _trunk(x, layer_states, P):
    new = []
    for l in range(DEPTH):
        x, st = _layer(x, layer_states[l], P, l)
        new.append(st)
    return x, [jnp.stack([s[i] for s in new]) for i in range(7)]


def setup_inputs(seed: int = 0) -> dict:
    key = jax.random.key(seed)
    keys = iter(jax.random.split(key, 48))

    def nrm(shape, scale=1.0):
        return jax.random.normal(next(keys), shape, jnp.float32) * scale

    def gain(shape):
        return 1.0 + nrm(shape, 0.05)

    swa_len = min(WINDOW, PAST_LEN)
    forget_bias = jnp.linspace(FORGET_BIAS_LO, FORGET_BIAS_HI, A_HEADS, dtype=jnp.float32)
    return {
        'x_prompt': nrm((BATCH, SEQ, D_MODEL)),
        'x_sample': nrm((DEC_BATCH, DEC_SEQ, D_MODEL)),
        'cache_swa_k': nrm((DEPTH, DEC_BATCH, swa_len, B_KV, B_HD)),
        'cache_swa_v': nrm((DEPTH, DEC_BATCH, swa_len, B_KV, B_HD)),
        'state_mlstm_c': nrm((DEPTH, DEC_BATCH, A_HEADS, A_DK, A_DV), 0.5),
        'state_mlstm_n': nrm((DEPTH, DEC_BATCH, A_HEADS, A_DK), 0.5),
        'state_mlstm_m': nrm((DEPTH, DEC_BATCH, A_HEADS), 0.5),
        'state_rwkv_s': nrm((DEPTH, DEC_BATCH, C_HEADS, C_HD, C_HD), 0.5),
        'state_rwkv_shift': nrm((DEPTH, DEC_BATCH, 1, C_COLS)),
        'w_in': nrm((DEPTH, D_MODEL, IN_COLS), D_MODEL ** -0.5),
        'norm_mix_pre': gain((DEPTH, D_MODEL)),
        'norm_mix_post': gain((DEPTH, D_MODEL)),
        'norm_ffn_pre': gain((DEPTH, D_MODEL)),
        'norm_ffn_post': gain((DEPTH, D_MODEL)),
        'a_gate_bias': jnp.concatenate([nrm((DEPTH, A_HEADS), 0.1),
                                        forget_bias + nrm((DEPTH, A_HEADS), 0.1)], axis=-1),
        'a_norm': gain((DEPTH, A_W)),
        'rel_bias': nrm((NUM_BUCKETS, B_HEADS), 0.5),
        'b_sinks': nrm((DEPTH, B_HEADS), 0.5),
        'c_mu': jax.random.uniform(next(keys), (DEPTH, C_COLS), jnp.float32),
        'c_w0': nrm((DEPTH, C_W), 0.5) - 1.0,
        'c_w_w2': nrm((DEPTH, C_DECAY_LORA, C_W), 0.1),
        'c_a0': nrm((DEPTH, C_W), 0.1),
        'c_w_a2': nrm((DEPTH, C_AAA_LORA, C_W), 0.5 * C_AAA_LORA ** -0.5),
        'c_w_g2': nrm((DEPTH, C_GATE_LORA, C_W), C_GATE_LORA ** -0.5),
        'c_k_k': 0.85 + nrm((DEPTH, C_W), 0.05),
        'c_k_a': 1.0 + nrm((DEPTH, C_W), 0.05),
        'c_r_k': nrm((DEPTH, C_W), 0.1),
        'c_gn_g': gain((DEPTH, C_W)),
        'c_gn_b': nrm((DEPTH, C_W), 0.01),
        'w_branch_a': nrm((DEPTH, A_W, D_MODEL), A_W ** -0.5),
        'w_branch_b': nrm((DEPTH, B_W, D_MODEL), B_W ** -0.5),
        'w_branch_c': nrm((DEPTH, C_W, D_MODEL), C_W ** -0.5),
        'w_out': nrm((DEPTH, D_MODEL, D_MODEL), D_MODEL ** -0.5),
        'w_ffn_in': nrm((DEPTH, D_MODEL, 2 * D_FF), D_MODEL ** -0.5),
        'w_ffn_out': nrm((DEPTH, D_FF, D_MODEL), D_FF ** -0.5),
    }


def reference(x_prompt, x_sample, cache_swa_k, cache_swa_v, state_mlstm_c, state_mlstm_n, state_mlstm_m,
              state_rwkv_s, state_rwkv_shift, w_in, norm_mix_pre, norm_mix_post, norm_ffn_pre, norm_ffn_post,
              a_gate_bias, a_norm, rel_bias, b_sinks, c_mu, c_w0, c_w_w2, c_a0, c_w_a2, c_w_g2, c_k_k, c_k_a,
              c_r_k, c_gn_g, c_gn_b, w_branch_a, w_branch_b, w_branch_c, w_out, w_ffn_in, w_ffn_out):
    P = {
        'w_in': w_in, 'norm_mix_pre': norm_mix_pre, 'norm_mix_post': norm_mix_post,
        'norm_ffn_pre': norm_ffn_pre, 'norm_ffn_post': norm_ffn_post,
        'a_gate_bias': a_gate_bias, 'a_norm': a_norm, 'rel_bias': rel_bias, 'b_sinks': b_sinks,
        'c_mu': c_mu, 'c_w0': c_w0, 'c_w_w2': c_w_w2, 'c_a0': c_a0, 'c_w_a2': c_w_a2, 'c_w_g2': c_w_g2,
        'c_k_k': c_k_k, 'c_k_a': c_k_a, 'c_r_k': c_r_k, 'c_gn_g': c_gn_g, 'c_gn_b': c_gn_b,
        'w_branch_a': w_branch_a, 'w_branch_b': w_branch_b, 'w_branch_c': w_branch_c, 'w_out': w_out,
        'w_ffn_in': w_ffn_in, 'w_ffn_out': w_ffn_out,
    }
    f32 = jnp.float32
    bp = x_prompt.shape[0]
    fresh = (None, None,
             jnp.zeros((bp, A_HEADS, A_DK, A_DV), f32), jnp.zeros((bp, A_HEADS, A_DK), f32),
             jnp.zeros((bp, A_HEADS), f32), jnp.zeros((bp, C_HEADS, C_HD, C_HD), f32),
             jnp.zeros((bp, 1, C_COLS), f32))
    y_prompt, p_states = _trunk(x_prompt, [fresh] * DEPTH, P)
    carried = [(cache_swa_k[l], cache_swa_v[l], state_mlstm_c[l], state_mlstm_n[l], state_mlstm_m[l],
                state_rwkv_s[l], state_rwkv_shift[l]) for l in range(DEPTH)]
    y_sample, s_states = _trunk(x_sample, carried, P)
    p_swa_k, p_swa_v, p_mlstm_c, p_mlstm_n, p_mlstm_m, p_rwkv_s, p_rwkv_shift = p_states
    s_swa_k, s_swa_v, s_mlstm_c, s_mlstm_n, s_mlstm_m, s_rwkv_s, s_rwkv_shift = s_states
    return (y_prompt, y_sample,
            p_swa_k, p_swa_v, p_mlstm_c, p_mlstm_n, p_mlstm_m, p_rwkv_s, p_rwkv_shift,
            s_swa_k, s_swa_v, s_mlstm_c, s_mlstm_n, s_mlstm_m, s_rwkv_s, s_rwkv_shift)
```

```python
import functools
import math

import jax
import jax.numpy as jnp
from jax import lax
from jax.experimental import pallas as pl
from jax.experimental.pallas import tpu as pltpu

F32 = jnp.float32
BF16 = jnp.bfloat16

D_MODEL = 1024
DEPTH = 4
CHUNK = 64
NORM_EPS = 1e-6

A_HEADS = 4
A_DK = 128
A_W = 512
B_HEADS = 8
B_KV = 2
B_GROUP = 4
B_HD = 64
B_W = 512
WINDOW = 128
WIN_CHUNKS = 2
NUM_BUCKETS = 32
REL_MAX_DIST = 256
C_HEADS = 8
C_HD = 64
C_W = 512
C_DECAY_LORA = 64
C_AAA_LORA = 64
C_GATE_LORA = 128
C_COLS = 1792
GN_EPS = 64e-5
D_FF = 2816

COL_A = 0
COL_C = 2048
COL_C_PAD = 2048
COL_BQ = 4096
COL_BK = 4608
COL_BV = 4736
COL_AG = 4864
COL_G = 5120
N_COLS = 8192

LANES = 128
VMEM_LIMIT = 56 * 1024 * 1024
HI = lax.Precision.HIGHEST

NT = (((1,), (1,)), ((), ()))
TN = (((0,), (0,)), ((), ()))


def _cparams(*sem):
    return pltpu.CompilerParams(dimension_semantics=sem, vmem_limit_bytes=VMEM_LIMIT)


def _tri(n, strict=False):
    r = lax.broadcasted_iota(jnp.int32, (n, n), 0)
    c = lax.broadcasted_iota(jnp.int32, (n, n), 1)
    return (c < r) if strict else (c <= r)


def _softplus(z):
    return jnp.maximum(z, 0.0) + jnp.log1p(jnp.exp(-jnp.abs(z)))


def _proj_in_kernel(x_ref, g_ref, w_ref, o_ref, xn_ref):
    @pl.when(pl.program_id(1) == 0)
    def _():
        x = x_ref[...]
        y = x * lax.rsqrt(jnp.mean(x * x, axis=-1, keepdims=True) + NORM_EPS) * g_ref[...]
        xn_ref[...] = y.astype(BF16)

    o_ref[...] = jnp.dot(xn_ref[...], w_ref[...], preferred_element_type=F32)


def _proj_in(x, g, w, tm, tn):
    t, d = x.shape
    n = w.shape[1]
    return pl.pallas_call(
        _proj_in_kernel,
        grid=(t // tm, n // tn),
        in_specs=[
            pl.BlockSpec((tm, d), lambda i, j: (i, 0)),
            pl.BlockSpec((1, d), lambda i, j: (0, 0)),
            pl.BlockSpec((d, tn), lambda i, j: (0, j)),
        ],
        out_specs=pl.BlockSpec((tm, tn), lambda i, j: (i, j)),
        out_shape=jax.ShapeDtypeStruct((t, n), F32),
        scratch_shapes=[pltpu.VMEM((tm, d), BF16)],
        compiler_params=_cparams("parallel", "arbitrary"),
    )(x, g, w)


def _mlstm_kernel(q_ref, k_ref, v_ref, o_ref, g_ref, c0_ref, n0_ref, m0_ref, gb_ref, ng_ref,
                  h_ref, c_ref, n_ref, m_ref, *, L):
    @pl.when(pl.program_id(1) == 0)
    def _():
        c_ref[...] = c0_ref[...]
        n_ref[...] = n0_ref[...]
        m_ref[...] = m0_ref[...]

    lane = lax.broadcasted_iota(jnp.int32, (L, LANES), 1)
    gates = g_ref[0] + gb_ref[...]
    lf = jnp.minimum(gates, 0.0) - jnp.log1p(jnp.exp(-jnp.abs(gates)))
    cum = jnp.dot(_tri(L).astype(F32), lf, precision=HI, preferred_element_type=F32)
    z = jnp.where(lane < A_HEADS, gates, cum)
    sel = (lax.broadcasted_iota(jnp.int32, (8, LANES), 0)
           == lax.broadcasted_iota(jnp.int32, (8, LANES), 1)).astype(F32)
    zrow = lax.dot_general(sel, z, NT, precision=HI, preferred_element_type=F32)
    causal = _tri(L)
    last = pl.ds(L - 1, 1)

    for h in range(A_HEADS):
        hs = pl.ds(h * A_DK, A_DK)
        q = q_ref[0, :, hs]
        k = k_ref[0, :, hs] * (A_DK ** -0.5)
        v = v_ref[0, :, hs]
        f_col = z[:, A_HEADS + h:A_HEADS + h + 1]
        li_col = z[:, h:h + 1]
        f_row = zrow[A_HEADS + h:A_HEADS + h + 1, :]
        li_row = zrow[h:h + 1, :]
        m_prev = m_ref[0, :, h:h + 1]
        c_mat = c_ref[0, h]
        n_vec = n_ref[0, h:h + 1, :]

        dlog = jnp.where(causal, f_col - f_row + li_row, -jnp.inf)
        m_inter = f_col + m_prev
        m_t = jnp.maximum(m_inter, jnp.max(dlog, axis=-1, keepdims=True))
        s = lax.dot_general(q, k, NT, preferred_element_type=F32)
        w = jnp.exp(dlog - m_t) * s
        inter = jnp.exp(m_inter - m_t)
        num = (jnp.dot(w, v, preferred_element_type=F32)
               + inter * jnp.dot(q, c_mat, preferred_element_type=F32))
        den = jnp.sum(w, axis=-1, keepdims=True) + inter * jnp.sum(q * n_vec, axis=-1, keepdims=True)
        hh = num / jnp.maximum(jnp.abs(den), jnp.exp(-m_t))

        m_new = m_t[L - 1:L, :]
        f_last = f_col[L - 1:L, :]
        wk = jnp.exp(f_last - f_col + li_col - m_new)
        decay = jnp.exp(f_last + m_prev - m_new)
        kw = wk * k
        c_ref[0, h] = decay * c_mat + lax.dot_general(kw, v, TN, preferred_element_type=F32)
        n_ref[0, h:h + 1, :] = decay * n_vec + jnp.sum(kw, axis=0, keepdims=True)
        m_ref[0, :, h:h + 1] = m_new

        hn = hh * lax.rsqrt(jnp.mean(hh * hh, axis=-1, keepdims=True) + NORM_EPS)
        h_ref[0, :, hs] = jax.nn.sigmoid(o_ref[0, :, hs]) * (hn * ng_ref[:, hs])


def _mlstm(cols, c0, n0, m0, gate_bias, norm_g, L):
    b, t, _ = cols.shape
    nc = t // L
    wb = A_W // A_W

    def col(idx):
        return pl.BlockSpec((1, L, A_W), lambda i, c: (i, c, idx * wb))

    state = lambda shape: pl.BlockSpec(shape, lambda i, c: (i,) + (0,) * (len(shape) - 1))
    return pl.pallas_call(
        functools.partial(_mlstm_kernel, L=L),
        grid=(b, nc),
        in_specs=[
            col(0), col(1), col(2), col(3),
            pl.BlockSpec((1, L, LANES), lambda i, c: (i, c, COL_AG // LANES)),
            state((1, A_HEADS, A_DK, A_DK)), state((1, A_HEADS, A_DK)), state((1, 1, A_HEADS)),
            pl.BlockSpec((1, LANES), lambda i, c: (0, 0)),
            pl.BlockSpec((1, A_W), lambda i, c: (0, 0)),
        ],
        out_specs=[
            pl.BlockSpec((1, L, A_W), lambda i, c: (i, c, 0)),
            state((1, A_HEADS, A_DK, A_DK)), state((1, A_HEADS, A_DK)), state((1, 1, A_HEADS)),
        ],
        out_shape=[
            jax.ShapeDtypeStruct((b, t, A_W), F32),
            jax.ShapeDtypeStruct((b, A_HEADS, A_DK, A_DK), F32),
            jax.ShapeDtypeStruct((b, A_HEADS, A_DK), F32),
            jax.ShapeDtypeStruct((b, 1, A_HEADS), F32),
        ],
        compiler_params=_cparams("parallel", "arbitrary"),
    )(cols, cols, cols, cols, cols, c0, n0, m0, gate_bias, norm_g)


def _swa_kernel(*refs, n_kv, first_valid_fn):
    q_ref = refs[0]
    k_refs = refs[1:1 + n_kv]
    v_refs = refs[1 + n_kv:1 + 2 * n_kv]
    bias_ref, sink_ref, o_ref = refs[1 + 2 * n_kv:]
    k = jnp.concatenate([r[0] for r in k_refs], axis=0) if n_kv > 1 else k_refs[0][0]
    v = jnp.concatenate([r[0] for r in v_refs], axis=0) if n_kv > 1 else v_refs[0][0]
    lq = q_ref.shape[1]
    lk = k.shape[0]
    first_valid = first_valid_fn(pl.program_id(1))
    valid = lax.broadcasted_iota(jnp.int32, (lq, lk), 1) >= first_valid
    for h in range(B_HEADS):
        n = h // B_GROUP
        qh = q_ref[0, :, pl.ds(h * B_HD, B_HD)]
        kn = k[:, n * B_HD:(n + 1) * B_HD]
        vn = v[:, n * B_HD:(n + 1) * B_HD]
        s = lax.dot_general(qh, kn, NT, preferred_element_type=F32) * (B_HD ** -0.5) + bias_ref[h]
        s = jnp.where(valid, s, -1e30)
        sk = sink_ref[:, h:h + 1]
        mx = jnp.maximum(jnp.max(s, axis=-1, keepdims=True), sk)
        p = jnp.exp(s - mx)
        p = p / (jnp.sum(p, axis=-1, keepdims=True) + jnp.exp(sk - mx))
        o_ref[0, :, pl.ds(h * B_HD, B_HD)] = jnp.dot(p, vn, preferred_element_type=F32)


def _swa_prompt(cols, bias, sinks):
    b, t, _ = cols.shape
    nc = t // CHUNK
    n_kv = WIN_CHUNKS + 1
    kw = B_KV * B_HD

    def kv_spec(col0, back):
        return pl.BlockSpec((1, CHUNK, kw), lambda i, c: (i, jnp.maximum(c - back, 0), col0 // kw))

    backs = list(range(WIN_CHUNKS, -1, -1))
    return pl.pallas_call(
        functools.partial(_swa_kernel, n_kv=n_kv, first_valid_fn=lambda c: (WIN_CHUNKS - c) * CHUNK),
        grid=(b, nc),
        in_specs=(
            [pl.BlockSpec((1, CHUNK, B_W), lambda i, c: (i, c, COL_BQ // B_W))]
            + [kv_spec(COL_BK, bk) for bk in backs]
            + [kv_spec(COL_BV, bk) for bk in backs]
            + [pl.BlockSpec(bias.shape, lambda i, c: (0, 0, 0)),
               pl.BlockSpec((1, B_HEADS), lambda i, c: (0, 0))]
        ),
        out_specs=pl.BlockSpec((1, CHUNK, B_W), lambda i, c: (i, c, 0)),
        out_shape=jax.ShapeDtypeStruct((b, t, B_W), F32),
        compiler_params=_cparams("parallel", "parallel"),
    )(cols, *([cols] * (2 * n_kv)), bias, sinks)


def _swa_sample(cols, kf, vf, bias, sinks):
    b, t, _ = cols.shape
    lk = kf.shape[1]
    kw = B_KV * B_HD
    return pl.pallas_call(
        functools.partial(_swa_kernel, n_kv=1, first_valid_fn=lambda c: 0),
        grid=(b, 1),
        in_specs=[
            pl.BlockSpec((1, t, B_W), lambda i, c: (i, 0, COL_BQ // B_W)),
            pl.BlockSpec((1, lk, kw), lambda i, c: (i, 0, 0)),
            pl.BlockSpec((1, lk, kw), lambda i, c: (i, 0, 0)),
            pl.BlockSpec(bias.shape, lambda i, c: (0, 0, 0)),
            pl.BlockSpec((1, B_HEADS), lambda i, c: (0, 0)),
        ],
        out_specs=pl.BlockSpec((1, t, B_W), lambda i, c: (i, 0, 0)),
        out_shape=jax.ShapeDtypeStruct((b, t, B_W), F32),
        compiler_params=_cparams("parallel", "arbitrary"),
    )(cols, kf, vf, bias, sinks)


def _rwkv_kernel(c_ref, sh0_ref, s0_ref, mu_ref, w0_ref, ww2_ref, a0_ref, wa2_ref, wg2_ref,
                 kk_ref, ka_ref, rk_ref, gg_ref, gb_ref, y_ref, s_ref, sh_ref, *, L):
    @pl.when(pl.program_id(1) == 0)
    def _():
        s_ref[...] = s0_ref[...]
        sh_ref[...] = sh0_ref[...]

    cc = c_ref[0][:, :C_COLS]
    row = lax.broadcasted_iota(jnp.int32, (L, C_COLS), 0)
    prev = jnp.where(row == 0, sh_ref[0], pltpu.roll(cc, 1, axis=0))
    sh_ref[0] = cc[L - 1:L, :]
    xm = cc + (prev - cc) * mu_ref[...]
    r = xm[:, 0:C_W]
    k = xm[:, C_W:2 * C_W]
    v = xm[:, 2 * C_W:3 * C_W]
    o1 = 3 * C_W
    wl = xm[:, o1:o1 + C_DECAY_LORA]
    al = xm[:, o1 + C_DECAY_LORA:o1 + C_DECAY_LORA + C_AAA_LORA]
    gl = xm[:, o1 + C_DECAY_LORA + C_AAA_LORA:C_COLS]

    wlog = -_softplus(-(w0_ref[...] + jnp.dot(jnp.tanh(wl), ww2_ref[...], preferred_element_type=F32))) - 0.5
    lw = -jnp.exp(wlog)
    a = jax.nn.sigmoid(a0_ref[...] + jnp.dot(al, wa2_ref[...], preferred_element_type=F32))
    g = jnp.dot(jax.nn.sigmoid(gl), wg2_ref[...], preferred_element_type=F32)
    kkf = k * kk_ref[...]
    kx = k * (1.0 + (a - 1.0) * ka_ref[...])

    cum = jnp.dot(_tri(L).astype(F32), lw, precision=HI, preferred_element_type=F32)
    e_in = jnp.exp(cum)
    e_ex = jnp.exp(cum - lw)
    e_neg = jnp.exp(-cum)
    cum_last = cum[L - 1:L, :]
    e_last = jnp.exp(cum_last - cum)
    w_chunk = jnp.exp(cum_last)

    strict = _tri(L, strict=True)
    incl = _tri(L)
    eye = (lax.broadcasted_iota(jnp.int32, (L, L), 0) == lax.broadcasted_iota(jnp.int32, (L, L), 1)).astype(F32)
    n_sq = int(math.log2(L)) - 1

    for h in range(C_HEADS):
        hs = slice(h * C_HD, (h + 1) * C_HD)
        kk_h = kkf[:, hs]
        kk_h = kk_h / jnp.maximum(jnp.sqrt(jnp.sum(kk_h * kk_h, axis=-1, keepdims=True)), 1e-12)
        a_h = a[:, hs]
        r_h = r[:, hs]
        kx_h = kx[:, hs]
        v_h = v[:, hs]
        b_h = kk_h * a_h
        kt = kk_h * e_ex[:, hs]
        rt = r_h * e_in[:, hs]
        bh = b_h * e_neg[:, hs]
        kh = kx_h * e_neg[:, hs]
        s0 = s_ref[0, h]

        lhs = jnp.concatenate([kt, rt], axis=0)
        rhs = jnp.concatenate([bh, kh], axis=0)
        g4 = lax.dot_general(lhs, rhs, NT, preferred_element_type=F32)
        tb = jnp.where(strict, g4[:L, :L], 0.0)
        tk = jnp.where(strict, g4[:L, L:], 0.0)
        qb = jnp.where(incl, g4[L:, :L], 0.0)
        qk = jnp.where(incl, g4[L:, L:], 0.0)

        x = eye - tb
        p = tb
        for _ in range(n_sq):
            p = jnp.dot(p, p, preferred_element_type=F32)
            x = x + jnp.dot(x, p, preferred_element_type=F32)

        ks = lax.dot_general(lhs, s0, NT, preferred_element_type=F32)
        u = jnp.dot(x, ks[:L] + jnp.dot(tk, v_h, preferred_element_type=F32), preferred_element_type=F32)
        y = (ks[L:] + jnp.dot(qk, v_h, preferred_element_type=F32)
             - jnp.dot(qb, u, preferred_element_type=F32))
        kl = kx_h * e_last[:, hs]
        bl = b_h * e_last[:, hs]
        s_ref[0, h] = (s0 * w_chunk[:, hs]
                       + lax.dot_general(v_h, kl, TN, preferred_element_type=F32)
                       - lax.dot_general(u, bl, TN, preferred_element_type=F32))

        mean = jnp.mean(y, axis=-1, keepdims=True)
        var = jnp.mean(jnp.square(y - mean), axis=-1, keepdims=True)
        yn = (y - mean) * lax.rsqrt(var + GN_EPS) * gg_ref[:, hs] + gb_ref[:, hs]
        bonus = jnp.sum(r_h * kx_h * rk_ref[:, hs], axis=-1, keepdims=True) * v_h
        y_ref[0, :, pl.ds(h * C_HD, C_HD)] = (yn + bonus) * g[:, hs]


def _rwkv(cols, shift0, s0, lp, L):
    b, t, _ = cols.shape
    nc = t // L
    state = lambda shape: pl.BlockSpec(shape, lambda i, c: (i,) + (0,) * (len(shape) - 1))
    par = lambda arr: pl.BlockSpec(arr.shape, lambda i, c: (0, 0))
    params = [lp['c_mu'], lp['c_w0'], lp['c_w_w2'], lp['c_a0'], lp['c_w_a2'], lp['c_w_g2'],
              lp['c_k_k'], lp['c_k_a'], lp['c_r_k'], lp['c_gn_g'], lp['c_gn_b']]
    return pl.pallas_call(
        functools.partial(_rwkv_kernel, L=L),
        grid=(b, nc),
        in_specs=[pl.BlockSpec((1, L, COL_C_PAD), lambda i, c: (i, c, COL_C // COL_C_PAD)),
                  state((1, 1, C_COLS)), state((1, C_HEADS, C_HD, C_HD))] + [par(p) for p in params],
        out_specs=[pl.BlockSpec((1, L, C_W), lambda i, c: (i, c, 0)),
                   state((1, C_HEADS, C_HD, C_HD)), state((1, 1, C_COLS))],
        out_shape=[jax.ShapeDtypeStruct((b, t, C_W), F32),
                   jax.ShapeDtypeStruct((b, C_HEADS, C_HD, C_HD), F32),
                   jax.ShapeDtypeStruct((b, 1, C_COLS), F32)],
        compiler_params=_cparams("parallel", "arbitrary"),
    )(cols, shift0, s0, *params)


def _merge_kernel(x_ref, g0_ref, g1_ref, g2_ref, ha_ref, hb_ref, hc_ref, wa_ref, wb_ref, wc_ref, wo_ref,
                  ng_ref, o_ref):
    def branch(g_ref, h_ref, w_ref):
        return jax.nn.sigmoid(g_ref[...]) * jnp.dot(h_ref[...].astype(BF16), w_ref[...],
                                                    preferred_element_type=F32)

    merged = branch(g0_ref, ha_ref, wa_ref) + branch(g1_ref, hb_ref, wb_ref) + branch(g2_ref, hc_ref, wc_ref)
    mix = jnp.dot(merged.astype(BF16), wo_ref[...], preferred_element_type=F32)
    y = mix * lax.rsqrt(jnp.mean(mix * mix, axis=-1, keepdims=True) + NORM_EPS) * ng_ref[...]
    o_ref[...] = x_ref[...] + y


def _merge(x, cols, h_a, h_b, h_c, wa, wb, wc, wo, ng, tm):
    t, d = x.shape
    gate = lambda idx: pl.BlockSpec((tm, d), lambda i: (i, COL_G // d + idx))
    tok = lambda w: pl.BlockSpec((tm, w), lambda i: (i, 0))
    full = lambda arr: pl.BlockSpec(arr.shape, lambda i: (0, 0))
    return pl.pallas_call(
        _merge_kernel,
        grid=(t // tm,),
        in_specs=[tok(d), gate(0), gate(1), gate(2), tok(A_W), tok(B_W), tok(C_W),
                  full(wa), full(wb), full(wc), full(wo), full(ng)],
        out_specs=tok(d),
        out_shape=jax.ShapeDtypeStruct((t, d), F32),
        compiler_params=_cparams("parallel"),
    )(x, cols, cols, cols, h_a, h_b, h_c, wa, wb, wc, wo, ng)


def _ffn_kernel(x_ref, gpre_ref, wg_ref, wu_ref, wo_ref, gpost_ref, o_ref, xn_ref, acc_ref):
    j = pl.program_id(1)

    @pl.when(j == 0)
    def _():
        x = x_ref[...]
        y = x * lax.rsqrt(jnp.mean(x * x, axis=-1, keepdims=True) + NORM_EPS) * gpre_ref[...]
        xn_ref[...] = y.astype(BF16)
        acc_ref[...] = jnp.zeros_like(acc_ref)

    xn = xn_ref[...]
    gate = jnp.dot(xn, wg_ref[...], preferred_element_type=F32)
    up = jnp.dot(xn, wu_ref[...], preferred_element_type=F32)
    act = (gate * jax.nn.sigmoid(gate) * up).astype(BF16)
    acc_ref[...] += jnp.dot(act, wo_ref[...], preferred_element_type=F32)

    @pl.when(j == pl.num_programs(1) - 1)
    def _():
        f = acc_ref[...]
        y = f * lax.rsqrt(jnp.mean(f * f, axis=-1, keepdims=True) + NORM_EPS) * gpost_ref[...]
        o_ref[...] = x_ref[...] + y


def _ffn(x, gpre, w_in, w_out, gpost, tm, tf):
    t, d = x.shape
    nf = D_FF // tf
    return pl.pallas_call(
        _ffn_kernel,
        grid=(t // tm, nf),
        in_specs=[
            pl.BlockSpec((tm, d), lambda i, j: (i, 0)),
            pl.BlockSpec((1, d), lambda i, j: (0, 0)),
            pl.BlockSpec((d, tf), lambda i, j: (0, j)),
            pl.BlockSpec((d, tf), lambda i, j: (0, nf + j)),
            pl.BlockSpec((tf, d), lambda i, j: (j, 0)),
            pl.BlockSpec((1, d), lambda i, j: (0, 0)),
        ],
        out_specs=pl.BlockSpec((tm, d), lambda i, j: (i, 0)),
        out_shape=jax.ShapeDtypeStruct((t, d), F32),
        scratch_shapes=[pltpu.VMEM((tm, d), BF16), pltpu.VMEM((tm, d), F32)],
        compiler_params=_cparams("parallel", "arbitrary"),
    )(x, gpre, w_in, w_in, w_out, gpost)


def _rel_bucket(rel):
    half = NUM_BUCKETS // 2
    exact = half // 2
    n = jnp.abs(rel)
    far = exact + (jnp.log(jnp.maximum(n, 1).astype(F32) / exact)
                   / math.log(REL_MAX_DIST / exact) * (half - exact)).astype(jnp.int32)
    far = jnp.minimum(far, half - 1)
    return jnp.where(rel > 0, half, 0) + jnp.where(n < exact, n, far)


def _rel_bias(table, n_q, n_before):
    rel = (jnp.arange(n_before + n_q)[None, :] - n_before) - jnp.arange(n_q)[:, None]
    return jnp.transpose(table[_rel_bucket(rel)].astype(F32), (2, 0, 1))


def _reorder_w_in(w):
    a_main = 4 * A_W
    a_all = a_main + 2 * A_HEADS
    b_all = B_W + 2 * B_KV * B_HD
    wa, wag = w[..., :a_main], w[..., a_main:a_all]
    wb = w[..., a_all:a_all + b_all]
    wc = w[..., a_all + b_all:a_all + b_all + C_COLS]
    wg = w[..., a_all + b_all + C_COLS:]
    z = lambda n: jnp.zeros(w.shape[:-1] + (n,), w.dtype)
    out = jnp.concatenate([wa, wc, z(COL_BQ - COL_C - C_COLS), wb, wag, z(COL_G - COL_AG - 2 * A_HEADS), wg],
                          axis=-1)
    assert out.shape[-1] == N_COLS
    return out.astype(BF16)


def _layer(x, st, lp, bias, L, tm):
    b, t, d = x.shape
    swa_k, swa_v, mc, mn, mm, rs, rshift = st
    x2 = x.reshape(b * t, d)
    cols2 = _proj_in(x2, lp['norm_mix_pre'], lp['w_in'], tm, 512)
    cols = cols2.reshape(b, t, N_COLS)

    h_a, mc, mn, mm = _mlstm(cols, mc, mn, mm, lp['a_gate_bias'], lp['a_norm'], L)

    k_new = cols[:, :, COL_BK:COL_BK + B_KV * B_HD]
    v_new = cols[:, :, COL_BV:COL_BV + B_KV * B_HD]
    if swa_k is None:
        h_b = _swa_prompt(cols, bias, lp['b_sinks'])
        keep = min(WINDOW, t)
        swa_k, swa_v = k_new[:, t - keep:], v_new[:, t - keep:]
    else:
        kf = jnp.concatenate([swa_k, k_new], axis=1)
        vf = jnp.concatenate([swa_v, v_new], axis=1)
        h_b = _swa_sample(cols, kf, vf, bias, lp['b_sinks'])
        swa_k, swa_v = kf[:, t:], vf[:, t:]

    h_c, rs, rshift = _rwkv(cols, rshift, rs, lp, L)

    x2 = _merge(x2, cols2, h_a.reshape(b * t, A_W), h_b.reshape(b * t, B_W), h_c.reshape(b * t, C_W),
                lp['w_branch_a'], lp['w_branch_b'], lp['w_branch_c'], lp['w_out'], lp['norm_mix_post'], tm // 2)
    x2 = _ffn(x2, lp['norm_ffn_pre'], lp['w_ffn_in'], lp['w_ffn_out'], lp['norm_ffn_post'], tm, 256)
    return x2.reshape(b, t, d), (swa_k, swa_v, mc, mn, mm, rs, rshift)


def _trunk(x, layer_states, layers, bias, L, tm):
    new = []
    for l in range(DEPTH):
        x, st = _layer(x, layer_states[l], layers[l], bias, L, tm)
        new.append(st)
    return x, [jnp.stack([s[i] for s in new]) for i in range(7)]


def kernel(x_prompt, x_sample, cache_swa_k, cache_swa_v, state_mlstm_c, state_mlstm_n, state_mlstm_m, state_rwkv_s, state_rwkv_shift, w_in, norm_mix_pre, norm_mix_post, norm_ffn_pre, norm_ffn_post, a_gate_bias, a_norm, rel_bias, b_sinks, c_mu, c_w0, c_w_w2, c_a0, c_w_a2, c_w_g2, c_k_k, c_k_a, c_r_k, c_gn_g, c_gn_b, w_branch_a, w_branch_b, w_branch_c, w_out, w_ffn_in, w_ffn_out):
    bp, tp, _ = x_prompt.shape
    bs, ts, _ = x_sample.shape
    kw = B_KV * B_HD
    w_in_p = _reorder_w_in(w_in)
    gate_bias = jnp.pad(a_gate_bias, ((0, 0), (0, LANES - 2 * A_HEADS)))
    row = lambda p, l: p[l][None, :]
    layers = []
    for l in range(DEPTH):
        layers.append({
            'w_in': w_in_p[l], 'norm_mix_pre': row(norm_mix_pre, l), 'norm_mix_post': row(norm_mix_post, l),
            'norm_ffn_pre': row(norm_ffn_pre, l), 'norm_ffn_post': row(norm_ffn_post, l),
            'a_gate_bias': row(gate_bias, l), 'a_norm': row(a_norm, l), 'b_sinks': row(b_sinks, l),
            'c_mu': row(c_mu, l), 'c_w0': row(c_w0, l), 'c_w_w2': c_w_w2[l], 'c_a0': row(c_a0, l),
            'c_w_a2': c_w_a2[l], 'c_w_g2': c_w_g2[l], 'c_k_k': row(c_k_k, l), 'c_k_a': row(c_k_a, l),
            'c_r_k': row(c_r_k, l), 'c_gn_g': row(c_gn_g, l), 'c_gn_b': row(c_gn_b, l),
            'w_branch_a': w_branch_a[l].astype(BF16), 'w_branch_b': w_branch_b[l].astype(BF16),
            'w_branch_c': w_branch_c[l].astype(BF16), 'w_out': w_out[l].astype(BF16),
            'w_ffn_in': w_ffn_in[l].astype(BF16), 'w_ffn_out': w_ffn_out[l].astype(BF16),
        })

    fresh = (None, None,
             jnp.zeros((bp, A_HEADS, A_DK, A_DK), F32), jnp.zeros((bp, A_HEADS, A_DK), F32),
             jnp.zeros((bp, 1, A_HEADS), F32), jnp.zeros((bp, C_HEADS, C_HD, C_HD), F32),
             jnp.zeros((bp, 1, C_COLS), F32))
    lp_chunk = min(CHUNK, tp)
    y_prompt, p_st = _trunk(x_prompt, [fresh] * DEPTH, layers,
                            _rel_bias(rel_bias, lp_chunk, WIN_CHUNKS * CHUNK), lp_chunk, min(1024, bp * tp))

    n_before = cache_swa_k.shape[2]
    carried = [(cache_swa_k[l].reshape(bs, n_before, kw), cache_swa_v[l].reshape(bs, n_before, kw),
                state_mlstm_c[l], state_mlstm_n[l], state_mlstm_m[l][:, None, :],
                state_rwkv_s[l], state_rwkv_shift[l]) for l in range(DEPTH)]
    ls_chunk = min(CHUNK, ts)
    y_sample, s_st = _trunk(x_sample, carried, layers, _rel_bias(rel_bias, ts, n_before), ls_chunk,
                            min(1024, bs * ts))

    def finish(st, b):
        swa_k, swa_v, mc, mn, mm, rs, rshift = st
        n_rows = swa_k.shape[2]
        return (swa_k.reshape(DEPTH, b, n_rows, B_KV, B_HD), swa_v.reshape(DEPTH, b, n_rows, B_KV, B_HD),
                mc, mn, mm.reshape(DEPTH, b, A_HEADS), rs, rshift)

    return (y_prompt, y_sample) + finish(p_st, bp) + finish(s_st, bs)
```

```python
import functools
import math

import jax
import jax.numpy as jnp
from jax import lax
from jax.experimental import pallas as pl
from jax.experimental.pallas import tpu as pltpu

F32 = jnp.float32
BF16 = jnp.bfloat16

D_MODEL = 1024
DEPTH = 4
CHUNK = 64
NORM_EPS = 1e-6

A_HEADS = 4
A_DK = 128
A_W = 512
B_HEADS = 8
B_KV = 2
B_GROUP = 4
B_HD = 64
B_W = 512
WINDOW = 128
WIN_CHUNKS = 2
NUM_BUCKETS = 32
REL_MAX_DIST = 256
C_HEADS = 8
C_HD = 64
C_W = 512
C_DECAY_LORA = 64
C_AAA_LORA = 64
C_GATE_LORA = 128
C_COLS = 1792
GN_EPS = 64e-5
D_FF = 2816

COL_A = 0
COL_C = 2048
COL_C_PAD = 2048
COL_BQ = 4096
COL_BK = 4608
COL_BV = 4736
COL_AG = 4864
COL_G = 5120
N_COLS = 8192

LANES = 128
VMEM_LIMIT = 56 * 1024 * 1024
HI = lax.Precision.HIGHEST

NT = (((1,), (1,)), ((), ()))
TN = (((0,), (0,)), ((), ()))


def _cparams(*sem):
    return pltpu.CompilerParams(dimension_semantics=sem, vmem_limit_bytes=VMEM_LIMIT)


def _tri(n, strict=False):
    r = lax.broadcasted_iota(jnp.int32, (n, n), 0)
    c = lax.broadcasted_iota(jnp.int32, (n, n), 1)
    return (c < r) if strict else (c <= r)


def _softplus(z):
    return jnp.maximum(z, 0.0) + jnp.log1p(jnp.exp(-jnp.abs(z)))


def _proj_in_kernel(x_ref, g_ref, w_ref, o_ref, xn_ref):
    @pl.when(pl.program_id(1) == 0)
    def _():
        x = x_ref[...]
        y = x * lax.rsqrt(jnp.mean(x * x, axis=-1, keepdims=True) + NORM_EPS) * g_ref[...]
        xn_ref[...] = y.astype(BF16)

    o_ref[...] = jnp.dot(xn_ref[...], w_ref[...], preferred_element_type=F32)


def _proj_in(x, g, w, tm, tn):
    t, d = x.shape
    n = w.shape[1]
    return pl.pallas_call(
        _proj_in_kernel,
        grid=(t // tm, n // tn),
        in_specs=[
            pl.BlockSpec((tm, d), lambda i, j: (i, 0)),
            pl.BlockSpec((1, d), lambda i, j: (0, 0)),
            pl.BlockSpec((d, tn), lambda i, j: (0, j)),
        ],
        out_specs=pl.BlockSpec((tm, tn), lambda i, j: (i, j)),
        out_shape=jax.ShapeDtypeStruct((t, n), F32),
        scratch_shapes=[pltpu.VMEM((tm, d), BF16)],
        compiler_params=_cparams("parallel", "arbitrary"),
    )(x, g, w)


def _mlstm_kernel(q_ref, k_ref, v_ref, o_ref, g_ref, c0_ref, n0_ref, m0_ref, gb_ref, ng_ref,
                  h_ref, c_ref, n_ref, m_ref, *, L):
    @pl.when(pl.program_id(1) == 0)
    def _():
        c_ref[...] = c0_ref[...]
        n_ref[...] = n0_ref[...]
        m_ref[...] = m0_ref[...]

    lane = lax.broadcasted_iota(jnp.int32, (L, LANES), 1)
    gates = g_ref[0] + gb_ref[...]
    lf = jnp.minimum(gates, 0.0) - jnp.log1p(jnp.exp(-jnp.abs(gates)))
    cum = jnp.dot(_tri(L).astype(F32), lf, precision=HI, preferred_element_type=F32)
    z = jnp.where(lane < A_HEADS, gates, cum)
    sel = (lax.broadcasted_iota(jnp.int32, (8, LANES), 0)
           == lax.broadcasted_iota(jnp.int32, (8, LANES), 1)).astype(F32)
    zrow = lax.dot_general(sel, z, NT, precision=HI, preferred_element_type=F32)
    causal = _tri(L)
    heads = range(A_HEADS)
    hsl = [pl.ds(h * A_DK, A_DK) for h in heads]

    q = [q_ref[0, :, hsl[h]] for h in heads]
    k = [k_ref[0, :, hsl[h]] * (A_DK ** -0.5) for h in heads]
    v = [v_ref[0, :, hsl[h]] for h in heads]
    c_mat = [c_ref[0, h] for h in heads]
    n_vec = [n_ref[0, h:h + 1, :] for h in heads]
    m_prev = [m_ref[0, :, h:h + 1] for h in heads]
    f_col = [z[:, A_HEADS + h:A_HEADS + h + 1] for h in heads]
    li_col = [z[:, h:h + 1] for h in heads]

    s = [lax.dot_general(q[h], k[h], NT, preferred_element_type=F32) for h in heads]
    qc = [jnp.dot(q[h], c_mat[h], preferred_element_type=F32) for h in heads]

    m_t, w, inter = [], [], []
    for h in heads:
        f_row = zrow[A_HEADS + h:A_HEADS + h + 1, :]
        li_row = zrow[h:h + 1, :]
        dlog = jnp.where(causal, f_col[h] - f_row + li_row, -jnp.inf)
        m_inter = f_col[h] + m_prev[h]
        m_t.append(jnp.maximum(m_inter, jnp.max(dlog, axis=-1, keepdims=True)))
        w.append(jnp.exp(dlog - m_t[h]) * s[h])
        inter.append(jnp.exp(m_inter - m_t[h]))

    kw, decay = [], []
    for h in heads:
        m_new = m_t[h][L - 1:L, :]
        f_last = f_col[h][L - 1:L, :]
        wk = jnp.exp(f_last - f_col[h] + li_col[h] - m_new)
        decay.append(jnp.exp(f_last + m_prev[h] - m_new))
        kw.append(wk * k[h])
        m_ref[0, :, h:h + 1] = m_new

    wv = [jnp.dot(w[h], v[h], preferred_element_type=F32) for h in heads]
    kv = [lax.dot_general(kw[h], v[h], TN, preferred_element_type=F32) for h in heads]

    for h in heads:
        c_ref[0, h] = decay[h] * c_mat[h] + kv[h]
        n_ref[0, h:h + 1, :] = decay[h] * n_vec[h] + jnp.sum(kw[h], axis=0, keepdims=True)
        num = wv[h] + inter[h] * qc[h]
        den = (jnp.sum(w[h], axis=-1, keepdims=True)
               + inter[h] * jnp.sum(q[h] * n_vec[h], axis=-1, keepdims=True))
        hh = num / jnp.maximum(jnp.abs(den), jnp.exp(-m_t[h]))
        hn = hh * lax.rsqrt(jnp.mean(hh * hh, axis=-1, keepdims=True) + NORM_EPS)
        h_ref[0, :, hsl[h]] = jax.nn.sigmoid(o_ref[0, :, hsl[h]]) * (hn * ng_ref[:, hsl[h]])


def _mlstm(cols, c0, n0, m0, gate_bias, norm_g, L):
    b, t, _ = cols.shape
    nc = t // L
    wb = A_W // A_W

    def col(idx):
        return pl.BlockSpec((1, L, A_W), lambda i, c: (i, c, idx * wb))

    state = lambda shape: pl.BlockSpec(shape, lambda i, c: (i,) + (0,) * (len(shape) - 1))
    return pl.pallas_call(
        functools.partial(_mlstm_kernel, L=L),
        grid=(b, nc),
        in_specs=[
            col(0), col(1), col(2), col(3),
            pl.BlockSpec((1, L, LANES), lambda i, c: (i, c, COL_AG // LANES)),
            state((1, A_HEADS, A_DK, A_DK)), state((1, A_HEADS, A_DK)), state((1, 1, A_HEADS)),
            pl.BlockSpec((1, LANES), lambda i, c: (0, 0)),
            pl.BlockSpec((1, A_W), lambda i, c: (0, 0)),
        ],
        out_specs=[
            pl.BlockSpec((1, L, A_W), lambda i, c: (i, c, 0)),
            state((1, A_HEADS, A_DK, A_DK)), state((1, A_HEADS, A_DK)), state((1, 1, A_HEADS)),
        ],
        out_shape=[
            jax.ShapeDtypeStruct((b, t, A_W), F32),
            jax.ShapeDtypeStruct((b, A_HEADS, A_DK, A_DK), F32),
            jax.ShapeDtypeStruct((b, A_HEADS, A_DK), F32),
            jax.ShapeDtypeStruct((b, 1, A_HEADS), F32),
        ],
        compiler_params=_cparams("parallel", "arbitrary"),
    )(cols, cols, cols, cols, cols, c0, n0, m0, gate_bias, norm_g)


def _swa_kernel(*refs, n_kv, first_valid_fn):
    q_ref = refs[0]
    k_refs = refs[1:1 + n_kv]
    v_refs = refs[1 + n_kv:1 + 2 * n_kv]
    bias_ref, sink_ref, o_ref = refs[1 + 2 * n_kv:]
    k = jnp.concatenate([r[0] for r in k_refs], axis=0) if n_kv > 1 else k_refs[0][0]
    v = jnp.concatenate([r[0] for r in v_refs], axis=0) if n_kv > 1 else v_refs[0][0]
    lq = q_ref.shape[1]
    lk = k.shape[0]
    first_valid = first_valid_fn(pl.program_id(1))
    valid = lax.broadcasted_iota(jnp.int32, (B_GROUP * lq, lk), 1) >= first_valid
    for n in range(B_KV):
        qn = jnp.concatenate([q_ref[0, :, pl.ds((n * B_GROUP + g) * B_HD, B_HD)] for g in range(B_GROUP)], axis=0)
        kn = k[:, n * B_HD:(n + 1) * B_HD]
        vn = v[:, n * B_HD:(n + 1) * B_HD]
        s = lax.dot_general(qn, kn, NT, preferred_element_type=F32) * (B_HD ** -0.5) + bias_ref[n]
        s = jnp.where(valid, s, -1e30)
        sk = sink_ref[n]
        mx = jnp.maximum(jnp.max(s, axis=-1, keepdims=True), sk)
        p = jnp.exp(s - mx)
        den = jnp.sum(p, axis=-1, keepdims=True) + jnp.exp(sk - mx)
        o = jnp.dot(p, vn, preferred_element_type=F32) / den
        for g in range(B_GROUP):
            o_ref[0, :, pl.ds((n * B_GROUP + g) * B_HD, B_HD)] = o[g * lq:(g + 1) * lq, :]


def _swa_prompt(cols, bias, sinks):
    b, t, _ = cols.shape
    nc = t // CHUNK
    n_kv = WIN_CHUNKS + 1
    kw = B_KV * B_HD

    def kv_spec(col0, back):
        return pl.BlockSpec((1, CHUNK, kw), lambda i, c: (i, jnp.maximum(c - back, 0), col0 // kw))

    backs = list(range(WIN_CHUNKS, -1, -1))
    return pl.pallas_call(
        functools.partial(_swa_kernel, n_kv=n_kv, first_valid_fn=lambda c: (WIN_CHUNKS - c) * CHUNK),
        grid=(b, nc),
        in_specs=(
            [pl.BlockSpec((1, CHUNK, B_W), lambda i, c: (i, c, COL_BQ // B_W))]
            + [kv_spec(COL_BK, bk) for bk in backs]
            + [kv_spec(COL_BV, bk) for bk in backs]
            + [pl.BlockSpec(bias.shape, lambda i, c: (0, 0, 0)),
               pl.BlockSpec(sinks.shape, lambda i, c: (0, 0, 0))]
        ),
        out_specs=pl.BlockSpec((1, CHUNK, B_W), lambda i, c: (i, c, 0)),
        out_shape=jax.ShapeDtypeStruct((b, t, B_W), F32),
        compiler_params=_cparams("parallel", "parallel"),
    )(cols, *([cols] * (2 * n_kv)), bias, sinks)


def _swa_sample(cols, kf, vf, bias, sinks):
    b, t, _ = cols.shape
    lk = kf.shape[1]
    kw = B_KV * B_HD
    return pl.pallas_call(
        functools.partial(_swa_kernel, n_kv=1, first_valid_fn=lambda c: 0),
        grid=(b, 1),
        in_specs=[
            pl.BlockSpec((1, t, B_W), lambda i, c: (i, 0, COL_BQ // B_W)),
            pl.BlockSpec((1, lk, kw), lambda i, c: (i, 0, 0)),
            pl.BlockSpec((1, lk, kw), lambda i, c: (i, 0, 0)),
            pl.BlockSpec(bias.shape, lambda i, c: (0, 0, 0)),
            pl.BlockSpec(sinks.shape, lambda i, c: (0, 0, 0)),
        ],
        out_specs=pl.BlockSpec((1, t, B_W), lambda i, c: (i, 0, 0)),
        out_shape=jax.ShapeDtypeStruct((b, t, B_W), F32),
        compiler_params=_cparams("parallel", "arbitrary"),
    )(cols, kf, vf, bias, sinks)


def _rwkv_kernel(c_ref, sh0_ref, s0_ref, mu_ref, w0_ref, ww2_ref, a0_ref, wa2_ref, wg2_ref,
                 kk_ref, ka_ref, rk_ref, gg_ref, gb_ref, y_ref, s_ref, sh_ref, *, L):
    @pl.when(pl.program_id(1) == 0)
    def _():
        s_ref[...] = s0_ref[...]
        sh_ref[...] = sh0_ref[...]

    cc = c_ref[0][:, :C_COLS]
    row = lax.broadcasted_iota(jnp.int32, (L, C_COLS), 0)
    prev = jnp.where(row == 0, sh_ref[0], pltpu.roll(cc, 1, axis=0))
    sh_ref[0] = cc[L - 1:L, :]
    xm = cc + (prev - cc) * mu_ref[...]
    r = xm[:, 0:C_W]
    k = xm[:, C_W:2 * C_W]
    v = xm[:, 2 * C_W:3 * C_W]
    o1 = 3 * C_W
    wl = xm[:, o1:o1 + C_DECAY_LORA]
    al = xm[:, o1 + C_DECAY_LORA:o1 + C_DECAY_LORA + C_AAA_LORA]
    gl = xm[:, o1 + C_DECAY_LORA + C_AAA_LORA:C_COLS]

    wlog = -_softplus(-(w0_ref[...] + jnp.dot(jnp.tanh(wl), ww2_ref[...], preferred_element_type=F32))) - 0.5
    lw = -jnp.exp(wlog)
    a = jax.nn.sigmoid(a0_ref[...] + jnp.dot(al, wa2_ref[...], preferred_element_type=F32))
    g = jnp.dot(jax.nn.sigmoid(gl), wg2_ref[...], preferred_element_type=F32)
    kkf = k * kk_ref[...]
    kx = k * (1.0 + (a - 1.0) * ka_ref[...])

    cum = jnp.dot(_tri(L).astype(F32), lw, precision=HI, preferred_element_type=F32)
    e_in = jnp.exp(cum)
    e_ex = jnp.exp(cum - lw)
    e_neg = jnp.exp(-cum)
    cum_last = cum[L - 1:L, :]
    e_last = jnp.exp(cum_last - cum)
    w_chunk = jnp.exp(cum_last)

    strict = _tri(L, strict=True)
    incl = _tri(L)
    eye = (lax.broadcasted_iota(jnp.int32, (L, L), 0) == lax.broadcasted_iota(jnp.int32, (L, L), 1)).astype(F32)
    n_sq = int(math.log2(L)) - 1

    heads = range(C_HEADS)
    hsl = [slice(h * C_HD, (h + 1) * C_HD) for h in heads]
    dot = functools.partial(jnp.dot, preferred_element_type=F32)
    dot_nt = lambda a_, b_: lax.dot_general(a_, b_, NT, preferred_element_type=F32)
    dot_tn = lambda a_, b_: lax.dot_general(a_, b_, TN, preferred_element_type=F32)

    r_h = [r[:, hs] for hs in hsl]
    v_h = [v[:, hs] for hs in hsl]
    kx_h = [kx[:, hs] for hs in hsl]
    s0 = [s_ref[0, h] for h in heads]
    b_h, lhs, rhs = [], [], []
    for h, hs in enumerate(hsl):
        kk = kkf[:, hs]
        kk = kk / jnp.maximum(jnp.sqrt(jnp.sum(kk * kk, axis=-1, keepdims=True)), 1e-12)
        b_h.append(kk * a[:, hs])
        lhs.append(jnp.concatenate([kk * e_ex[:, hs], r_h[h] * e_in[:, hs]], axis=0))
        rhs.append(jnp.concatenate([b_h[h] * e_neg[:, hs], kx_h[h] * e_neg[:, hs]], axis=0))

    g4 = [dot_nt(lhs[h], rhs[h]) for h in heads]
    ks = [dot_nt(lhs[h], s0[h]) for h in heads]
    tb = [jnp.where(strict, g4[h][:L, :L], 0.0) for h in heads]
    tk = [jnp.where(strict, g4[h][:L, L:], 0.0) for h in heads]
    qb = [jnp.where(incl, g4[h][L:, :L], 0.0) for h in heads]
    qk = [jnp.where(incl, g4[h][L:, L:], 0.0) for h in heads]

    x = [eye - tb[h] for h in heads]
    p = [dot(tb[h], tb[h]) for h in heads]
    tkv = [dot(tk[h], v_h[h]) for h in heads]
    for i in range(n_sq):
        x = [x[h] + dot(x[h], p[h]) for h in heads]
        if i + 1 < n_sq:
            p = [dot(p[h], p[h]) for h in heads]

    u = [dot(x[h], ks[h][:L] + tkv[h]) for h in heads]
    y = [ks[h][L:] + dot(qk[h], v_h[h]) - dot(qb[h], u[h]) for h in heads]
    for h, hs in enumerate(hsl):
        kl = kx_h[h] * e_last[:, hs]
        bl = b_h[h] * e_last[:, hs]
        s_ref[0, h] = s0[h] * w_chunk[:, hs] + dot_tn(v_h[h], kl) - dot_tn(u[h], bl)

    for h, hs in enumerate(hsl):
        mean = jnp.mean(y[h], axis=-1, keepdims=True)
        var = jnp.mean(jnp.square(y[h] - mean), axis=-1, keepdims=True)
        yn = (y[h] - mean) * lax.rsqrt(var + GN_EPS) * gg_ref[:, hs] + gb_ref[:, hs]
        bonus = jnp.sum(r_h[h] * kx_h[h] * rk_ref[:, hs], axis=-1, keepdims=True) * v_h[h]
        y_ref[0, :, pl.ds(h * C_HD, C_HD)] = (yn + bonus) * g[:, hs]


def _rwkv(cols, shift0, s0, lp, L):
    b, t, _ = cols.shape
    nc = t // L
    state = lambda shape: pl.BlockSpec(shape, lambda i, c: (i,) + (0,) * (len(shape) - 1))
    par = lambda arr: pl.BlockSpec(arr.shape, lambda i, c: (0, 0))
    params = [lp['c_mu'], lp['c_w0'], lp['c_w_w2'], lp['c_a0'], lp['c_w_a2'], lp['c_w_g2'],
              lp['c_k_k'], lp['c_k_a'], lp['c_r_k'], lp['c_gn_g'], lp['c_gn_b']]
    return pl.pallas_call(
        functools.partial(_rwkv_kernel, L=L),
        grid=(b, nc),
        in_specs=[pl.BlockSpec((1, L, COL_C_PAD), lambda i, c: (i, c, COL_C // COL_C_PAD)),
                  state((1, 1, C_COLS)), state((1, C_HEADS, C_HD, C_HD))] + [par(p) for p in params],
        out_specs=[pl.BlockSpec((1, L, C_W), lambda i, c: (i, c, 0)),
                   state((1, C_HEADS, C_HD, C_HD)), state((1, 1, C_COLS))],
        out_shape=[jax.ShapeDtypeStruct((b, t, C_W), F32),
                   jax.ShapeDtypeStruct((b, C_HEADS, C_HD, C_HD), F32),
                   jax.ShapeDtypeStruct((b, 1, C_COLS), F32)],
        compiler_params=_cparams("parallel", "arbitrary"),
    )(cols, shift0, s0, *params)


def _merge_kernel(x_ref, g0_ref, g1_ref, g2_ref, ha_ref, hb_ref, hc_ref, wa_ref, wb_ref, wc_ref, wo_ref,
                  ng_ref, o_ref):
    def branch(g_ref, h_ref, w_ref):
        return jax.nn.sigmoid(g_ref[...]) * jnp.dot(h_ref[...].astype(BF16), w_ref[...],
                                                    preferred_element_type=F32)

    merged = branch(g0_ref, ha_ref, wa_ref) + branch(g1_ref, hb_ref, wb_ref) + branch(g2_ref, hc_ref, wc_ref)
    mix = jnp.dot(merged.astype(BF16), wo_ref[...], preferred_element_type=F32)
    y = mix * lax.rsqrt(jnp.mean(mix * mix, axis=-1, keepdims=True) + NORM_EPS) * ng_ref[...]
    o_ref[...] = x_ref[...] + y


def _merge(x, cols, h_a, h_b, h_c, wa, wb, wc, wo, ng, tm):
    t, d = x.shape
    gate = lambda idx: pl.BlockSpec((tm, d), lambda i: (i, COL_G // d + idx))
    tok = lambda w: pl.BlockSpec((tm, w), lambda i: (i, 0))
    full = lambda arr: pl.BlockSpec(arr.shape, lambda i: (0, 0))
    return pl.pallas_call(
        _merge_kernel,
        grid=(t // tm,),
        in_specs=[tok(d), gate(0), gate(1), gate(2), tok(A_W), tok(B_W), tok(C_W),
                  full(wa), full(wb), full(wc), full(wo), full(ng)],
        out_specs=tok(d),
        out_shape=jax.ShapeDtypeStruct((t, d), F32),
        compiler_params=_cparams("parallel"),
    )(x, cols, cols, cols, h_a, h_b, h_c, wa, wb, wc, wo, ng)


def _ffn_kernel(x_ref, gpre_ref, wg_ref, wu_ref, wo_ref, gpost_ref, o_ref, xn_ref, acc_ref):
    j = pl.program_id(1)

    @pl.when(j == 0)
    def _():
        x = x_ref[...]
        y = x * lax.rsqrt(jnp.mean(x * x, axis=-1, keepdims=True) + NORM_EPS) * gpre_ref[...]
        xn_ref[...] = y.astype(BF16)
        acc_ref[...] = jnp.zeros_like(acc_ref)

    xn = xn_ref[...]
    gate = jnp.dot(xn, wg_ref[...], preferred_element_type=F32)
    up = jnp.dot(xn, wu_ref[...], preferred_element_type=F32)
    act = (gate * jax.nn.sigmoid(gate) * up).astype(BF16)
    acc_ref[...] += jnp.dot(act, wo_ref[...], preferred_element_type=F32)

    @pl.when(j == pl.num_programs(1) - 1)
    def _():
        f = acc_ref[...]
        y = f * lax.rsqrt(jnp.mean(f * f, axis=-1, keepdims=True) + NORM_EPS) * gpost_ref[...]
        o_ref[...] = x_ref[...] + y


def _ffn(x, gpre, w_in, w_out, gpost, tm, tf):
    t, d = x.shape
    nf = D_FF // tf
    return pl.pallas_call(
        _ffn_kernel,
        grid=(t // tm, nf),
        in_specs=[
            pl.BlockSpec((tm, d), lambda i, j: (i, 0)),
            pl.BlockSpec((1, d), lambda i, j: (0, 0)),
            pl.BlockSpec((d, tf), lambda i, j: (0, j)),
            pl.BlockSpec((d, tf), lambda i, j: (0, nf + j)),
            pl.BlockSpec((tf, d), lambda i, j: (j, 0)),
            pl.BlockSpec((1, d), lambda i, j: (0, 0)),
        ],
        out_specs=pl.BlockSpec((tm, d), lambda i, j: (i, 0)),
        out_shape=jax.ShapeDtypeStruct((t, d), F32),
        scratch_shapes=[pltpu.VMEM((tm, d), BF16), pltpu.VMEM((tm, d), F32)],
        compiler_params=_cparams("parallel", "arbitrary"),
    )(x, gpre, w_in, w_in, w_out, gpost)


def _rel_bucket(rel):
    half = NUM_BUCKETS // 2
    exact = half // 2
    n = jnp.abs(rel)
    far = exact + (jnp.log(jnp.maximum(n, 1).astype(F32) / exact)
                   / math.log(REL_MAX_DIST / exact) * (half - exact)).astype(jnp.int32)
    far = jnp.minimum(far, half - 1)
    return jnp.where(rel > 0, half, 0) + jnp.where(n < exact, n, far)


def _rel_bias(table, n_q, n_before):
    rel = (jnp.arange(n_before + n_q)[None, :] - n_before) - jnp.arange(n_q)[:, None]
    bias = jnp.transpose(table[_rel_bucket(rel)].astype(F32), (2, 0, 1))
    return bias.reshape(B_KV, B_GROUP * n_q, n_before + n_q)


def _reorder_w_in(w):
    a_main = 4 * A_W
    a_all = a_main + 2 * A_HEADS
    b_all = B_W + 2 * B_KV * B_HD
    wa, wag = w[..., :a_main], w[..., a_main:a_all]
    wb = w[..., a_all:a_all + b_all]
    wc = w[..., a_all + b_all:a_all + b_all + C_COLS]
    wg = w[..., a_all + b_all + C_COLS:]
    z = lambda n: jnp.zeros(w.shape[:-1] + (n,), w.dtype)
    out = jnp.concatenate([wa, wc, z(COL_BQ - COL_C - C_COLS), wb, wag, z(COL_G - COL_AG - 2 * A_HEADS), wg],
                          axis=-1)
    assert out.shape[-1] == N_COLS
    return out.astype(BF16)


def _layer(x, st, lp, bias, L, tm):
    b, t, d = x.shape
    swa_k, swa_v, mc, mn, mm, rs, rshift = st
    x2 = x.reshape(b * t, d)
    cols2 = _proj_in(x2, lp['norm_mix_pre'], lp['w_in'], tm, 512)
    cols = cols2.reshape(b, t, N_COLS)

    h_a, mc, mn, mm = _mlstm(cols, mc, mn, mm, lp['a_gate_bias'], lp['a_norm'], L)

    k_new = cols[:, :, COL_BK:COL_BK + B_KV * B_HD]
    v_new = cols[:, :, COL_BV:COL_BV + B_KV * B_HD]
    lq = bias.shape[1] // B_GROUP
    sinks = jnp.repeat(lp['b_sinks'][0], lq).reshape(B_KV, B_GROUP * lq, 1)
    if swa_k is None:
        h_b = _swa_prompt(cols, bias, sinks)
        keep = min(WINDOW, t)
        swa_k, swa_v = k_new[:, t - keep:], v_new[:, t - keep:]
    else:
        kf = jnp.concatenate([swa_k, k_new], axis=1)
        vf = jnp.concatenate([swa_v, v_new], axis=1)
        h_b = _swa_sample(cols, kf, vf, bias, sinks)
        swa_k, swa_v = kf[:, t:], vf[:, t:]

    h_c, rs, rshift = _rwkv(cols, rshift, rs, lp, L)

    x2 = _merge(x2, cols2, h_a.reshape(b * t, A_W), h_b.reshape(b * t, B_W), h_c.reshape(b * t, C_W),
                lp['w_branch_a'], lp['w_branch_b'], lp['w_branch_c'], lp['w_out'], lp['norm_mix_post'], tm // 2)
    x2 = _ffn(x2, lp['norm_ffn_pre'], lp['w_ffn_in'], lp['w_ffn_out'], lp['norm_ffn_post'], tm, 256)
    return x2.reshape(b, t, d), (swa_k, swa_v, mc, mn, mm, rs, rshift)


def _trunk(x, layer_states, layers, bias, L, tm):
    new = []
    for l in range(DEPTH):
        x, st = _layer(x, layer_states[l], layers[l], bias, L, tm)
        new.append(st)
    return x, [jnp.stack([s[i] for s in new]) for i in range(7)]


def kernel(x_prompt, x_sample, cache_swa_k, cache_swa_v, state_mlstm_c, state_mlstm_n, state_mlstm_m, state_rwkv_s, state_rwkv_shift, w_in, norm_mix_pre, norm_mix_post, norm_ffn_pre, norm_ffn_post, a_gate_bias, a_norm, rel_bias, b_sinks, c_mu, c_w0, c_w_w2, c_a0, c_w_a2, c_w_g2, c_k_k, c_k_a, c_r_k, c_gn_g, c_gn_b, w_branch_a, w_branch_b, w_branch_c, w_out, w_ffn_in, w_ffn_out):
    bp, tp, _ = x_prompt.shape
    bs, ts, _ = x_sample.shape
    kw = B_KV * B_HD
    w_in_p = _reorder_w_in(w_in)
    gate_bias = jnp.pad(a_gate_bias, ((0, 0), (0, LANES - 2 * A_HEADS)))
    row = lambda p, l: p[l][None, :]
    layers = []
    for l in range(DEPTH):
        layers.append({
            'w_in': w_in_p[l], 'norm_mix_pre': row(norm_mix_pre, l), 'norm_mix_post': row(norm_mix_post, l),
            'norm_ffn_pre': row(norm_ffn_pre, l), 'norm_ffn_post': row(norm_ffn_post, l),
            'a_gate_bias': row(gate_bias, l), 'a_norm': row(a_norm, l), 'b_sinks': row(b_sinks, l),
            'c_mu': row(c_mu, l), 'c_w0': row(c_w0, l), 'c_w_w2': c_w_w2[l], 'c_a0': row(c_a0, l),
            'c_w_a2': c_w_a2[l], 'c_w_g2': c_w_g2[l], 'c_k_k': row(c_k_k, l), 'c_k_a': row(c_k_a, l),
            'c_r_k': row(c_r_k, l), 'c_gn_g': row(c_gn_g, l), 'c_gn_b': row(c_gn_b, l),
            'w_branch_a': w_branch_a[l].astype(BF16), 'w_branch_b': w_branch_b[l].astype(BF16),
            'w_branch_c': w_branch_c[l].astype(BF16), 'w_out': w_out[l].astype(BF16),
            'w_ffn_in': w_ffn_in[l].astype(BF16), 'w_ffn_out': w_ffn_out[l].astype(BF16),
        })

    fresh = (None, None,
             jnp.zeros((bp, A_HEADS, A_DK, A_DK), F32), jnp.zeros((bp, A_HEADS, A_DK), F32),
             jnp.zeros((bp, 1, A_HEADS), F32), jnp.zeros((bp, C_HEADS, C_HD, C_HD), F32),
             jnp.zeros((bp, 1, C_COLS), F32))
    lp_chunk = min(CHUNK, tp)
    y_prompt, p_st = _trunk(x_prompt, [fresh] * DEPTH, layers,
                            _rel_bias(rel_bias, lp_chunk, WIN_CHUNKS * CHUNK), lp_chunk, min(1024, bp * tp))

    n_before = cache_swa_k.shape[2]
    carried = [(cache_swa_k[l].reshape(bs, n_before, kw), cache_swa_v[l].reshape(bs, n_before, kw),
                state_mlstm_c[l], state_mlstm_n[l], state_mlstm_m[l][:, None, :],
                state_rwkv_s[l], state_rwkv_shift[l]) for l in range(DEPTH)]
    ls_chunk = min(CHUNK, ts)
    y_sample, s_st = _trunk(x_sample, carried, layers, _rel_bias(rel_bias, ts, n_before), ls_chunk,
                            min(1024, bs * ts))

    def finish(st, b):
        swa_k, swa_v, mc, mn, mm, rs, rshift = st
        n_rows = swa_k.shape[2]
        return (swa_k.reshape(DEPTH, b, n_rows, B_KV, B_HD), swa_v.reshape(DEPTH, b, n_rows, B_KV, B_HD),
                mc, mn, mm.reshape(DEPTH, b, A_HEADS), rs, rshift)

    return (y_prompt, y_sample) + finish(p_st, bp) + finish(s_st, bs)
```

```python
import functools
import math

import jax
import jax.numpy as jnp
from jax import lax
from jax.experimental import pallas as pl
from jax.experimental.pallas import tpu as pltpu

F32 = jnp.float32
BF16 = jnp.bfloat16

D_MODEL = 1024
DEPTH = 4
CHUNK = 64
NORM_EPS = 1e-6

A_HEADS = 4
A_DK = 128
A_W = 512
B_HEADS = 8
B_KV = 2
B_GROUP = 4
B_HD = 64
B_W = 512
WINDOW = 128
WIN_CHUNKS = 2
NUM_BUCKETS = 32
REL_MAX_DIST = 256
C_HEADS = 8
C_HD = 64
C_W = 512
C_DECAY_LORA = 64
C_AAA_LORA = 64
C_GATE_LORA = 128
C_COLS = 1792
GN_EPS = 64e-5
D_FF = 2816

COL_A = 0
COL_C = 2048
COL_C_PAD = 2048
COL_BQ = 4096
COL_BK = 4608
COL_BV = 4736
COL_AG = 4864
COL_G = 5120
N_COLS = 8192

LANES = 128
VMEM_LIMIT = 56 * 1024 * 1024
HI = lax.Precision.HIGHEST

NT = (((1,), (1,)), ((), ()))
TN = (((0,), (0,)), ((), ()))


MLSTM_ROWS_PER_STEP = 4
SWA_ROWS_PER_STEP = 4
RWKV_ROWS_PER_STEP = 2


def _batch_block(b, want):
    return max(d for d in range(1, want + 1) if b % d == 0)


def _cparams(*sem):
    return pltpu.CompilerParams(dimension_semantics=sem, vmem_limit_bytes=VMEM_LIMIT)


def _tri(n, strict=False):
    r = lax.broadcasted_iota(jnp.int32, (n, n), 0)
    c = lax.broadcasted_iota(jnp.int32, (n, n), 1)
    return (c < r) if strict else (c <= r)


def _softplus(z):
    return jnp.maximum(z, 0.0) + jnp.log1p(jnp.exp(-jnp.abs(z)))


def _proj_in_kernel(x_ref, g_ref, w_ref, o_ref, xn_ref):
    @pl.when(pl.program_id(1) == 0)
    def _():
        x = x_ref[...]
        y = x * lax.rsqrt(jnp.mean(x * x, axis=-1, keepdims=True) + NORM_EPS) * g_ref[...]
        xn_ref[...] = y.astype(BF16)

    o_ref[...] = jnp.dot(xn_ref[...], w_ref[...], preferred_element_type=F32)


def _proj_in(x, g, w, tm, tn):
    t, d = x.shape
    n = w.shape[1]
    return pl.pallas_call(
        _proj_in_kernel,
        grid=(t // tm, n // tn),
        in_specs=[
            pl.BlockSpec((tm, d), lambda i, j: (i, 0)),
            pl.BlockSpec((1, d), lambda i, j: (0, 0)),
            pl.BlockSpec((d, tn), lambda i, j: (0, j)),
        ],
        out_specs=pl.BlockSpec((tm, tn), lambda i, j: (i, j)),
        out_shape=jax.ShapeDtypeStruct((t, n), F32),
        scratch_shapes=[pltpu.VMEM((tm, d), BF16)],
        compiler_params=_cparams("parallel", "arbitrary"),
    )(x, g, w)


def _mlstm_kernel(q_ref, k_ref, v_ref, o_ref, g_ref, c0_ref, n0_ref, m0_ref, gb_ref, ng_ref,
                  h_ref, c_ref, n_ref, m_ref, *, L):
    @pl.when(pl.program_id(1) == 0)
    def _():
        c_ref[...] = c0_ref[...]
        n_ref[...] = n0_ref[...]
        m_ref[...] = m0_ref[...]

    bb = q_ref.shape[0]
    lane = lax.broadcasted_iota(jnp.int32, (L, LANES), 1)
    sel = (lax.broadcasted_iota(jnp.int32, (8, LANES), 0)
           == lax.broadcasted_iota(jnp.int32, (8, LANES), 1)).astype(F32)
    tri_f = _tri(L).astype(F32)
    z, zrow = [], []
    for bi in range(bb):
        gates = g_ref[bi] + gb_ref[...]
        lf = jnp.minimum(gates, 0.0) - jnp.log1p(jnp.exp(-jnp.abs(gates)))
        cum = jnp.dot(tri_f, lf, precision=HI, preferred_element_type=F32)
        z.append(jnp.where(lane < A_HEADS, gates, cum))
        zrow.append(lax.dot_general(sel, z[bi], NT, precision=HI, preferred_element_type=F32))
    causal = _tri(L)
    hsl = [pl.ds(h * A_DK, A_DK) for h in range(A_HEADS)]
    units = [(bi, h) for bi in range(bb) for h in range(A_HEADS)]
    un = range(len(units))

    q = [q_ref[bi, :, hsl[h]] for bi, h in units]
    k = [k_ref[bi, :, hsl[h]] * (A_DK ** -0.5) for bi, h in units]
    v = [v_ref[bi, :, hsl[h]] for bi, h in units]
    c_mat = [c_ref[bi, h] for bi, h in units]
    n_vec = [n_ref[bi, h:h + 1, :] for bi, h in units]
    m_prev = [m_ref[bi, :, h:h + 1] for bi, h in units]
    f_col = [z[bi][:, A_HEADS + h:A_HEADS + h + 1] for bi, h in units]
    li_col = [z[bi][:, h:h + 1] for bi, h in units]

    s = [lax.dot_general(q[u], k[u], NT, preferred_element_type=F32) for u in un]
    qc = [jnp.dot(q[u], c_mat[u], preferred_element_type=F32) for u in un]

    dlog = [jnp.where(causal, f_col[u] - zrow[bi][A_HEADS + h:A_HEADS + h + 1, :] + zrow[bi][h:h + 1, :], -jnp.inf)
            for u, (bi, h) in enumerate(units)]
    dmax = [jnp.max(dlog[u], axis=-1, keepdims=True) for u in un]
    qn = [jnp.sum(q[u] * n_vec[u], axis=-1, keepdims=True) for u in un]
    m_inter = [f_col[u] + m_prev[u] for u in un]
    m_t = [jnp.maximum(m_inter[u], dmax[u]) for u in un]
    w = [jnp.exp(dlog[u] - m_t[u]) * s[u] for u in un]
    inter = [jnp.exp(m_inter[u] - m_t[u]) for u in un]
    m_new = [m_t[u][L - 1:L, :] for u in un]
    f_last = [f_col[u][L - 1:L, :] for u in un]
    kw = [jnp.exp(f_last[u] - f_col[u] + li_col[u] - m_new[u]) * k[u] for u in un]
    decay = [jnp.exp(f_last[u] + m_prev[u] - m_new[u]) for u in un]

    wv = [jnp.dot(w[u], v[u], preferred_element_type=F32) for u in un]
    kv = [lax.dot_general(kw[u], v[u], TN, preferred_element_type=F32) for u in un]

    wsum = [jnp.sum(w[u], axis=-1, keepdims=True) for u in un]
    ksum = [jnp.sum(kw[u], axis=0, keepdims=True) for u in un]
    floor = [jnp.exp(-m_t[u]) for u in un]
    den = [jnp.maximum(jnp.abs(wsum[u] + inter[u] * qn[u]), floor[u]) for u in un]
    hh = [(wv[u] + inter[u] * qc[u]) / den[u] for u in un]
    ms = [jnp.mean(hh[u] * hh[u], axis=-1, keepdims=True) for u in un]
    scale = [lax.rsqrt(ms[u] + NORM_EPS) for u in un]
    for u, (bi, h) in enumerate(units):
        h_ref[bi, :, hsl[h]] = jax.nn.sigmoid(o_ref[bi, :, hsl[h]]) * (hh[u] * scale[u] * ng_ref[:, hsl[h]])
    for u, (bi, h) in enumerate(units):
        c_ref[bi, h] = decay[u] * c_mat[u] + kv[u]
        n_ref[bi, h:h + 1, :] = decay[u] * n_vec[u] + ksum[u]
        m_ref[bi, :, h:h + 1] = m_new[u]


def _mlstm(cols, c0, n0, m0, gate_bias, norm_g, L):
    b, t, _ = cols.shape
    nc = t // L
    bb = _batch_block(b, MLSTM_ROWS_PER_STEP)

    def col(idx):
        return pl.BlockSpec((bb, L, A_W), lambda i, c: (i, c, COL_A // A_W + idx))

    state = lambda shape: pl.BlockSpec(shape, lambda i, c: (i,) + (0,) * (len(shape) - 1))
    return pl.pallas_call(
        functools.partial(_mlstm_kernel, L=L),
        grid=(b // bb, nc),
        in_specs=[
            col(0), col(1), col(2), col(3),
            pl.BlockSpec((bb, L, LANES), lambda i, c: (i, c, COL_AG // LANES)),
            state((bb, A_HEADS, A_DK, A_DK)), state((bb, A_HEADS, A_DK)), state((bb, 1, A_HEADS)),
            pl.BlockSpec((1, LANES), lambda i, c: (0, 0)),
            pl.BlockSpec((1, A_W), lambda i, c: (0, 0)),
        ],
        out_specs=[
            pl.BlockSpec((bb, L, A_W), lambda i, c: (i, c, 0)),
            state((bb, A_HEADS, A_DK, A_DK)), state((bb, A_HEADS, A_DK)), state((bb, 1, A_HEADS)),
        ],
        out_shape=[
            jax.ShapeDtypeStruct((b, t, A_W), F32),
            jax.ShapeDtypeStruct((b, A_HEADS, A_DK, A_DK), F32),
            jax.ShapeDtypeStruct((b, A_HEADS, A_DK), F32),
            jax.ShapeDtypeStruct((b, 1, A_HEADS), F32),
        ],
        compiler_params=_cparams("parallel", "arbitrary"),
    )(cols, cols, cols, cols, cols, c0, n0, m0, gate_bias, norm_g)


def _swa_kernel(*refs, n_kv, first_valid_fn):
    q_ref = refs[0]
    k_refs = refs[1:1 + n_kv]
    v_refs = refs[1 + n_kv:1 + 2 * n_kv]
    bias_ref, sink_ref, o_ref = refs[1 + 2 * n_kv:]
    bb, lq, _ = q_ref.shape
    cat = lambda rs, bi: jnp.concatenate([r[bi] for r in rs], axis=0) if n_kv > 1 else rs[0][bi]
    k = [cat(k_refs, bi) for bi in range(bb)]
    v = [cat(v_refs, bi) for bi in range(bb)]
    lk = k[0].shape[0]
    first_valid = first_valid_fn(pl.program_id(1))
    valid = lax.broadcasted_iota(jnp.int32, (B_GROUP * lq, lk), 1) >= first_valid
    units = [(bi, n) for bi in range(bb) for n in range(B_KV)]
    un = range(len(units))
    head = lambda n, g: pl.ds((n * B_GROUP + g) * B_HD, B_HD)

    qn = [jnp.concatenate([q_ref[bi, :, head(n, g)] for g in range(B_GROUP)], axis=0) for bi, n in units]
    s = [lax.dot_general(qn[u], k[bi][:, n * B_HD:(n + 1) * B_HD], NT, preferred_element_type=F32)
         for u, (bi, n) in enumerate(units)]
    sk = [sink_ref[n] for _, n in units]
    s = [jnp.where(valid, s[u] * (B_HD ** -0.5) + bias_ref[n], -1e30) for u, (bi, n) in enumerate(units)]
    mx = [jnp.max(s[u], axis=-1, keepdims=True) for u in un]
    mx = [jnp.maximum(mx[u], sk[u]) for u in un]
    p = [jnp.exp(s[u] - mx[u]) for u in un]
    pv = [jnp.dot(p[u], v[bi][:, n * B_HD:(n + 1) * B_HD], preferred_element_type=F32)
          for u, (bi, n) in enumerate(units)]
    psum = [jnp.sum(p[u], axis=-1, keepdims=True) for u in un]
    den = [psum[u] + jnp.exp(sk[u] - mx[u]) for u in un]
    o = [pv[u] / den[u] for u in un]
    for u, (bi, n) in enumerate(units):
        for g in range(B_GROUP):
            o_ref[bi, :, head(n, g)] = o[u][g * lq:(g + 1) * lq, :]


def _swa_prompt(cols, bias, sinks):
    b, t, _ = cols.shape
    nc = t // CHUNK
    n_kv = WIN_CHUNKS + 1
    kw = B_KV * B_HD
    bb = _batch_block(b, SWA_ROWS_PER_STEP)

    def kv_spec(col0, back):
        return pl.BlockSpec((bb, CHUNK, kw), lambda i, c: (i, jnp.maximum(c - back, 0), col0 // kw))

    backs = list(range(WIN_CHUNKS, -1, -1))
    return pl.pallas_call(
        functools.partial(_swa_kernel, n_kv=n_kv, first_valid_fn=lambda c: (WIN_CHUNKS - c) * CHUNK),
        grid=(b // bb, nc),
        in_specs=(
            [pl.BlockSpec((bb, CHUNK, B_W), lambda i, c: (i, c, COL_BQ // B_W))]
            + [kv_spec(COL_BK, bk) for bk in backs]
            + [kv_spec(COL_BV, bk) for bk in backs]
            + [pl.BlockSpec(bias.shape, lambda i, c: (0, 0, 0)),
               pl.BlockSpec(sinks.shape, lambda i, c: (0, 0, 0))]
        ),
        out_specs=pl.BlockSpec((bb, CHUNK, B_W), lambda i, c: (i, c, 0)),
        out_shape=jax.ShapeDtypeStruct((b, t, B_W), F32),
        compiler_params=_cparams("parallel", "parallel"),
    )(cols, *([cols] * (2 * n_kv)), bias, sinks)


def _swa_sample(cols, kf, vf, bias, sinks):
    b, t, _ = cols.shape
    lk = kf.shape[1]
    kw = B_KV * B_HD
    bb = _batch_block(b, SWA_ROWS_PER_STEP)
    return pl.pallas_call(
        functools.partial(_swa_kernel, n_kv=1, first_valid_fn=lambda c: 0),
        grid=(b // bb, 1),
        in_specs=[
            pl.BlockSpec((bb, t, B_W), lambda i, c: (i, 0, COL_BQ // B_W)),
            pl.BlockSpec((bb, lk, kw), lambda i, c: (i, 0, 0)),
            pl.BlockSpec((bb, lk, kw), lambda i, c: (i, 0, 0)),
            pl.BlockSpec(bias.shape, lambda i, c: (0, 0, 0)),
            pl.BlockSpec(sinks.shape, lambda i, c: (0, 0, 0)),
        ],
        out_specs=pl.BlockSpec((bb, t, B_W), lambda i, c: (i, 0, 0)),
        out_shape=jax.ShapeDtypeStruct((b, t, B_W), F32),
        compiler_params=_cparams("parallel", "arbitrary"),
    )(cols, kf, vf, bias, sinks)


def _rwkv_kernel(c_ref, sh0_ref, s0_ref, mu_ref, w0_ref, ww2_ref, a0_ref, wa2_ref, wg2_ref,
                 kk_ref, ka_ref, rk_ref, gg_ref, gb_ref, y_ref, s_ref, sh_ref, *, L):
    @pl.when(pl.program_id(1) == 0)
    def _():
        s_ref[...] = s0_ref[...]
        sh_ref[...] = sh0_ref[...]

    bb = c_ref.shape[0]
    row = lax.broadcasted_iota(jnp.int32, (L, C_COLS), 0)
    tri_f = _tri(L).astype(F32)
    o1 = 3 * C_W
    r, v, a, g, kkf, kx, e_in, e_ex, e_neg, e_last, w_chunk = ([] for _ in range(11))
    for bi in range(bb):
        cc = c_ref[bi][:, :C_COLS]
        prev = jnp.where(row == 0, sh_ref[bi], pltpu.roll(cc, 1, axis=0))
        sh_ref[bi] = cc[L - 1:L, :]
        xm = cc + (prev - cc) * mu_ref[...]
        k = xm[:, C_W:2 * C_W]
        wl = xm[:, o1:o1 + C_DECAY_LORA]
        al = xm[:, o1 + C_DECAY_LORA:o1 + C_DECAY_LORA + C_AAA_LORA]
        gl = xm[:, o1 + C_DECAY_LORA + C_AAA_LORA:C_COLS]
        wlog = -_softplus(-(w0_ref[...] + jnp.dot(jnp.tanh(wl), ww2_ref[...], preferred_element_type=F32))) - 0.5
        lw = -jnp.exp(wlog)
        r.append(xm[:, 0:C_W])
        v.append(xm[:, 2 * C_W:3 * C_W])
        a.append(jax.nn.sigmoid(a0_ref[...] + jnp.dot(al, wa2_ref[...], preferred_element_type=F32)))
        g.append(jnp.dot(jax.nn.sigmoid(gl), wg2_ref[...], preferred_element_type=F32))
        kkf.append(k * kk_ref[...])
        kx.append(k * (1.0 + (a[bi] - 1.0) * ka_ref[...]))
        cum = jnp.dot(tri_f, lw, precision=HI, preferred_element_type=F32)
        cum_last = cum[L - 1:L, :]
        e_in.append(jnp.exp(cum))
        e_ex.append(jnp.exp(cum - lw))
        e_neg.append(jnp.exp(-cum))
        e_last.append(jnp.exp(cum_last - cum))
        w_chunk.append(jnp.exp(cum_last))

    strict = _tri(L, strict=True)
    incl = _tri(L)
    eye = (lax.broadcasted_iota(jnp.int32, (L, L), 0) == lax.broadcasted_iota(jnp.int32, (L, L), 1)).astype(F32)
    n_sq = int(math.log2(L)) - 1

    hsl = [slice(h * C_HD, (h + 1) * C_HD) for h in range(C_HEADS)]
    units = [(bi, h) for bi in range(bb) for h in range(C_HEADS)]
    un = range(len(units))
    dot = functools.partial(jnp.dot, preferred_element_type=F32)
    dot_nt = lambda a_, b_: lax.dot_general(a_, b_, NT, preferred_element_type=F32)
    dot_tn = lambda a_, b_: lax.dot_general(a_, b_, TN, preferred_element_type=F32)

    r_h = [r[bi][:, hsl[h]] for bi, h in units]
    v_h = [v[bi][:, hsl[h]] for bi, h in units]
    kx_h = [kx[bi][:, hsl[h]] for bi, h in units]
    s0 = [s_ref[bi, h] for bi, h in units]
    kk = [kkf[bi][:, hsl[h]] for bi, h in units]
    kk_norm = [jnp.maximum(jnp.sqrt(jnp.sum(kk[u] * kk[u], axis=-1, keepdims=True)), 1e-12) for u in un]
    kk = [kk[u] / kk_norm[u] for u in un]
    b_h = [kk[u] * a[bi][:, hsl[h]] for u, (bi, h) in enumerate(units)]
    lhs = [jnp.concatenate([kk[u] * e_ex[bi][:, hsl[h]], r_h[u] * e_in[bi][:, hsl[h]]], axis=0)
           for u, (bi, h) in enumerate(units)]
    rhs = [jnp.concatenate([b_h[u] * e_neg[bi][:, hsl[h]], kx_h[u] * e_neg[bi][:, hsl[h]]], axis=0)
           for u, (bi, h) in enumerate(units)]

    g4 = [dot_nt(lhs[u], rhs[u]) for u in un]
    ks = [dot_nt(lhs[u], s0[u]) for u in un]
    tb = [jnp.where(strict, g4[u][:L, :L], 0.0) for u in un]
    tk = [jnp.where(strict, g4[u][:L, L:], 0.0) for u in un]
    qb = [jnp.where(incl, g4[u][L:, :L], 0.0) for u in un]
    qk = [jnp.where(incl, g4[u][L:, L:], 0.0) for u in un]

    x = [eye - tb[u] for u in un]
    p = [dot(tb[u], tb[u]) for u in un]
    tkv = [dot(tk[u], v_h[u]) for u in un]
    for i in range(n_sq):
        x = [x[u] + dot(x[u], p[u]) for u in un]
        if i + 1 < n_sq:
            p = [dot(p[u], p[u]) for u in un]

    uu = [dot(x[u], ks[u][:L] + tkv[u]) for u in un]
    y = [ks[u][L:] + dot(qk[u], v_h[u]) - dot(qb[u], uu[u]) for u in un]
    kl = [kx_h[u] * e_last[bi][:, hsl[h]] for u, (bi, h) in enumerate(units)]
    bl = [b_h[u] * e_last[bi][:, hsl[h]] for u, (bi, h) in enumerate(units)]
    vk = [dot_tn(v_h[u], kl[u]) for u in un]
    ub = [dot_tn(uu[u], bl[u]) for u in un]
    for u, (bi, h) in enumerate(units):
        s_ref[bi, h] = s0[u] * w_chunk[bi][:, hsl[h]] + vk[u] - ub[u]

    mean = [jnp.mean(y[u], axis=-1, keepdims=True) for u in un]
    rk = [jnp.sum(r_h[u] * kx_h[u] * rk_ref[:, hsl[h]], axis=-1, keepdims=True) for u, (bi, h) in enumerate(units)]
    yc = [y[u] - mean[u] for u in un]
    var = [jnp.mean(jnp.square(yc[u]), axis=-1, keepdims=True) for u in un]
    inv = [lax.rsqrt(var[u] + GN_EPS) for u in un]
    for u, (bi, h) in enumerate(units):
        hs = hsl[h]
        yn = yc[u] * inv[u] * gg_ref[:, hs] + gb_ref[:, hs]
        y_ref[bi, :, pl.ds(h * C_HD, C_HD)] = (yn + rk[u] * v_h[u]) * g[bi][:, hs]


def _rwkv(cols, shift0, s0, lp, L):
    b, t, _ = cols.shape
    nc = t // L
    state = lambda shape: pl.BlockSpec(shape, lambda i, c: (i,) + (0,) * (len(shape) - 1))
    par = lambda arr: pl.BlockSpec(arr.shape, lambda i, c: (0, 0))
    params = [lp['c_mu'], lp['c_w0'], lp['c_w_w2'], lp['c_a0'], lp['c_w_a2'], lp['c_w_g2'],
              lp['c_k_k'], lp['c_k_a'], lp['c_r_k'], lp['c_gn_g'], lp['c_gn_b']]
    bb = _batch_block(b, RWKV_ROWS_PER_STEP)
    return pl.pallas_call(
        functools.partial(_rwkv_kernel, L=L),
        grid=(b // bb, nc),
        in_specs=[pl.BlockSpec((bb, L, COL_C_PAD), lambda i, c: (i, c, COL_C // COL_C_PAD)),
                  state((bb, 1, C_COLS)), state((bb, C_HEADS, C_HD, C_HD))] + [par(p) for p in params],
        out_specs=[pl.BlockSpec((bb, L, C_W), lambda i, c: (i, c, 0)),
                   state((bb, C_HEADS, C_HD, C_HD)), state((bb, 1, C_COLS))],
        out_shape=[jax.ShapeDtypeStruct((b, t, C_W), F32),
                   jax.ShapeDtypeStruct((b, C_HEADS, C_HD, C_HD), F32),
                   jax.ShapeDtypeStruct((b, 1, C_COLS), F32)],
        compiler_params=_cparams("parallel", "arbitrary"),
    )(cols, shift0, s0, *params)


def _merge_kernel(x_ref, g0_ref, g1_ref, g2_ref, ha_ref, hb_ref, hc_ref, wa_ref, wb_ref, wc_ref, wo_ref,
                  ng_ref, o_ref):
    def branch(g_ref, h_ref, w_ref):
        return jax.nn.sigmoid(g_ref[...]) * jnp.dot(h_ref[...].astype(BF16), w_ref[...],
                                                    preferred_element_type=F32)

    merged = branch(g0_ref, ha_ref, wa_ref) + branch(g1_ref, hb_ref, wb_ref) + branch(g2_ref, hc_ref, wc_ref)
    mix = jnp.dot(merged.astype(BF16), wo_ref[...], preferred_element_type=F32)
    y = mix * lax.rsqrt(jnp.mean(mix * mix, axis=-1, keepdims=True) + NORM_EPS) * ng_ref[...]
    o_ref[...] = x_ref[...] + y


def _merge(x, cols, h_a, h_b, h_c, wa, wb, wc, wo, ng, tm):
    t, d = x.shape
    gate = lambda idx: pl.BlockSpec((tm, d), lambda i: (i, COL_G // d + idx))
    tok = lambda w: pl.BlockSpec((tm, w), lambda i: (i, 0))
    full = lambda arr: pl.BlockSpec(arr.shape, lambda i: (0, 0))
    return pl.pallas_call(
        _merge_kernel,
        grid=(t // tm,),
        in_specs=[tok(d), gate(0), gate(1), gate(2), tok(A_W), tok(B_W), tok(C_W),
                  full(wa), full(wb), full(wc), full(wo), full(ng)],
        out_specs=tok(d),
        out_shape=jax.ShapeDtypeStruct((t, d), F32),
        compiler_params=_cparams("parallel"),
    )(x, cols, cols, cols, h_a, h_b, h_c, wa, wb, wc, wo, ng)


def _ffn_kernel(x_ref, gpre_ref, wg_ref, wu_ref, wo_ref, gpost_ref, o_ref, xn_ref, acc_ref):
    j = pl.program_id(1)

    @pl.when(j == 0)
    def _():
        x = x_ref[...]
        y = x * lax.rsqrt(jnp.mean(x * x, axis=-1, keepdims=True) + NORM_EPS) * gpre_ref[...]
        xn_ref[...] = y.astype(BF16)
        acc_ref[...] = jnp.zeros_like(acc_ref)

    xn = xn_ref[...]
    gate = jnp.dot(xn, wg_ref[...], preferred_element_type=F32)
    up = jnp.dot(xn, wu_ref[...], preferred_element_type=F32)
    act = (gate * jax.nn.sigmoid(gate) * up).astype(BF16)
    acc_ref[...] += jnp.dot(act, wo_ref[...], preferred_element_type=F32)

    @pl.when(j == pl.num_programs(1) - 1)
    def _():
        f = acc_ref[...]
        y = f * lax.rsqrt(jnp.mean(f * f, axis=-1, keepdims=True) + NORM_EPS) * gpost_ref[...]
        o_ref[...] = x_ref[...] + y


def _ffn(x, gpre, w_in, w_out, gpost, tm, tf):
    t, d = x.shape
    nf = D_FF // tf
    return pl.pallas_call(
        _ffn_kernel,
        grid=(t // tm, nf),
        in_specs=[
            pl.BlockSpec((tm, d), lambda i, j: (i, 0)),
            pl.BlockSpec((1, d), lambda i, j: (0, 0)),
            pl.BlockSpec((d, tf), lambda i, j: (0, j)),
            pl.BlockSpec((d, tf), lambda i, j: (0, nf + j)),
            pl.BlockSpec((tf, d), lambda i, j: (j, 0)),
            pl.BlockSpec((1, d), lambda i, j: (0, 0)),
        ],
        out_specs=pl.BlockSpec((tm, d), lambda i, j: (i, 0)),
        out_shape=jax.ShapeDtypeStruct((t, d), F32),
        scratch_shapes=[pltpu.VMEM((tm, d), BF16), pltpu.VMEM((tm, d), F32)],
        compiler_params=_cparams("parallel", "arbitrary"),
    )(x, gpre, w_in, w_in, w_out, gpost)


def _rel_bucket(rel):
    half = NUM_BUCKETS // 2
    exact = half // 2
    n = jnp.abs(rel)
    far = exact + (jnp.log(jnp.maximum(n, 1).astype(F32) / exact)
                   / math.log(REL_MAX_DIST / exact) * (half - exact)).astype(jnp.int32)
    far = jnp.minimum(far, half - 1)
    return jnp.where(rel > 0, half, 0) + jnp.where(n < exact, n, far)


def _rel_bias(table, n_q, n_before):
    rel = (jnp.arange(n_before + n_q)[None, :] - n_before) - jnp.arange(n_q)[:, None]
    bias = jnp.transpose(table[_rel_bucket(rel)].astype(F32), (2, 0, 1))
    return bias.reshape(B_KV, B_GROUP * n_q, n_before + n_q)


def _reorder_w_in(w):
    a_main = 4 * A_W
    a_all = a_main + 2 * A_HEADS
    b_all = B_W + 2 * B_KV * B_HD
    wa, wag = w[..., :a_main], w[..., a_main:a_all]
    wb = w[..., a_all:a_all + b_all]
    wc = w[..., a_all + b_all:a_all + b_all + C_COLS]
    wg = w[..., a_all + b_all + C_COLS:]
    z = lambda n: jnp.zeros(w.shape[:-1] + (n,), w.dtype)
    out = jnp.concatenate([wa, wc, z(COL_BQ - COL_C - C_COLS), wb, wag, z(COL_G - COL_AG - 2 * A_HEADS), wg],
                          axis=-1)
    assert out.shape[-1] == N_COLS
    return out.astype(BF16)


def _layer(x, st, lp, bias, L, tm):
    b, t, d = x.shape
    swa_k, swa_v, mc, mn, mm, rs, rshift = st
    x2 = x.reshape(b * t, d)
    cols2 = _proj_in(x2, lp['norm_mix_pre'], lp['w_in'], tm, 512)
    cols = cols2.reshape(b, t, N_COLS)

    h_a, mc, mn, mm = _mlstm(cols, mc, mn, mm, lp['a_gate_bias'], lp['a_norm'], L)

    k_new = cols[:, :, COL_BK:COL_BK + B_KV * B_HD]
    v_new = cols[:, :, COL_BV:COL_BV + B_KV * B_HD]
    lq = bias.shape[1] // B_GROUP
    sinks = jnp.repeat(lp['b_sinks'][0], lq).reshape(B_KV, B_GROUP * lq, 1)
    if swa_k is None:
        h_b = _swa_prompt(cols, bias, sinks)
        keep = min(WINDOW, t)
        swa_k, swa_v = k_new[:, t - keep:], v_new[:, t - keep:]
    else:
        kf = jnp.concatenate([swa_k, k_new], axis=1)
        vf = jnp.concatenate([swa_v, v_new], axis=1)
        h_b = _swa_sample(cols, kf, vf, bias, sinks)
        swa_k, swa_v = kf[:, t:], vf[:, t:]

    h_c, rs, rshift = _rwkv(cols, rshift, rs, lp, L)

    x2 = _merge(x2, cols2, h_a.reshape(b * t, A_W), h_b.reshape(b * t, B_W), h_c.reshape(b * t, C_W),
                lp['w_branch_a'], lp['w_branch_b'], lp['w_branch_c'], lp['w_out'], lp['norm_mix_post'], tm // 2)
    x2 = _ffn(x2, lp['norm_ffn_pre'], lp['w_ffn_in'], lp['w_ffn_out'], lp['norm_ffn_post'], tm, 256)
    return x2.reshape(b, t, d), (swa_k, swa_v, mc, mn, mm, rs, rshift)


def _trunk(x, layer_states, layers, bias, L, tm):
    new = []
    for l in range(DEPTH):
        x, st = _layer(x, layer_states[l], layers[l], bias, L, tm)
        new.append(st)
    return x, [jnp.stack([s[i] for s in new]) for i in range(7)]


def kernel(x_prompt, x_sample, cache_swa_k, cache_swa_v, state_mlstm_c, state_mlstm_n, state_mlstm_m, state_rwkv_s, state_rwkv_shift, w_in, norm_mix_pre, norm_mix_post, norm_ffn_pre, norm_ffn_post, a_gate_bias, a_norm, rel_bias, b_sinks, c_mu, c_w0, c_w_w2, c_a0, c_w_a2, c_w_g2, c_k_k, c_k_a, c_r_k, c_gn_g, c_gn_b, w_branch_a, w_branch_b, w_branch_c, w_out, w_ffn_in, w_ffn_out):
    bp, tp, _ = x_prompt.shape
    bs, ts, _ = x_sample.shape
    kw = B_KV * B_HD
    w_in_p = _reorder_w_in(w_in)
    gate_bias = jnp.pad(a_gate_bias, ((0, 0), (0, LANES - 2 * A_HEADS)))
    row = lambda p, l: p[l][None, :]
    layers = []
    for l in range(DEPTH):
        layers.append({
            'w_in': w_in_p[l], 'norm_mix_pre': row(norm_mix_pre, l), 'norm_mix_post': row(norm_mix_post, l),
            'norm_ffn_pre': row(norm_ffn_pre, l), 'norm_ffn_post': row(norm_ffn_post, l),
            'a_gate_bias': row(gate_bias, l), 'a_norm': row(a_norm, l), 'b_sinks': row(b_sinks, l),
            'c_mu': row(c_mu, l), 'c_w0': row(c_w0, l), 'c_w_w2': c_w_w2[l], 'c_a0': row(c_a0, l),
            'c_w_a2': c_w_a2[l], 'c_w_g2': c_w_g2[l], 'c_k_k': row(c_k_k, l), 'c_k_a': row(c_k_a, l),
            'c_r_k': row(c_r_k, l), 'c_gn_g': row(c_gn_g, l), 'c_gn_b': row(c_gn_b, l),
            'w_branch_a': w_branch_a[l].astype(BF16), 'w_branch_b': w_branch_b[l].astype(BF16),
            'w_branch_c': w_branch_c[l].astype(BF16), 'w_out': w_out[l].astype(BF16),
            'w_ffn_in': w_ffn_in[l].astype(BF16), 'w_ffn_out': w_ffn_out[l].astype(BF16),
        })

    fresh = (None, None,
             jnp.zeros((bp, A_HEADS, A_DK, A_DK), F32), jnp.zeros((bp, A_HEADS, A_DK), F32),
             jnp.zeros((bp, 1, A_HEADS), F32), jnp.zeros((bp, C_HEADS, C_HD, C_HD), F32),
             jnp.zeros((bp, 1, C_COLS), F32))
    lp_chunk = min(CHUNK, tp)
    y_prompt, p_st = _trunk(x_prompt, [fresh] * DEPTH, layers,
                            _rel_bias(rel_bias, lp_chunk, WIN_CHUNKS * CHUNK), lp_chunk, min(1024, bp * tp))

    n_before = cache_swa_k.shape[2]
    carried = [(cache_swa_k[l].reshape(bs, n_before, kw), cache_swa_v[l].reshape(bs, n_before, kw),
                state_mlstm_c[l], state_mlstm_n[l], state_mlstm_m[l][:, None, :],
                state_rwkv_s[l], state_rwkv_shift[l]) for l in range(DEPTH)]
    ls_chunk = min(CHUNK, ts)
    y_sample, s_st = _trunk(x_sample, carried, layers, _rel_bias(rel_bias, ts, n_before), ls_chunk,
                            min(1024, bs * ts))

    def finish(st, b):
        swa_k, swa_v, mc, mn, mm, rs, rshift = st
        n_rows = swa_k.shape[2]
        return (swa_k.reshape(DEPTH, b, n_rows, B_KV, B_HD), swa_v.reshape(DEPTH, b, n_rows, B_KV, B_HD),
                mc, mn, mm.reshape(DEPTH, b, A_HEADS), rs, rshift)

    return (y_prompt, y_sample) + finish(p_st, bp) + finish(s_st, bs)
```

```python
import functools
import math

import jax
import jax.numpy as jnp
from jax import lax
from jax.experimental import pallas as pl
from jax.experimental.pallas import tpu as pltpu

F32 = jnp.float32
BF16 = jnp.bfloat16

D_MODEL = 1024
DEPTH = 4
CHUNK = 64
NORM_EPS = 1e-6

A_HEADS = 4
A_DK = 128
A_W = 512
B_HEADS = 8
B_KV = 2
B_GROUP = 4
B_HD = 64
B_W = 512
WINDOW = 128
WIN_CHUNKS = 2
NUM_BUCKETS = 32
REL_MAX_DIST = 256
C_HEADS = 8
C_HD = 64
C_W = 512
C_DECAY_LORA = 64
C_AAA_LORA = 64
C_GATE_LORA = 128
C_COLS = 1792
GN_EPS = 64e-5
D_FF = 2816

COL_A = 0
COL_C = 2048
COL_C_PAD = 2048
COL_BQ = 4096
COL_BK = 4608
COL_BV = 4736
COL_AG = 4864
N_COLS = 5120
PROJ_IN_TN = 2560

LANES = 128
VMEM_LIMIT = 56 * 1024 * 1024
HI = lax.Precision.HIGHEST

NT = (((1,), (1,)), ((), ()))
TN = (((0,), (0,)), ((), ()))


MLSTM_ROWS_PER_STEP = 4
SWA_ROWS_PER_STEP = 4
RWKV_ROWS_PER_STEP = 2


def _batch_block(b, want):
    return max(d for d in range(1, want + 1) if b % d == 0)


def _cparams(*sem):
    return pltpu.CompilerParams(dimension_semantics=sem, vmem_limit_bytes=VMEM_LIMIT)


def _tri(n, strict=False):
    r = lax.broadcasted_iota(jnp.int32, (n, n), 0)
    c = lax.broadcasted_iota(jnp.int32, (n, n), 1)
    return (c < r) if strict else (c <= r)


def _softplus(z):
    return jnp.maximum(z, 0.0) + jnp.log1p(jnp.exp(-jnp.abs(z)))


def _proj_in_kernel(x_ref, g_ref, w_ref, o_ref, gate_ref, xn_ref, *, tn):
    j = pl.program_id(1)

    @pl.when(j == 0)
    def _():
        x = x_ref[...]
        y = x * lax.rsqrt(jnp.mean(x * x, axis=-1, keepdims=True) + NORM_EPS) * g_ref[...]
        xn_ref[...] = y.astype(BF16)

    acc = jnp.dot(xn_ref[...], w_ref[...], preferred_element_type=F32)
    o_ref[...] = acc.astype(BF16)

    @pl.when(j == COL_AG // tn)
    def _():
        gate_ref[...] = acc[:, COL_AG % tn:COL_AG % tn + LANES]


def _proj_in(x, g, w, tm, tn):
    t, d = x.shape
    n = w.shape[1]
    return pl.pallas_call(
        functools.partial(_proj_in_kernel, tn=tn),
        grid=(t // tm, n // tn),
        in_specs=[
            pl.BlockSpec((tm, d), lambda i, j: (i, 0)),
            pl.BlockSpec((1, d), lambda i, j: (0, 0)),
            pl.BlockSpec((d, tn), lambda i, j: (0, j)),
        ],
        out_specs=[pl.BlockSpec((tm, tn), lambda i, j: (i, j)),
                   pl.BlockSpec((tm, LANES), lambda i, j: (i, 0))],
        out_shape=[jax.ShapeDtypeStruct((t, n), BF16), jax.ShapeDtypeStruct((t, LANES), F32)],
        scratch_shapes=[pltpu.VMEM((tm, d), BF16)],
        compiler_params=_cparams("parallel", "arbitrary"),
    )(x, g, w)


def _mlstm_kernel(q_ref, k_ref, v_ref, o_ref, g_ref, c0_ref, n0_ref, m0_ref, gb_ref, ng_ref,
                  h_ref, c_ref, n_ref, m_ref, *, L):
    @pl.when(pl.program_id(1) == 0)
    def _():
        c_ref[...] = c0_ref[...]
        n_ref[...] = n0_ref[...]
        m_ref[...] = m0_ref[...]

    bb = q_ref.shape[0]
    lane = lax.broadcasted_iota(jnp.int32, (L, LANES), 1)
    sel = (lax.broadcasted_iota(jnp.int32, (8, LANES), 0)
           == lax.broadcasted_iota(jnp.int32, (8, LANES), 1)).astype(F32)
    tri_f = _tri(L).astype(F32)
    z, zrow = [], []
    for bi in range(bb):
        gates = g_ref[bi] + gb_ref[...]
        lf = jnp.minimum(gates, 0.0) - jnp.log1p(jnp.exp(-jnp.abs(gates)))
        cum = jnp.dot(tri_f, lf, precision=HI, preferred_element_type=F32)
        z.append(jnp.where(lane < A_HEADS, gates, cum))
        zrow.append(lax.dot_general(sel, z[bi], NT, precision=HI, preferred_element_type=F32))
    causal = _tri(L)
    hsl = [pl.ds(h * A_DK, A_DK) for h in range(A_HEADS)]
    units = [(bi, h) for bi in range(bb) for h in range(A_HEADS)]
    un = range(len(units))

    q = [q_ref[bi, :, hsl[h]].astype(F32) for bi, h in units]
    k = [k_ref[bi, :, hsl[h]].astype(F32) * (A_DK ** -0.5) for bi, h in units]
    v = [v_ref[bi, :, hsl[h]].astype(F32) for bi, h in units]
    c_mat = [c_ref[bi, h] for bi, h in units]
    n_vec = [n_ref[bi, h:h + 1, :] for bi, h in units]
    m_prev = [m_ref[bi, :, h:h + 1] for bi, h in units]
    f_col = [z[bi][:, A_HEADS + h:A_HEADS + h + 1] for bi, h in units]
    li_col = [z[bi][:, h:h + 1] for bi, h in units]

    s = [lax.dot_general(q[u], k[u], NT, preferred_element_type=F32) for u in un]
    qc = [jnp.dot(q[u], c_mat[u], preferred_element_type=F32) for u in un]

    dlog = [jnp.where(causal, f_col[u] - zrow[bi][A_HEADS + h:A_HEADS + h + 1, :] + zrow[bi][h:h + 1, :], -jnp.inf)
            for u, (bi, h) in enumerate(units)]
    dmax = [jnp.max(dlog[u], axis=-1, keepdims=True) for u in un]
    qn = [jnp.sum(q[u] * n_vec[u], axis=-1, keepdims=True) for u in un]
    m_inter = [f_col[u] + m_prev[u] for u in un]
    m_t = [jnp.maximum(m_inter[u], dmax[u]) for u in un]
    w = [jnp.exp(dlog[u] - m_t[u]) * s[u] for u in un]
    inter = [jnp.exp(m_inter[u] - m_t[u]) for u in un]
    m_new = [m_t[u][L - 1:L, :] for u in un]
    f_last = [f_col[u][L - 1:L, :] for u in un]
    kw = [jnp.exp(f_last[u] - f_col[u] + li_col[u] - m_new[u]) * k[u] for u in un]
    decay = [jnp.exp(f_last[u] + m_prev[u] - m_new[u]) for u in un]

    wv = [jnp.dot(w[u], v[u], preferred_element_type=F32) for u in un]
    kv = [lax.dot_general(kw[u], v[u], TN, preferred_element_type=F32) for u in un]

    wsum = [jnp.sum(w[u], axis=-1, keepdims=True) for u in un]
    ksum = [jnp.sum(kw[u], axis=0, keepdims=True) for u in un]
    floor = [jnp.exp(-m_t[u]) for u in un]
    den = [jnp.maximum(jnp.abs(wsum[u] + inter[u] * qn[u]), floor[u]) for u in un]
    hh = [(wv[u] + inter[u] * qc[u]) / den[u] for u in un]
    ms = [jnp.mean(hh[u] * hh[u], axis=-1, keepdims=True) for u in un]
    scale = [lax.rsqrt(ms[u] + NORM_EPS) for u in un]
    for u, (bi, h) in enumerate(units):
        gate = jax.nn.sigmoid(o_ref[bi, :, hsl[h]].astype(F32))
        h_ref[bi, :, hsl[h]] = (gate * (hh[u] * scale[u] * ng_ref[:, hsl[h]])).astype(h_ref.dtype)
    for u, (bi, h) in enumerate(units):
        c_ref[bi, h] = decay[u] * c_mat[u] + kv[u]
        n_ref[bi, h:h + 1, :] = decay[u] * n_vec[u] + ksum[u]
        m_ref[bi, :, h:h + 1] = m_new[u]


def _mlstm(cols, gates, c0, n0, m0, gate_bias, norm_g, L):
    b, t, _ = cols.shape
    nc = t // L
    bb = _batch_block(b, MLSTM_ROWS_PER_STEP)

    def col(idx):
        return pl.BlockSpec((bb, L, A_W), lambda i, c: (i, c, COL_A // A_W + idx))

    state = lambda shape: pl.BlockSpec(shape, lambda i, c: (i,) + (0,) * (len(shape) - 1))
    return pl.pallas_call(
        functools.partial(_mlstm_kernel, L=L),
        grid=(b // bb, nc),
        in_specs=[
            col(0), col(1), col(2), col(3),
            pl.BlockSpec((bb, L, LANES), lambda i, c: (i, c, 0)),
            state((bb, A_HEADS, A_DK, A_DK)), state((bb, A_HEADS, A_DK)), state((bb, 1, A_HEADS)),
            pl.BlockSpec((1, LANES), lambda i, c: (0, 0)),
            pl.BlockSpec((1, A_W), lambda i, c: (0, 0)),
        ],
        out_specs=[
            pl.BlockSpec((bb, L, A_W), lambda i, c: (i, c, 0)),
            state((bb, A_HEADS, A_DK, A_DK)), state((bb, A_HEADS, A_DK)), state((bb, 1, A_HEADS)),
        ],
        out_shape=[
            jax.ShapeDtypeStruct((b, t, A_W), BF16),
            jax.ShapeDtypeStruct((b, A_HEADS, A_DK, A_DK), F32),
            jax.ShapeDtypeStruct((b, A_HEADS, A_DK), F32),
            jax.ShapeDtypeStruct((b, 1, A_HEADS), F32),
        ],
        compiler_params=_cparams("parallel", "arbitrary"),
    )(cols, cols, cols, cols, gates, c0, n0, m0, gate_bias, norm_g)


def _swa_kernel(*refs, n_kv, first_valid_fn):
    q_ref = refs[0]
    k_refs = refs[1:1 + n_kv]
    v_refs = refs[1 + n_kv:1 + 2 * n_kv]
    bias_ref, sink_ref, o_ref = refs[1 + 2 * n_kv:]
    bb, lq, _ = q_ref.shape
    cat = lambda rs, bi: jnp.concatenate([r[bi] for r in rs], axis=0) if n_kv > 1 else rs[0][bi]
    k = [cat(k_refs, bi) for bi in range(bb)]
    v = [cat(v_refs, bi) for bi in range(bb)]
    lk = k[0].shape[0]
    first_valid = first_valid_fn(pl.program_id(1))
    valid = lax.broadcasted_iota(jnp.int32, (B_GROUP * lq, lk), 1) >= first_valid
    units = [(bi, n) for bi in range(bb) for n in range(B_KV)]
    un = range(len(units))
    head = lambda n, g: pl.ds((n * B_GROUP + g) * B_HD, B_HD)

    qn = [jnp.concatenate([q_ref[bi, :, head(n, g)] for g in range(B_GROUP)], axis=0) for bi, n in units]
    s = [lax.dot_general(qn[u], k[bi][:, n * B_HD:(n + 1) * B_HD], NT, preferred_element_type=F32)
         for u, (bi, n) in enumerate(units)]
    sk = [sink_ref[n] for _, n in units]
    s = [jnp.where(valid, s[u] * (B_HD ** -0.5) + bias_ref[n], -1e30) for u, (bi, n) in enumerate(units)]
    mx = [jnp.max(s[u], axis=-1, keepdims=True) for u in un]
    mx = [jnp.maximum(mx[u], sk[u]) for u in un]
    p = [jnp.exp(s[u] - mx[u]) for u in un]
    pv = [jnp.dot(p[u].astype(v[bi].dtype), v[bi][:, n * B_HD:(n + 1) * B_HD], preferred_element_type=F32)
          for u, (bi, n) in enumerate(units)]
    psum = [jnp.sum(p[u], axis=-1, keepdims=True) for u in un]
    den = [psum[u] + jnp.exp(sk[u] - mx[u]) for u in un]
    o = [pv[u] / den[u] for u in un]
    for u, (bi, n) in enumerate(units):
        for g in range(B_GROUP):
            o_ref[bi, :, head(n, g)] = o[u][g * lq:(g + 1) * lq, :].astype(o_ref.dtype)


def _swa_prompt(cols, bias, sinks):
    b, t, _ = cols.shape
    nc = t // CHUNK
    n_kv = WIN_CHUNKS + 1
    kw = B_KV * B_HD
    bb = _batch_block(b, SWA_ROWS_PER_STEP)

    def kv_spec(col0, back):
        return pl.BlockSpec((bb, CHUNK, kw), lambda i, c: (i, jnp.maximum(c - back, 0), col0 // kw))

    backs = list(range(WIN_CHUNKS, -1, -1))
    return pl.pallas_call(
        functools.partial(_swa_kernel, n_kv=n_kv, first_valid_fn=lambda c: (WIN_CHUNKS - c) * CHUNK),
        grid=(b // bb, nc),
        in_specs=(
            [pl.BlockSpec((bb, CHUNK, B_W), lambda i, c: (i, c, COL_BQ // B_W))]
            + [kv_spec(COL_BK, bk) for bk in backs]
            + [kv_spec(COL_BV, bk) for bk in backs]
            + [pl.BlockSpec(bias.shape, lambda i, c: (0, 0, 0)),
               pl.BlockSpec(sinks.shape, lambda i, c: (0, 0, 0))]
        ),
        out_specs=pl.BlockSpec((bb, CHUNK, B_W), lambda i, c: (i, c, 0)),
        out_shape=jax.ShapeDtypeStruct((b, t, B_W), BF16),
        compiler_params=_cparams("parallel", "parallel"),
    )(cols, *([cols] * (2 * n_kv)), bias, sinks)


def _swa_sample(cols, kf, vf, bias, sinks):
    b, t, _ = cols.shape
    lk = kf.shape[1]
    kw = B_KV * B_HD
    bb = _batch_block(b, SWA_ROWS_PER_STEP)
    return pl.pallas_call(
        functools.partial(_swa_kernel, n_kv=1, first_valid_fn=lambda c: 0),
        grid=(b // bb, 1),
        in_specs=[
            pl.BlockSpec((bb, t, B_W), lambda i, c: (i, 0, COL_BQ // B_W)),
            pl.BlockSpec((bb, lk, kw), lambda i, c: (i, 0, 0)),
            pl.BlockSpec((bb, lk, kw), lambda i, c: (i, 0, 0)),
            pl.BlockSpec(bias.shape, lambda i, c: (0, 0, 0)),
            pl.BlockSpec(sinks.shape, lambda i, c: (0, 0, 0)),
        ],
        out_specs=pl.BlockSpec((bb, t, B_W), lambda i, c: (i, 0, 0)),
        out_shape=jax.ShapeDtypeStruct((b, t, B_W), BF16),
        compiler_params=_cparams("parallel", "arbitrary"),
    )(cols, kf, vf, bias, sinks)


def _rwkv_kernel(c_ref, sh0_ref, s0_ref, mu_ref, w0_ref, ww2_ref, a0_ref, wa2_ref, wg2_ref,
                 kk_ref, ka_ref, rk_ref, gg_ref, gb_ref, y_ref, s_ref, sh_ref, *, L):
    @pl.when(pl.program_id(1) == 0)
    def _():
        s_ref[...] = s0_ref[...]
        sh_ref[...] = sh0_ref[...]

    bb = c_ref.shape[0]
    row = lax.broadcasted_iota(jnp.int32, (L, C_COLS), 0)
    tri_f = _tri(L).astype(F32)
    o1 = 3 * C_W
    r, v, a, g, kkf, kx, e_in, e_ex, e_neg, e_last, w_chunk = ([] for _ in range(11))
    for bi in range(bb):
        cc = c_ref[bi][:, :C_COLS].astype(F32)
        prev = jnp.where(row == 0, sh_ref[bi], pltpu.roll(cc, 1, axis=0))
        sh_ref[bi] = cc[L - 1:L, :]
        xm = cc + (prev - cc) * mu_ref[...]
        k = xm[:, C_W:2 * C_W]
        wl = xm[:, o1:o1 + C_DECAY_LORA]
        al = xm[:, o1 + C_DECAY_LORA:o1 + C_DECAY_LORA + C_AAA_LORA]
        gl = xm[:, o1 + C_DECAY_LORA + C_AAA_LORA:C_COLS]
        wlog = -_softplus(-(w0_ref[...] + jnp.dot(jnp.tanh(wl), ww2_ref[...], preferred_element_type=F32))) - 0.5
        lw = -jnp.exp(wlog)
        r.append(xm[:, 0:C_W])
        v.append(xm[:, 2 * C_W:3 * C_W])
        a.append(jax.nn.sigmoid(a0_ref[...] + jnp.dot(al, wa2_ref[...], preferred_element_type=F32)))
        g.append(jnp.dot(jax.nn.sigmoid(gl), wg2_ref[...], preferred_element_type=F32))
        kkf.append(k * kk_ref[...])
        kx.append(k * (1.0 + (a[bi] - 1.0) * ka_ref[...]))
        cum = jnp.dot(tri_f, lw, precision=HI, preferred_element_type=F32)
        cum_last = cum[L - 1:L, :]
        e_in.append(jnp.exp(cum))
        e_ex.append(jnp.exp(cum - lw))
        e_neg.append(jnp.exp(-cum))
        e_last.append(jnp.exp(cum_last - cum))
        w_chunk.append(jnp.exp(cum_last))

    strict = _tri(L, strict=True)
    incl = _tri(L)
    eye = (lax.broadcasted_iota(jnp.int32, (L, L), 0) == lax.broadcasted_iota(jnp.int32, (L, L), 1)).astype(F32)
    n_sq = int(math.log2(L)) - 1

    hsl = [slice(h * C_HD, (h + 1) * C_HD) for h in range(C_HEADS)]
    units = [(bi, h) for bi in range(bb) for h in range(C_HEADS)]
    un = range(len(units))
    dot = functools.partial(jnp.dot, preferred_element_type=F32)
    dot_nt = lambda a_, b_: lax.dot_general(a_, b_, NT, preferred_element_type=F32)
    dot_tn = lambda a_, b_: lax.dot_general(a_, b_, TN, preferred_element_type=F32)

    r_h = [r[bi][:, hsl[h]] for bi, h in units]
    v_h = [v[bi][:, hsl[h]] for bi, h in units]
    kx_h = [kx[bi][:, hsl[h]] for bi, h in units]
    s0 = [s_ref[bi, h] for bi, h in units]
    kk = [kkf[bi][:, hsl[h]] for bi, h in units]
    kk_norm = [jnp.maximum(jnp.sqrt(jnp.sum(kk[u] * kk[u], axis=-1, keepdims=True)), 1e-12) for u in un]
    kk = [kk[u] / kk_norm[u] for u in un]
    b_h = [kk[u] * a[bi][:, hsl[h]] for u, (bi, h) in enumerate(units)]
    lhs = [jnp.concatenate([kk[u] * e_ex[bi][:, hsl[h]], r_h[u] * e_in[bi][:, hsl[h]]], axis=0)
           for u, (bi, h) in enumerate(units)]
    rhs = [jnp.concatenate([b_h[u] * e_neg[bi][:, hsl[h]], kx_h[u] * e_neg[bi][:, hsl[h]]], axis=0)
           for u, (bi, h) in enumerate(units)]

    g4 = [dot_nt(lhs[u], rhs[u]) for u in un]
    ks = [dot_nt(lhs[u], s0[u]) for u in un]
    tb = [jnp.where(strict, g4[u][:L, :L], 0.0) for u in un]
    tk = [jnp.where(strict, g4[u][:L, L:], 0.0) for u in un]
    qb = [jnp.where(incl, g4[u][L:, :L], 0.0) for u in un]
    qk = [jnp.where(incl, g4[u][L:, L:], 0.0) for u in un]

    x = [eye - tb[u] for u in un]
    p = [dot(tb[u], tb[u]) for u in un]
    tkv = [dot(tk[u], v_h[u]) for u in un]
    for i in range(n_sq):
        x = [x[u] + dot(x[u], p[u]) for u in un]
        if i + 1 < n_sq:
            p = [dot(p[u], p[u]) for u in un]

    uu = [dot(x[u], ks[u][:L] + tkv[u]) for u in un]
    y = [ks[u][L:] + dot(qk[u], v_h[u]) - dot(qb[u], uu[u]) for u in un]
    kl = [kx_h[u] * e_last[bi][:, hsl[h]] for u, (bi, h) in enumerate(units)]
    bl = [b_h[u] * e_last[bi][:, hsl[h]] for u, (bi, h) in enumerate(units)]
    vk = [dot_tn(v_h[u], kl[u]) for u in un]
    ub = [dot_tn(uu[u], bl[u]) for u in un]
    for u, (bi, h) in enumerate(units):
        s_ref[bi, h] = s0[u] * w_chunk[bi][:, hsl[h]] + vk[u] - ub[u]

    mean = [jnp.mean(y[u], axis=-1, keepdims=True) for u in un]
    rk = [jnp.sum(r_h[u] * kx_h[u] * rk_ref[:, hsl[h]], axis=-1, keepdims=True) for u, (bi, h) in enumerate(units)]
    yc = [y[u] - mean[u] for u in un]
    var = [jnp.mean(jnp.square(yc[u]), axis=-1, keepdims=True) for u in un]
    inv = [lax.rsqrt(var[u] + GN_EPS) for u in un]
    for u, (bi, h) in enumerate(units):
        hs = hsl[h]
        yn = yc[u] * inv[u] * gg_ref[:, hs] + gb_ref[:, hs]
        y_ref[bi, :, pl.ds(h * C_HD, C_HD)] = ((yn + rk[u] * v_h[u]) * g[bi][:, hs]).astype(y_ref.dtype)


def _rwkv(cols, shift0, s0, lp, L):
    b, t, _ = cols.shape
    nc = t // L
    state = lambda shape: pl.BlockSpec(shape, lambda i, c: (i,) + (0,) * (len(shape) - 1))
    par = lambda arr: pl.BlockSpec(arr.shape, lambda i, c: (0, 0))
    params = [lp['c_mu'], lp['c_w0'], lp['c_w_w2'], lp['c_a0'], lp['c_w_a2'], lp['c_w_g2'],
              lp['c_k_k'], lp['c_k_a'], lp['c_r_k'], lp['c_gn_g'], lp['c_gn_b']]
    bb = _batch_block(b, RWKV_ROWS_PER_STEP)
    return pl.pallas_call(
        functools.partial(_rwkv_kernel, L=L),
        grid=(b // bb, nc),
        in_specs=[pl.BlockSpec((bb, L, COL_C_PAD), lambda i, c: (i, c, COL_C // COL_C_PAD)),
                  state((bb, 1, C_COLS)), state((bb, C_HEADS, C_HD, C_HD))] + [par(p) for p in params],
        out_specs=[pl.BlockSpec((bb, L, C_W), lambda i, c: (i, c, 0)),
                   state((bb, C_HEADS, C_HD, C_HD)), state((bb, 1, C_COLS))],
        out_shape=[jax.ShapeDtypeStruct((b, t, C_W), BF16),
                   jax.ShapeDtypeStruct((b, C_HEADS, C_HD, C_HD), F32),
                   jax.ShapeDtypeStruct((b, 1, C_COLS), F32)],
        compiler_params=_cparams("parallel", "arbitrary"),
    )(cols, shift0, s0, *params)


def _merge_kernel(x_ref, gpre_ref, wg_ref, ha_ref, hb_ref, hc_ref, wa_ref, wb_ref, wc_ref, wo_ref,
                  ng_ref, o_ref):
    d = x_ref.shape[1]
    x = x_ref[...]
    xn = (x * lax.rsqrt(jnp.mean(x * x, axis=-1, keepdims=True) + NORM_EPS) * gpre_ref[...]).astype(BF16)

    def branch(idx, h_ref, w_ref):
        gate = jax.nn.sigmoid(jnp.dot(xn, wg_ref[:, idx * d:(idx + 1) * d], preferred_element_type=F32))
        return gate * jnp.dot(h_ref[...].astype(BF16), w_ref[...], preferred_element_type=F32)

    merged = branch(0, ha_ref, wa_ref) + branch(1, hb_ref, wb_ref) + branch(2, hc_ref, wc_ref)
    mix = jnp.dot(merged.astype(BF16), wo_ref[...], preferred_element_type=F32)
    y = mix * lax.rsqrt(jnp.mean(mix * mix, axis=-1, keepdims=True) + NORM_EPS) * ng_ref[...]
    o_ref[...] = x + y


def _merge(x, gpre, wg, h_a, h_b, h_c, wa, wb, wc, wo, ng, tm):
    t, d = x.shape
    tok = lambda w: pl.BlockSpec((tm, w), lambda i: (i, 0))
    full = lambda arr: pl.BlockSpec(arr.shape, lambda i: (0, 0))
    return pl.pallas_call(
        _merge_kernel,
        grid=(t // tm,),
        in_specs=[tok(d), full(gpre), full(wg), tok(A_W), tok(B_W), tok(C_W),
                  full(wa), full(wb), full(wc), full(wo), full(ng)],
        out_specs=tok(d),
        out_shape=jax.ShapeDtypeStruct((t, d), F32),
        compiler_params=_cparams("parallel"),
    )(x, gpre, wg, h_a, h_b, h_c, wa, wb, wc, wo, ng)


def _ffn_kernel(x_ref, gpre_ref, wg_ref, wu_ref, wo_ref, gpost_ref, o_ref, xn_ref, acc_ref):
    j = pl.program_id(1)

    @pl.when(j == 0)
    def _():
        x = x_ref[...]
        y = x * lax.rsqrt(jnp.mean(x * x, axis=-1, keepdims=True) + NORM_EPS) * gpre_ref[...]
        xn_ref[...] = y.astype(BF16)
        acc_ref[...] = jnp.zeros_like(acc_ref)

    xn = xn_ref[...]
    gate = jnp.dot(xn, wg_ref[...], preferred_element_type=F32)
    up = jnp.dot(xn, wu_ref[...], preferred_element_type=F32)
    act = (gate * jax.nn.sigmoid(gate) * up).astype(BF16)
    acc_ref[...] += jnp.dot(act, wo_ref[...], preferred_element_type=F32)

    @pl.when(j == pl.num_programs(1) - 1)
    def _():
        f = acc_ref[...]
        y = f * lax.rsqrt(jnp.mean(f * f, axis=-1, keepdims=True) + NORM_EPS) * gpost_ref[...]
        o_ref[...] = x_ref[...] + y


def _ffn(x, gpre, w_in, w_out, gpost, tm, tf):
    t, d = x.shape
    nf = D_FF // tf
    return pl.pallas_call(
        _ffn_kernel,
        grid=(t // tm, nf),
        in_specs=[
            pl.BlockSpec((tm, d), lambda i, j: (i, 0)),
            pl.BlockSpec((1, d), lambda i, j: (0, 0)),
            pl.BlockSpec((d, tf), lambda i, j: (0, j)),
            pl.BlockSpec((d, tf), lambda i, j: (0, nf + j)),
            pl.BlockSpec((tf, d), lambda i, j: (j, 0)),
            pl.BlockSpec((1, d), lambda i, j: (0, 0)),
        ],
        out_specs=pl.BlockSpec((tm, d), lambda i, j: (i, 0)),
        out_shape=jax.ShapeDtypeStruct((t, d), F32),
        scratch_shapes=[pltpu.VMEM((tm, d), BF16), pltpu.VMEM((tm, d), F32)],
        compiler_params=_cparams("parallel", "arbitrary"),
    )(x, gpre, w_in, w_in, w_out, gpost)


def _rel_bucket(rel):
    half = NUM_BUCKETS // 2
    exact = half // 2
    n = jnp.abs(rel)
    far = exact + (jnp.log(jnp.maximum(n, 1).astype(F32) / exact)
                   / math.log(REL_MAX_DIST / exact) * (half - exact)).astype(jnp.int32)
    far = jnp.minimum(far, half - 1)
    return jnp.where(rel > 0, half, 0) + jnp.where(n < exact, n, far)


def _rel_bias(table, n_q, n_before):
    rel = (jnp.arange(n_before + n_q)[None, :] - n_before) - jnp.arange(n_q)[:, None]
    bias = jnp.transpose(table[_rel_bucket(rel)].astype(F32), (2, 0, 1))
    return bias.reshape(B_KV, B_GROUP * n_q, n_before + n_q)


def _reorder_w_in(w):
    a_main = 4 * A_W
    a_all = a_main + 2 * A_HEADS
    b_all = B_W + 2 * B_KV * B_HD
    wa, wag = w[..., :a_main], w[..., a_main:a_all]
    wb = w[..., a_all:a_all + b_all]
    wc = w[..., a_all + b_all:a_all + b_all + C_COLS]
    wg = w[..., a_all + b_all + C_COLS:]
    z = lambda n: jnp.zeros(w.shape[:-1] + (n,), w.dtype)
    out = jnp.concatenate([wa, wc, z(COL_BQ - COL_C - C_COLS), wb, wag, z(N_COLS - COL_AG - 2 * A_HEADS)], axis=-1)
    assert out.shape[-1] == N_COLS
    return out.astype(BF16), wg.astype(BF16)


def _layer(x, st, lp, bias, L, tm):
    b, t, d = x.shape
    swa_k, swa_v, mc, mn, mm, rs, rshift = st
    x2 = x.reshape(b * t, d)
    cols2, gates2 = _proj_in(x2, lp['norm_mix_pre'], lp['w_in'], tm, PROJ_IN_TN)
    cols = cols2.reshape(b, t, N_COLS)

    h_a, mc, mn, mm = _mlstm(cols, gates2.reshape(b, t, LANES), mc, mn, mm, lp['a_gate_bias'], lp['a_norm'], L)

    lq = bias.shape[1] // B_GROUP
    sinks = jnp.repeat(lp['b_sinks'][0], lq).reshape(B_KV, B_GROUP * lq, 1)
    keep = min(WINDOW, t)
    k_new = cols[:, t - keep:, COL_BK:COL_BK + B_KV * B_HD]
    v_new = cols[:, t - keep:, COL_BV:COL_BV + B_KV * B_HD]
    if swa_k is None:
        h_b = _swa_prompt(cols, bias, sinks)
        swa_k, swa_v = k_new.astype(F32), v_new.astype(F32)
    else:
        kf = jnp.concatenate([swa_k.astype(cols.dtype), k_new], axis=1)
        vf = jnp.concatenate([swa_v.astype(cols.dtype), v_new], axis=1)
        h_b = _swa_sample(cols, kf, vf, bias, sinks)
        swa_k = jnp.concatenate([swa_k[:, t:], k_new.astype(F32)], axis=1)
        swa_v = jnp.concatenate([swa_v[:, t:], v_new.astype(F32)], axis=1)

    h_c, rs, rshift = _rwkv(cols, rshift, rs, lp, L)

    x2 = _merge(x2, lp['norm_mix_pre'], lp['w_gate'], h_a.reshape(b * t, A_W), h_b.reshape(b * t, B_W),
                h_c.reshape(b * t, C_W), lp['w_branch_a'], lp['w_branch_b'], lp['w_branch_c'], lp['w_out'],
                lp['norm_mix_post'], tm // 2)
    x2 = _ffn(x2, lp['norm_ffn_pre'], lp['w_ffn_in'], lp['w_ffn_out'], lp['norm_ffn_post'], tm, 256)
    return x2.reshape(b, t, d), (swa_k, swa_v, mc, mn, mm, rs, rshift)


def _trunk(x, layer_states, layers, bias, L, tm):
    new = []
    for l in range(DEPTH):
        x, st = _layer(x, layer_states[l], layers[l], bias, L, tm)
        new.append(st)
    return x, [jnp.stack([s[i] for s in new]) for i in range(7)]


def kernel(x_prompt, x_sample, cache_swa_k, cache_swa_v, state_mlstm_c, state_mlstm_n, state_mlstm_m, state_rwkv_s, state_rwkv_shift, w_in, norm_mix_pre, norm_mix_post, norm_ffn_pre, norm_ffn_post, a_gate_bias, a_norm, rel_bias, b_sinks, c_mu, c_w0, c_w_w2, c_a0, c_w_a2, c_w_g2, c_k_k, c_k_a, c_r_k, c_gn_g, c_gn_b, w_branch_a, w_branch_b, w_branch_c, w_out, w_ffn_in, w_ffn_out):
    bp, tp, _ = x_prompt.shape
    bs, ts, _ = x_sample.shape
    kw = B_KV * B_HD
    w_in_p, w_gate_p = _reorder_w_in(w_in)
    gate_bias = jnp.pad(a_gate_bias, ((0, 0), (0, LANES - 2 * A_HEADS)))
    row = lambda p, l: p[l][None, :]
    layers = []
    for l in range(DEPTH):
        layers.append({
            'w_in': w_in_p[l], 'w_gate': w_gate_p[l], 'norm_mix_pre': row(norm_mix_pre, l), 'norm_mix_post': row(norm_mix_post, l),
            'norm_ffn_pre': row(norm_ffn_pre, l), 'norm_ffn_post': row(norm_ffn_post, l),
            'a_gate_bias': row(gate_bias, l), 'a_norm': row(a_norm, l), 'b_sinks': row(b_sinks, l),
            'c_mu': row(c_mu, l), 'c_w0': row(c_w0, l), 'c_w_w2': c_w_w2[l], 'c_a0': row(c_a0, l),
            'c_w_a2': c_w_a2[l], 'c_w_g2': c_w_g2[l], 'c_k_k': row(c_k_k, l), 'c_k_a': row(c_k_a, l),
            'c_r_k': row(c_r_k, l), 'c_gn_g': row(c_gn_g, l), 'c_gn_b': row(c_gn_b, l),
            'w_branch_a': w_branch_a[l].astype(BF16), 'w_branch_b': w_branch_b[l].astype(BF16),
            'w_branch_c': w_branch_c[l].astype(BF16), 'w_out': w_out[l].astype(BF16),
            'w_ffn_in': w_ffn_in[l].astype(BF16), 'w_ffn_out': w_ffn_out[l].astype(BF16),
        })

    fresh = (None, None,
             jnp.zeros((bp, A_HEADS, A_DK, A_DK), F32), jnp.zeros((bp, A_HEADS, A_DK), F32),
             jnp.zeros((bp, 1, A_HEADS), F32), jnp.zeros((bp, C_HEADS, C_HD, C_HD), F32),
             jnp.zeros((bp, 1, C_COLS), F32))
    lp_chunk = min(CHUNK, tp)
    y_prompt, p_st = _trunk(x_prompt, [fresh] * DEPTH, layers,
                            _rel_bias(rel_bias, lp_chunk, WIN_CHUNKS * CHUNK), lp_chunk, min(1024, bp * tp))

    n_before = cache_swa_k.shape[2]
    carried = [(cache_swa_k[l].reshape(bs, n_before, kw), cache_swa_v[l].reshape(bs, n_before, kw),
                state_mlstm_c[l], state_mlstm_n[l], state_mlstm_m[l][:, None, :],
                state_rwkv_s[l], state_rwkv_shift[l]) for l in range(DEPTH)]
    ls_chunk = min(CHUNK, ts)
    y_sample, s_st = _trunk(x_sample, carried, layers, _rel_bias(rel_bias, ts, n_before), ls_chunk,
                            min(1024, bs * ts))

    def finish(st, b):
        swa_k, swa_v, mc, mn, mm, rs, rshift = st
        n_rows = swa_k.shape[2]
        return (swa_k.reshape(DEPTH, b, n_rows, B_KV, B_HD), swa_v.reshape(DEPTH, b, n_rows, B_KV, B_HD),
                mc, mn, mm.reshape(DEPTH, b, A_HEADS), rs, rshift)

    return (y_prompt, y_sample) + finish(p_st, bp) + finish(s_st, bs)
```

```python
import functools
import math

import jax
import jax.numpy as jnp
from jax import lax
from jax.experimental import pallas as pl
from jax.experimental.pallas import tpu as pltpu

F32 = jnp.float32
BF16 = jnp.bfloat16

D_MODEL = 1024
DEPTH = 4
CHUNK = 64
NORM_EPS = 1e-6

A_HEADS = 4
A_DK = 128
A_W = 512
B_HEADS = 8
B_KV = 2
B_GROUP = 4
B_HD = 64
B_W = 512
WINDOW = 128
WIN_CHUNKS = 2
NUM_BUCKETS = 32
REL_MAX_DIST = 256
C_HEADS = 8
C_HD = 64
C_W = 512
C_DECAY_LORA = 64
C_AAA_LORA = 64
C_GATE_LORA = 128
C_COLS = 1792
GN_EPS = 64e-5
D_FF = 2816

COL_A = 0
COL_C = 2048
COL_C_PAD = 2048
COL_BQ = 4096
COL_BK = 4608
COL_BV = 4736
COL_AG = 4864
N_COLS = 5120
PROJ_IN_TN = 2560

LANES = 128
SIDE_COL0 = COL_BQ
SIDE_W = COL_AG + LANES - COL_BQ
S_BQ, S_BK, S_BV, S_AG = 0, COL_BK - COL_BQ, COL_BV - COL_BQ, COL_AG - COL_BQ
VMEM_LIMIT = 56 * 1024 * 1024
HI = lax.Precision.HIGHEST

NT = (((1,), (1,)), ((), ()))
TN = (((0,), (0,)), ((), ()))


MLSTM_ROWS_PER_STEP = 4
SWA_ROWS_PER_STEP = 4
RWKV_ROWS_PER_STEP = 2


def _batch_block(b, want):
    return max(d for d in range(1, want + 1) if b % d == 0)


def _cparams(*sem):
    return pltpu.CompilerParams(dimension_semantics=sem, vmem_limit_bytes=VMEM_LIMIT)


def _tri(n, strict=False):
    r = lax.broadcasted_iota(jnp.int32, (n, n), 0)
    c = lax.broadcasted_iota(jnp.int32, (n, n), 1)
    return (c < r) if strict else (c <= r)


def _softplus(z):
    return jnp.maximum(z, 0.0) + jnp.log1p(jnp.exp(-jnp.abs(z)))


def _proj_in_kernel(x_ref, g_ref, w_ref, o_ref, gate_ref, xn_ref, *, tn):
    j = pl.program_id(1)

    @pl.when(j == 0)
    def _():
        x = x_ref[...]
        y = x * lax.rsqrt(jnp.mean(x * x, axis=-1, keepdims=True) + NORM_EPS) * g_ref[...]
        xn_ref[...] = y.astype(BF16)

    acc = jnp.dot(xn_ref[...], w_ref[...], preferred_element_type=F32)
    o_ref[...] = acc.astype(BF16)

    @pl.when(j == SIDE_COL0 // tn)
    def _():
        gate_ref[...] = acc[:, SIDE_COL0 % tn:SIDE_COL0 % tn + SIDE_W]


def _proj_in(x, g, w, tm, tn):
    assert SIDE_COL0 // tn == (SIDE_COL0 + SIDE_W - 1) // tn
    t, d = x.shape
    n = w.shape[1]
    return pl.pallas_call(
        functools.partial(_proj_in_kernel, tn=tn),
        grid=(t // tm, n // tn),
        in_specs=[
            pl.BlockSpec((tm, d), lambda i, j: (i, 0)),
            pl.BlockSpec((1, d), lambda i, j: (0, 0)),
            pl.BlockSpec((d, tn), lambda i, j: (0, j)),
        ],
        out_specs=[pl.BlockSpec((tm, tn), lambda i, j: (i, j)),
                   pl.BlockSpec((tm, SIDE_W), lambda i, j: (i, 0))],
        out_shape=[jax.ShapeDtypeStruct((t, n), BF16), jax.ShapeDtypeStruct((t, SIDE_W), F32)],
        scratch_shapes=[pltpu.VMEM((tm, d), BF16)],
        compiler_params=_cparams("parallel", "arbitrary"),
    )(x, g, w)


def _mlstm_kernel(q_ref, k_ref, v_ref, o_ref, g_ref, c0_ref, n0_ref, m0_ref, gb_ref, ng_ref,
                  h_ref, c_ref, n_ref, m_ref, *, L):
    @pl.when(pl.program_id(1) == 0)
    def _():
        c_ref[...] = c0_ref[...]
        n_ref[...] = n0_ref[...]
        m_ref[...] = m0_ref[...]

    bb = q_ref.shape[0]
    lane = lax.broadcasted_iota(jnp.int32, (L, LANES), 1)
    sel = (lax.broadcasted_iota(jnp.int32, (8, LANES), 0)
           == lax.broadcasted_iota(jnp.int32, (8, LANES), 1)).astype(F32)
    tri_f = _tri(L).astype(F32)
    z, zrow = [], []
    for bi in range(bb):
        gates = g_ref[bi] + gb_ref[...]
        lf = jnp.minimum(gates, 0.0) - jnp.log1p(jnp.exp(-jnp.abs(gates)))
        cum = jnp.dot(tri_f, lf, precision=HI, preferred_element_type=F32)
        z.append(jnp.where(lane < A_HEADS, gates, cum))
        zrow.append(lax.dot_general(sel, z[bi], NT, precision=HI, preferred_element_type=F32))
    causal = _tri(L)
    hsl = [pl.ds(h * A_DK, A_DK) for h in range(A_HEADS)]
    units = [(bi, h) for bi in range(bb) for h in range(A_HEADS)]
    un = range(len(units))

    q = [q_ref[bi, :, hsl[h]].astype(F32) for bi, h in units]
    k = [k_ref[bi, :, hsl[h]].astype(F32) * (A_DK ** -0.5) for bi, h in units]
    v = [v_ref[bi, :, hsl[h]].astype(F32) for bi, h in units]
    c_mat = [c_ref[bi, h] for bi, h in units]
    n_vec = [n_ref[bi, h:h + 1, :] for bi, h in units]
    m_prev = [m_ref[bi, :, h:h + 1] for bi, h in units]
    f_col = [z[bi][:, A_HEADS + h:A_HEADS + h + 1] for bi, h in units]
    li_col = [z[bi][:, h:h + 1] for bi, h in units]

    s = [lax.dot_general(q[u], k[u], NT, preferred_element_type=F32) for u in un]
    qc = [jnp.dot(q[u], c_mat[u], preferred_element_type=F32) for u in un]

    dlog = [jnp.where(causal, f_col[u] - zrow[bi][A_HEADS + h:A_HEADS + h + 1, :] + zrow[bi][h:h + 1, :], -jnp.inf)
            for u, (bi, h) in enumerate(units)]
    dmax = [jnp.max(dlog[u], axis=-1, keepdims=True) for u in un]
    qn = [jnp.sum(q[u] * n_vec[u], axis=-1, keepdims=True) for u in un]
    m_inter = [f_col[u] + m_prev[u] for u in un]
    m_t = [jnp.maximum(m_inter[u], dmax[u]) for u in un]
    w = [jnp.exp(dlog[u] - m_t[u]) * s[u] for u in un]
    inter = [jnp.exp(m_inter[u] - m_t[u]) for u in un]
    m_new = [m_t[u][L - 1:L, :] for u in un]
    f_last = [f_col[u][L - 1:L, :] for u in un]
    kw = [jnp.exp(f_last[u] - f_col[u] + li_col[u] - m_new[u]) * k[u] for u in un]
    decay = [jnp.exp(f_last[u] + m_prev[u] - m_new[u]) for u in un]

    wv = [jnp.dot(w[u], v[u], preferred_element_type=F32) for u in un]
    kv = [lax.dot_general(kw[u], v[u], TN, preferred_element_type=F32) for u in un]

    wsum = [jnp.sum(w[u], axis=-1, keepdims=True) for u in un]
    ksum = [jnp.sum(kw[u], axis=0, keepdims=True) for u in un]
    floor = [jnp.exp(-m_t[u]) for u in un]
    den = [jnp.maximum(jnp.abs(wsum[u] + inter[u] * qn[u]), floor[u]) for u in un]
    hh = [(wv[u] + inter[u] * qc[u]) / den[u] for u in un]
    ms = [jnp.mean(hh[u] * hh[u], axis=-1, keepdims=True) for u in un]
    scale = [lax.rsqrt(ms[u] + NORM_EPS) for u in un]
    for u, (bi, h) in enumerate(units):
        gate = jax.nn.sigmoid(o_ref[bi, :, hsl[h]].astype(F32))
        h_ref[bi, :, hsl[h]] = (gate * (hh[u] * scale[u] * ng_ref[:, hsl[h]])).astype(h_ref.dtype)
    for u, (bi, h) in enumerate(units):
        c_ref[bi, h] = decay[u] * c_mat[u] + kv[u]
        n_ref[bi, h:h + 1, :] = decay[u] * n_vec[u] + ksum[u]
        m_ref[bi, :, h:h + 1] = m_new[u]


def _mlstm(cols, gates, c0, n0, m0, gate_bias, norm_g, L):
    b, t, _ = cols.shape
    nc = t // L
    bb = _batch_block(b, MLSTM_ROWS_PER_STEP)

    def col(idx):
        return pl.BlockSpec((bb, L, A_W), lambda i, c: (i, c, COL_A // A_W + idx))

    state = lambda shape: pl.BlockSpec(shape, lambda i, c: (i,) + (0,) * (len(shape) - 1))
    return pl.pallas_call(
        functools.partial(_mlstm_kernel, L=L),
        grid=(b // bb, nc),
        in_specs=[
            col(0), col(1), col(2), col(3),
            pl.BlockSpec((bb, L, LANES), lambda i, c: (i, c, S_AG // LANES)),
            state((bb, A_HEADS, A_DK, A_DK)), state((bb, A_HEADS, A_DK)), state((bb, 1, A_HEADS)),
            pl.BlockSpec((1, LANES), lambda i, c: (0, 0)),
            pl.BlockSpec((1, A_W), lambda i, c: (0, 0)),
        ],
        out_specs=[
            pl.BlockSpec((bb, L, A_W), lambda i, c: (i, c, 0)),
            state((bb, A_HEADS, A_DK, A_DK)), state((bb, A_HEADS, A_DK)), state((bb, 1, A_HEADS)),
        ],
        out_shape=[
            jax.ShapeDtypeStruct((b, t, A_W), BF16),
            jax.ShapeDtypeStruct((b, A_HEADS, A_DK, A_DK), F32),
            jax.ShapeDtypeStruct((b, A_HEADS, A_DK), F32),
            jax.ShapeDtypeStruct((b, 1, A_HEADS), F32),
        ],
        compiler_params=_cparams("parallel", "arbitrary"),
    )(cols, cols, cols, cols, gates, c0, n0, m0, gate_bias, norm_g)


def _swa_kernel(*refs, n_kv, first_valid_fn):
    q_ref = refs[0]
    k_refs = refs[1:1 + n_kv]
    v_refs = refs[1 + n_kv:1 + 2 * n_kv]
    bias_ref, sink_ref, o_ref = refs[1 + 2 * n_kv:]
    bb, lq, _ = q_ref.shape
    cat = lambda rs, bi: jnp.concatenate([r[bi].astype(F32) for r in rs], axis=0)
    k = [cat(k_refs, bi) for bi in range(bb)]
    v = [cat(v_refs, bi) for bi in range(bb)]
    lk = k[0].shape[0]
    first_valid = first_valid_fn(pl.program_id(1))
    valid = lax.broadcasted_iota(jnp.int32, (B_GROUP * lq, lk), 1) >= first_valid
    units = [(bi, n) for bi in range(bb) for n in range(B_KV)]
    un = range(len(units))
    head = lambda n, g: pl.ds((n * B_GROUP + g) * B_HD, B_HD)

    qn = [jnp.concatenate([q_ref[bi, :, head(n, g)].astype(F32) for g in range(B_GROUP)], axis=0)
          for bi, n in units]
    s = [lax.dot_general(qn[u], k[bi][:, n * B_HD:(n + 1) * B_HD], NT, preferred_element_type=F32)
         for u, (bi, n) in enumerate(units)]
    sk = [sink_ref[n] for _, n in units]
    s = [jnp.where(valid, s[u] * (B_HD ** -0.5) + bias_ref[n], -1e30) for u, (bi, n) in enumerate(units)]
    mx = [jnp.max(s[u], axis=-1, keepdims=True) for u in un]
    mx = [jnp.maximum(mx[u], sk[u]) for u in un]
    p = [jnp.exp(s[u] - mx[u]) for u in un]
    pv = [jnp.dot(p[u].astype(v[bi].dtype), v[bi][:, n * B_HD:(n + 1) * B_HD], preferred_element_type=F32)
          for u, (bi, n) in enumerate(units)]
    psum = [jnp.sum(p[u], axis=-1, keepdims=True) for u in un]
    den = [psum[u] + jnp.exp(sk[u] - mx[u]) for u in un]
    o = [pv[u] / den[u] for u in un]
    for u, (bi, n) in enumerate(units):
        for g in range(B_GROUP):
            o_ref[bi, :, head(n, g)] = o[u][g * lq:(g + 1) * lq, :].astype(o_ref.dtype)


def _swa_prompt(cols, bias, sinks):
    b, t, _ = cols.shape
    nc = t // CHUNK
    n_kv = WIN_CHUNKS + 1
    kw = B_KV * B_HD
    bb = _batch_block(b, SWA_ROWS_PER_STEP)

    def kv_spec(col0, back):
        return pl.BlockSpec((bb, CHUNK, kw), lambda i, c: (i, jnp.maximum(c - back, 0), col0 // kw))

    backs = list(range(WIN_CHUNKS, -1, -1))
    return pl.pallas_call(
        functools.partial(_swa_kernel, n_kv=n_kv, first_valid_fn=lambda c: (WIN_CHUNKS - c) * CHUNK),
        grid=(b // bb, nc),
        in_specs=(
            [pl.BlockSpec((bb, CHUNK, B_W), lambda i, c: (i, c, S_BQ // B_W))]
            + [kv_spec(S_BK, bk) for bk in backs]
            + [kv_spec(S_BV, bk) for bk in backs]
            + [pl.BlockSpec(bias.shape, lambda i, c: (0, 0, 0)),
               pl.BlockSpec(sinks.shape, lambda i, c: (0, 0, 0))]
        ),
        out_specs=pl.BlockSpec((bb, CHUNK, B_W), lambda i, c: (i, c, 0)),
        out_shape=jax.ShapeDtypeStruct((b, t, B_W), BF16),
        compiler_params=_cparams("parallel", "parallel"),
    )(cols, *([cols] * (2 * n_kv)), bias, sinks)


def _swa_sample(cols, kf, vf, bias, sinks):
    b, t, _ = cols.shape
    lk = kf.shape[1]
    kw = B_KV * B_HD
    bb = _batch_block(b, SWA_ROWS_PER_STEP)
    return pl.pallas_call(
        functools.partial(_swa_kernel, n_kv=1, first_valid_fn=lambda c: 0),
        grid=(b // bb, 1),
        in_specs=[
            pl.BlockSpec((bb, t, B_W), lambda i, c: (i, 0, S_BQ // B_W)),
            pl.BlockSpec((bb, lk, kw), lambda i, c: (i, 0, 0)),
            pl.BlockSpec((bb, lk, kw), lambda i, c: (i, 0, 0)),
            pl.BlockSpec(bias.shape, lambda i, c: (0, 0, 0)),
            pl.BlockSpec(sinks.shape, lambda i, c: (0, 0, 0)),
        ],
        out_specs=pl.BlockSpec((bb, t, B_W), lambda i, c: (i, 0, 0)),
        out_shape=jax.ShapeDtypeStruct((b, t, B_W), BF16),
        compiler_params=_cparams("parallel", "arbitrary"),
    )(cols, kf, vf, bias, sinks)


def _rwkv_kernel(c_ref, sh0_ref, s0_ref, mu_ref, w0_ref, ww2_ref, a0_ref, wa2_ref, wg2_ref,
                 kk_ref, ka_ref, rk_ref, gg_ref, gb_ref, y_ref, s_ref, sh_ref, *, L):
    @pl.when(pl.program_id(1) == 0)
    def _():
        s_ref[...] = s0_ref[...]
        sh_ref[...] = sh0_ref[...]

    bb = c_ref.shape[0]
    row = lax.broadcasted_iota(jnp.int32, (L, C_COLS), 0)
    tri_f = _tri(L).astype(F32)
    o1 = 3 * C_W
    r, v, a, g, kkf, kx, e_in, e_ex, e_neg, e_last, w_chunk = ([] for _ in range(11))
    for bi in range(bb):
        cc = c_ref[bi][:, :C_COLS].astype(F32)
        prev = jnp.where(row == 0, sh_ref[bi], pltpu.roll(cc, 1, axis=0))
        sh_ref[bi] = cc[L - 1:L, :]
        xm = cc + (prev - cc) * mu_ref[...]
        k = xm[:, C_W:2 * C_W]
        wl = xm[:, o1:o1 + C_DECAY_LORA]
        al = xm[:, o1 + C_DECAY_LORA:o1 + C_DECAY_LORA + C_AAA_LORA]
        gl = xm[:, o1 + C_DECAY_LORA + C_AAA_LORA:C_COLS]
        wlog = -_softplus(-(w0_ref[...] + jnp.dot(jnp.tanh(wl), ww2_ref[...], preferred_element_type=F32))) - 0.5
        lw = -jnp.exp(wlog)
        r.append(xm[:, 0:C_W])
        v.append(xm[:, 2 * C_W:3 * C_W])
        a.append(jax.nn.sigmoid(a0_ref[...] + jnp.dot(al, wa2_ref[...], preferred_element_type=F32)))
        g.append(jnp.dot(jax.nn.sigmoid(gl), wg2_ref[...], preferred_element_type=F32))
        kkf.append(k * kk_ref[...])
        kx.append(k * (1.0 + (a[bi] - 1.0) * ka_ref[...]))
        cum = jnp.dot(tri_f, lw, precision=HI, preferred_element_type=F32)
        cum_last = cum[L - 1:L, :]
        e_in.append(jnp.exp(cum))
        e_ex.append(jnp.exp(cum - lw))
        e_neg.append(jnp.exp(-cum))
        e_last.append(jnp.exp(cum_last - cum))
        w_chunk.append(jnp.exp(cum_last))

    strict = _tri(L, strict=True)
    incl = _tri(L)
    eye = (lax.broadcasted_iota(jnp.int32, (L, L), 0) == lax.broadcasted_iota(jnp.int32, (L, L), 1)).astype(F32)
    n_sq = int(math.log2(L)) - 1

    hsl = [slice(h * C_HD, (h + 1) * C_HD) for h in range(C_HEADS)]
    units = [(bi, h) for bi in range(bb) for h in range(C_HEADS)]
    un = range(len(units))
    dot = functools.partial(jnp.dot, preferred_element_type=F32)
    dot_nt = lambda a_, b_: lax.dot_general(a_, b_, NT, preferred_element_type=F32)
    dot_tn = lambda a_, b_: lax.dot_general(a_, b_, TN, preferred_element_type=F32)

    r_h = [r[bi][:, hsl[h]] for bi, h in units]
    v_h = [v[bi][:, hsl[h]] for bi, h in units]
    kx_h = [kx[bi][:, hsl[h]] for bi, h in units]
    s0 = [s_ref[bi, h] for bi, h in units]
    kk = [kkf[bi][:, hsl[h]] for bi, h in units]
    kk_norm = [jnp.maximum(jnp.sqrt(jnp.sum(kk[u] * kk[u], axis=-1, keepdims=True)), 1e-12) for u in un]
    kk = [kk[u] / kk_norm[u] for u in un]
    b_h = [kk[u] * a[bi][:, hsl[h]] for u, (bi, h) in enumerate(units)]
    lhs = [jnp.concatenate([kk[u] * e_ex[bi][:, hsl[h]], r_h[u] * e_in[bi][:, hsl[h]]], axis=0)
           for u, (bi, h) in enumerate(units)]
    rhs = [jnp.concatenate([b_h[u] * e_neg[bi][:, hsl[h]], kx_h[u] * e_neg[bi][:, hsl[h]]], axis=0)
           for u, (bi, h) in enumerate(units)]

    g4 = [dot_nt(lhs[u], rhs[u]) for u in un]
    ks = [dot_nt(lhs[u], s0[u]) for u in un]
    tb = [jnp.where(strict, g4[u][:L, :L], 0.0) for u in un]
    tk = [jnp.where(strict, g4[u][:L, L:], 0.0) for u in un]
    qb = [jnp.where(incl, g4[u][L:, :L], 0.0) for u in un]
    qk = [jnp.where(incl, g4[u][L:, L:], 0.0) for u in un]

    x = [eye - tb[u] for u in un]
    p = [dot(tb[u], tb[u]) for u in un]
    tkv = [dot(tk[u], v_h[u]) for u in un]
    for i in range(n_sq):
        x = [x[u] + dot(x[u], p[u]) for u in un]
        if i + 1 < n_sq:
            p = [dot(p[u], p[u]) for u in un]

    uu = [dot(x[u], ks[u][:L] + tkv[u]) for u in un]
    y = [ks[u][L:] + dot(qk[u], v_h[u]) - dot(qb[u], uu[u]) for u in un]
    kl = [kx_h[u] * e_last[bi][:, hsl[h]] for u, (bi, h) in enumerate(units)]
    bl = [b_h[u] * e_last[bi][:, hsl[h]] for u, (bi, h) in enumerate(units)]
    vk = [dot_tn(v_h[u], kl[u]) for u in un]
    ub = [dot_tn(uu[u], bl[u]) for u in un]
    for u, (bi, h) in enumerate(units):
        s_ref[bi, h] = s0[u] * w_chunk[bi][:, hsl[h]] + vk[u] - ub[u]

    mean = [jnp.mean(y[u], axis=-1, keepdims=True) for u in un]
    rk = [jnp.sum(r_h[u] * kx_h[u] * rk_ref[:, hsl[h]], axis=-1, keepdims=True) for u, (bi, h) in enumerate(units)]
    yc = [y[u] - mean[u] for u in un]
    var = [jnp.mean(jnp.square(yc[u]), axis=-1, keepdims=True) for u in un]
    inv = [lax.rsqrt(var[u] + GN_EPS) for u in un]
    for u, (bi, h) in enumerate(units):
        hs = hsl[h]
        yn = yc[u] * inv[u] * gg_ref[:, hs] + gb_ref[:, hs]
        y_ref[bi, :, pl.ds(h * C_HD, C_HD)] = ((yn + rk[u] * v_h[u]) * g[bi][:, hs]).astype(y_ref.dtype)


def _rwkv(cols, shift0, s0, lp, L):
    b, t, _ = cols.shape
    nc = t // L
    state = lambda shape: pl.BlockSpec(shape, lambda i, c: (i,) + (0,) * (len(shape) - 1))
    par = lambda arr: pl.BlockSpec(arr.shape, lambda i, c: (0, 0))
    params = [lp['c_mu'], lp['c_w0'], lp['c_w_w2'], lp['c_a0'], lp['c_w_a2'], lp['c_w_g2'],
              lp['c_k_k'], lp['c_k_a'], lp['c_r_k'], lp['c_gn_g'], lp['c_gn_b']]
    bb = _batch_block(b, RWKV_ROWS_PER_STEP)
    return pl.pallas_call(
        functools.partial(_rwkv_kernel, L=L),
        grid=(b // bb, nc),
        in_specs=[pl.BlockSpec((bb, L, COL_C_PAD), lambda i, c: (i, c, COL_C // COL_C_PAD)),
                  state((bb, 1, C_COLS)), state((bb, C_HEADS, C_HD, C_HD))] + [par(p) for p in params],
        out_specs=[pl.BlockSpec((bb, L, C_W), lambda i, c: (i, c, 0)),
                   state((bb, C_HEADS, C_HD, C_HD)), state((bb, 1, C_COLS))],
        out_shape=[jax.ShapeDtypeStruct((b, t, C_W), BF16),
                   jax.ShapeDtypeStruct((b, C_HEADS, C_HD, C_HD), F32),
                   jax.ShapeDtypeStruct((b, 1, C_COLS), F32)],
        compiler_params=_cparams("parallel", "arbitrary"),
    )(cols, shift0, s0, *params)


def _merge_kernel(x_ref, gpre_ref, wg_ref, ha_ref, hb_ref, hc_ref, wa_ref, wb_ref, wc_ref, wo_ref,
                  ng_ref, o_ref):
    d = x_ref.shape[1]
    x = x_ref[...]
    xn = (x * lax.rsqrt(jnp.mean(x * x, axis=-1, keepdims=True) + NORM_EPS) * gpre_ref[...]).astype(BF16)

    def branch(idx, h_ref, w_ref):
        gate = jax.nn.sigmoid(jnp.dot(xn, wg_ref[:, idx * d:(idx + 1) * d], preferred_element_type=F32))
        return gate * jnp.dot(h_ref[...].astype(BF16), w_ref[...], preferred_element_type=F32)

    merged = branch(0, ha_ref, wa_ref) + branch(1, hb_ref, wb_ref) + branch(2, hc_ref, wc_ref)
    mix = jnp.dot(merged.astype(BF16), wo_ref[...], preferred_element_type=F32)
    y = mix * lax.rsqrt(jnp.mean(mix * mix, axis=-1, keepdims=True) + NORM_EPS) * ng_ref[...]
    o_ref[...] = x + y


def _merge(x, gpre, wg, h_a, h_b, h_c, wa, wb, wc, wo, ng, tm):
    t, d = x.shape
    tok = lambda w: pl.BlockSpec((tm, w), lambda i: (i, 0))
    full = lambda arr: pl.BlockSpec(arr.shape, lambda i: (0, 0))
    return pl.pallas_call(
        _merge_kernel,
        grid=(t // tm,),
        in_specs=[tok(d), full(gpre), full(wg), tok(A_W), tok(B_W), tok(C_W),
                  full(wa), full(wb), full(wc), full(wo), full(ng)],
        out_specs=tok(d),
        out_shape=jax.ShapeDtypeStruct((t, d), F32),
        compiler_params=_cparams("parallel"),
    )(x, gpre, wg, h_a, h_b, h_c, wa, wb, wc, wo, ng)


def _ffn_kernel(x_ref, gpre_ref, wi_ref, wo_ref, gpost_ref, o_ref, *, tf):
    x = x_ref[...]
    xn = (x * lax.rsqrt(jnp.mean(x * x, axis=-1, keepdims=True) + NORM_EPS) * gpre_ref[...]).astype(BF16)
    f = None
    for c in range(D_FF // tf):
        gate = jnp.dot(xn, wi_ref[:, c * tf:(c + 1) * tf], preferred_element_type=F32)
        up = jnp.dot(xn, wi_ref[:, D_FF + c * tf:D_FF + (c + 1) * tf], preferred_element_type=F32)
        act = (gate * jax.nn.sigmoid(gate) * up).astype(BF16)
        part = jnp.dot(act, wo_ref[c * tf:(c + 1) * tf, :], preferred_element_type=F32)
        f = part if f is None else f + part
    y = f * lax.rsqrt(jnp.mean(f * f, axis=-1, keepdims=True) + NORM_EPS) * gpost_ref[...]
    o_ref[...] = x + y


def _ffn(x, gpre, w_in, w_out, gpost, tm, tf):
    t, d = x.shape
    full = lambda arr: pl.BlockSpec(arr.shape, lambda i: (0, 0))
    return pl.pallas_call(
        functools.partial(_ffn_kernel, tf=tf),
        grid=(t // tm,),
        in_specs=[pl.BlockSpec((tm, d), lambda i: (i, 0)), full(gpre), full(w_in), full(w_out), full(gpost)],
        out_specs=pl.BlockSpec((tm, d), lambda i: (i, 0)),
        out_shape=jax.ShapeDtypeStruct((t, d), F32),
        compiler_params=_cparams("parallel"),
    )(x, gpre, w_in, w_out, gpost)


def _rel_bucket(rel):
    half = NUM_BUCKETS // 2
    exact = half // 2
    n = jnp.abs(rel)
    far = exact + (jnp.log(jnp.maximum(n, 1).astype(F32) / exact)
                   / math.log(REL_MAX_DIST / exact) * (half - exact)).astype(jnp.int32)
    far = jnp.minimum(far, half - 1)
    return jnp.where(rel > 0, half, 0) + jnp.where(n < exact, n, far)


def _rel_bias(table, n_q, n_before):
    rel = (jnp.arange(n_before + n_q)[None, :] - n_before) - jnp.arange(n_q)[:, None]
    bias = jnp.transpose(table[_rel_bucket(rel)].astype(F32), (2, 0, 1))
    return bias.reshape(B_KV, B_GROUP * n_q, n_before + n_q)


def _reorder_w_in(w):
    a_main = 4 * A_W
    a_all = a_main + 2 * A_HEADS
    b_all = B_W + 2 * B_KV * B_HD
    wa, wag = w[..., :a_main], w[..., a_main:a_all]
    wb = w[..., a_all:a_all + b_all]
    wc = w[..., a_all + b_all:a_all + b_all + C_COLS]
    wg = w[..., a_all + b_all + C_COLS:]
    z = lambda n: jnp.zeros(w.shape[:-1] + (n,), w.dtype)
    out = jnp.concatenate([wa, wc, z(COL_BQ - COL_C - C_COLS), wb, wag, z(N_COLS - COL_AG - 2 * A_HEADS)], axis=-1)
    assert out.shape[-1] == N_COLS
    return out.astype(BF16), wg.astype(BF16)


def _layer(x, st, lp, bias, L, tm):
    b, t, d = x.shape
    swa_k, swa_v, mc, mn, mm, rs, rshift = st
    x2 = x.reshape(b * t, d)
    cols2, side2 = _proj_in(x2, lp['norm_mix_pre'], lp['w_in'], tm, PROJ_IN_TN)
    cols = cols2.reshape(b, t, N_COLS)
    side = side2.reshape(b, t, SIDE_W)

    h_a, mc, mn, mm = _mlstm(cols, side, mc, mn, mm, lp['a_gate_bias'], lp['a_norm'], L)

    lq = bias.shape[1] // B_GROUP
    sinks = jnp.repeat(lp['b_sinks'][0], lq).reshape(B_KV, B_GROUP * lq, 1)
    keep = min(WINDOW, t)
    k_new = side[:, t - keep:, S_BK:S_BK + B_KV * B_HD]
    v_new = side[:, t - keep:, S_BV:S_BV + B_KV * B_HD]
    if swa_k is None:
        h_b = _swa_prompt(side, bias, sinks)
        swa_k, swa_v = k_new, v_new
    else:
        kf = jnp.concatenate([swa_k, k_new], axis=1)
        vf = jnp.concatenate([swa_v, v_new], axis=1)
        h_b = _swa_sample(side, kf, vf, bias, sinks)
        swa_k, swa_v = kf[:, t:], vf[:, t:]

    h_c, rs, rshift = _rwkv(cols, rshift, rs, lp, L)

    x2 = _merge(x2, lp['norm_mix_pre'], lp['w_gate'], h_a.reshape(b * t, A_W), h_b.reshape(b * t, B_W),
                h_c.reshape(b * t, C_W), lp['w_branch_a'], lp['w_branch_b'], lp['w_branch_c'], lp['w_out'],
                lp['norm_mix_post'], tm // 2)
    x2 = _ffn(x2, lp['norm_ffn_pre'], lp['w_ffn_in'], lp['w_ffn_out'], lp['norm_ffn_post'], tm, 256)
    return x2.reshape(b, t, d), (swa_k, swa_v, mc, mn, mm, rs, rshift)


def _trunk(x, layer_states, layers, bias, L, tm):
    new = []
    for l in range(DEPTH):
        x, st = _layer(x, layer_states[l], layers[l], bias, L, tm)
        new.append(st)
    return x, [jnp.stack([s[i] for s in new]) for i in range(7)]


def kernel(x_prompt, x_sample, cache_swa_k, cache_swa_v, state_mlstm_c, state_mlstm_n, state_mlstm_m, state_rwkv_s, state_rwkv_shift, w_in, norm_mix_pre, norm_mix_post, norm_ffn_pre, norm_ffn_post, a_gate_bias, a_norm, rel_bias, b_sinks, c_mu, c_w0, c_w_w2, c_a0, c_w_a2, c_w_g2, c_k_k, c_k_a, c_r_k, c_gn_g, c_gn_b, w_branch_a, w_branch_b, w_branch_c, w_out, w_ffn_in, w_ffn_out):
    bp, tp, _ = x_prompt.shape
    bs, ts, _ = x_sample.shape
    kw = B_KV * B_HD
    w_in_p, w_gate_p = _reorder_w_in(w_in)
    gate_bias = jnp.pad(a_gate_bias, ((0, 0), (0, LANES - 2 * A_HEADS)))
    row = lambda p, l: p[l][None, :]
    layers = []
    for l in range(DEPTH):
        layers.append({
            'w_in': w_in_p[l], 'w_gate': w_gate_p[l], 'norm_mix_pre': row(norm_mix_pre, l), 'norm_mix_post': row(norm_mix_post, l),
            'norm_ffn_pre': row(norm_ffn_pre, l), 'norm_ffn_post': row(norm_ffn_post, l),
            'a_gate_bias': row(gate_bias, l), 'a_norm': row(a_norm, l), 'b_sinks': row(b_sinks, l),
            'c_mu': row(c_mu, l), 'c_w0': row(c_w0, l), 'c_w_w2': c_w_w2[l], 'c_a0': row(c_a0, l),
            'c_w_a2': c_w_a2[l], 'c_w_g2': c_w_g2[l], 'c_k_k': row(c_k_k, l), 'c_k_a': row(c_k_a, l),
            'c_r_k': row(c_r_k, l), 'c_gn_g': row(c_gn_g, l), 'c_gn_b': row(c_gn_b, l),
            'w_branch_a': w_branch_a[l].astype(BF16), 'w_branch_b': w_branch_b[l].astype(BF16),
            'w_branch_c': w_branch_c[l].astype(BF16), 'w_out': w_out[l].astype(BF16),
            'w_ffn_in': w_ffn_in[l].astype(BF16), 'w_ffn_out': w_ffn_out[l].astype(BF16),
        })

    fresh = (None, None,
             jnp.zeros((bp, A_HEADS, A_DK, A_DK), F32), jnp.zeros((bp, A_HEADS, A_DK), F32),
             jnp.zeros((bp, 1, A_HEADS), F32), jnp.zeros((bp, C_HEADS, C_HD, C_HD), F32),
             jnp.zeros((bp, 1, C_COLS), F32))
    lp_chunk = min(CHUNK, tp)
    y_prompt, p_st = _trunk(x_prompt, [fresh] * DEPTH, layers,
                            _rel_bias(rel_bias, lp_chunk, WIN_CHUNKS * CHUNK), lp_chunk, min(1024, bp * tp))

    n_before = cache_swa_k.shape[2]
    carried = [(cache_swa_k[l].reshape(bs, n_before, kw), cache_swa_v[l].reshape(bs, n_before, kw),
                state_mlstm_c[l], state_mlstm_n[l], state_mlstm_m[l][:, None, :],
                state_rwkv_s[l], state_rwkv_shift[l]) for l in range(DEPTH)]
    ls_chunk = min(CHUNK, ts)
    y_sample, s_st = _trunk(x_sample, carried, layers, _rel_bias(rel_bias, ts, n_before), ls_chunk,
                            min(1024, bs * ts))

    def finish(st, b):
        swa_k, swa_v, mc, mn, mm, rs, rshift = st
        n_rows = swa_k.shape[2]
        return (swa_k.reshape(DEPTH, b, n_rows, B_KV, B_HD), swa_v.reshape(DEPTH, b, n_rows, B_KV, B_HD),
                mc, mn, mm.reshape(DEPTH, b, A_HEADS), rs, rshift)

    return (y_prompt, y_sample) + finish(p_st, bp) + finish(s_st, bs)
```

```python
import functools
import math

import jax
import jax.numpy as jnp
from jax import lax
from jax.experimental import pallas as pl
from jax.experimental.pallas import tpu as pltpu

F32 = jnp.float32
BF16 = jnp.bfloat16

D_MODEL = 1024
DEPTH = 4
CHUNK = 64
NORM_EPS = 1e-6

A_HEADS = 4
A_DK = 128
A_W = 512
B_HEADS = 8
B_KV = 2
B_GROUP = 4
B_HD = 64
B_W = 512
WINDOW = 128
WIN_CHUNKS = 2
NUM_BUCKETS = 32
REL_MAX_DIST = 256
C_HEADS = 8
C_HD = 64
C_W = 512
C_DECAY_LORA = 64
C_AAA_LORA = 64
C_GATE_LORA = 128
C_COLS = 1792
GN_EPS = 64e-5
D_FF = 2816

COL_A = 0
COL_C = 2048
COL_C_PAD = 2048
COL_BQ = 4096
COL_BK = 4608
COL_BV = 4736
COL_AG = 4864
N_COLS = 5120
PROJ_IN_TN = 2560

LANES = 128
SIDE_COL0 = COL_BQ
SIDE_W = COL_AG + LANES - COL_BQ
S_BQ, S_BK, S_BV, S_AG = 0, COL_BK - COL_BQ, COL_BV - COL_BQ, COL_AG - COL_BQ
VMEM_LIMIT = 56 * 1024 * 1024
HI = lax.Precision.HIGHEST

NT = (((1,), (1,)), ((), ()))
TN = (((0,), (0,)), ((), ()))


MLSTM_ROWS_PER_STEP = 4
SWA_ROWS_PER_STEP = 4
RWKV_ROWS_PER_STEP = 4
RWKV_ROW_SKEW = 7


def _batch_block(b, want):
    return max(d for d in range(1, want + 1) if b % d == 0)


def _cparams(*sem):
    return pltpu.CompilerParams(dimension_semantics=sem, vmem_limit_bytes=VMEM_LIMIT)


def _run_skewed(programs, skew):
    pending, active, step = list(programs), [], 0
    while pending or active:
        if pending and step % skew == 0:
            active.append(pending.pop(0))
        for prog in list(active):
            if next(prog, StopIteration) is StopIteration:
                active.remove(prog)
        step += 1


def _tri(n, strict=False):
    r = lax.broadcasted_iota(jnp.int32, (n, n), 0)
    c = lax.broadcasted_iota(jnp.int32, (n, n), 1)
    return (c < r) if strict else (c <= r)


def _cumsum_rows(x):
    row = lax.broadcasted_iota(jnp.int32, x.shape, 0)
    shift = 1
    while shift < x.shape[0]:
        x = x + jnp.where(row >= shift, pltpu.roll(x, shift, axis=0), 0.0)
        shift *= 2
    return x


def _softplus(z):
    return jnp.maximum(z, 0.0) + jnp.log1p(jnp.exp(-jnp.abs(z)))


def _proj_in_kernel(x_ref, g_ref, w_ref, o_ref, gate_ref, xn_ref, *, tn):
    j = pl.program_id(1)

    @pl.when(j == 0)
    def _():
        x = x_ref[...]
        y = x * lax.rsqrt(jnp.mean(x * x, axis=-1, keepdims=True) + NORM_EPS) * g_ref[...]
        xn_ref[...] = y.astype(BF16)

    acc = jnp.dot(xn_ref[...], w_ref[...], preferred_element_type=F32)
    o_ref[...] = acc.astype(BF16)

    @pl.when(j == SIDE_COL0 // tn)
    def _():
        gate_ref[...] = acc[:, SIDE_COL0 % tn:SIDE_COL0 % tn + SIDE_W]


def _proj_in(x, g, w, tm, tn):
    assert SIDE_COL0 // tn == (SIDE_COL0 + SIDE_W - 1) // tn
    t, d = x.shape
    n = w.shape[1]
    return pl.pallas_call(
        functools.partial(_proj_in_kernel, tn=tn),
        grid=(t // tm, n // tn),
        in_specs=[
            pl.BlockSpec((tm, d), lambda i, j: (i, 0)),
            pl.BlockSpec((1, d), lambda i, j: (0, 0)),
            pl.BlockSpec((d, tn), lambda i, j: (0, j)),
        ],
        out_specs=[pl.BlockSpec((tm, tn), lambda i, j: (i, j)),
                   pl.BlockSpec((tm, SIDE_W), lambda i, j: (i, 0))],
        out_shape=[jax.ShapeDtypeStruct((t, n), BF16), jax.ShapeDtypeStruct((t, SIDE_W), F32)],
        scratch_shapes=[pltpu.VMEM((tm, d), BF16)],
        compiler_params=_cparams("parallel", "arbitrary"),
    )(x, g, w)


def _mlstm_kernel(q_ref, k_ref, v_ref, o_ref, g_ref, c0_ref, n0_ref, m0_ref, gb_ref, ng_ref,
                  h_ref, c_ref, n_ref, m_ref, *, L):
    @pl.when(pl.program_id(1) == 0)
    def _():
        c_ref[...] = c0_ref[...]
        n_ref[...] = n0_ref[...]
        m_ref[...] = m0_ref[...]

    bb = q_ref.shape[0]
    lane = lax.broadcasted_iota(jnp.int32, (L, LANES), 1)
    sel = (lax.broadcasted_iota(jnp.int32, (8, LANES), 0)
           == lax.broadcasted_iota(jnp.int32, (8, LANES), 1)).astype(F32)
    tri_f = _tri(L).astype(F32)
    z, zrow = [], []
    for bi in range(bb):
        gates = g_ref[bi] + gb_ref[...]
        lf = jnp.minimum(gates, 0.0) - jnp.log1p(jnp.exp(-jnp.abs(gates)))
        cum = jnp.dot(tri_f, lf, precision=HI, preferred_element_type=F32)
        z.append(jnp.where(lane < A_HEADS, gates, cum))
        zrow.append(lax.dot_general(sel, z[bi], NT, precision=HI, preferred_element_type=F32))
    causal = _tri(L)
    hsl = [pl.ds(h * A_DK, A_DK) for h in range(A_HEADS)]
    units = [(bi, h) for bi in range(bb) for h in range(A_HEADS)]
    un = range(len(units))

    q = [q_ref[bi, :, hsl[h]].astype(F32) for bi, h in units]
    k = [k_ref[bi, :, hsl[h]].astype(F32) * (A_DK ** -0.5) for bi, h in units]
    v = [v_ref[bi, :, hsl[h]].astype(F32) for bi, h in units]
    c_mat = [c_ref[bi, h] for bi, h in units]
    n_vec = [n_ref[bi, h:h + 1, :] for bi, h in units]
    m_prev = [m_ref[bi, :, h:h + 1] for bi, h in units]
    f_col = [z[bi][:, A_HEADS + h:A_HEADS + h + 1] for bi, h in units]
    li_col = [z[bi][:, h:h + 1] for bi, h in units]

    s = [lax.dot_general(q[u], k[u], NT, preferred_element_type=F32) for u in un]
    qc = [jnp.dot(q[u], c_mat[u], preferred_element_type=F32) for u in un]

    dlog = [jnp.where(causal, f_col[u] - zrow[bi][A_HEADS + h:A_HEADS + h + 1, :] + zrow[bi][h:h + 1, :], -jnp.inf)
            for u, (bi, h) in enumerate(units)]
    dmax = [jnp.max(dlog[u], axis=-1, keepdims=True) for u in un]
    qn = [jnp.sum(q[u] * n_vec[u], axis=-1, keepdims=True) for u in un]
    m_inter = [f_col[u] + m_prev[u] for u in un]
    m_t = [jnp.maximum(m_inter[u], dmax[u]) for u in un]
    w = [jnp.exp(dlog[u] - m_t[u]) * s[u] for u in un]
    inter = [jnp.exp(m_inter[u] - m_t[u]) for u in un]
    m_new = [m_t[u][L - 1:L, :] for u in un]
    f_last = [f_col[u][L - 1:L, :] for u in un]
    kw = [jnp.exp(f_last[u] - f_col[u] + li_col[u] - m_new[u]) * k[u] for u in un]
    decay = [jnp.exp(f_last[u] + m_prev[u] - m_new[u]) for u in un]

    wv = [jnp.dot(w[u], v[u], preferred_element_type=F32) for u in un]
    kv = [lax.dot_general(kw[u], v[u], TN, preferred_element_type=F32) for u in un]

    wsum = [jnp.sum(w[u], axis=-1, keepdims=True) for u in un]
    ksum = [jnp.sum(kw[u], axis=0, keepdims=True) for u in un]
    floor = [jnp.exp(-m_t[u]) for u in un]
    den = [jnp.maximum(jnp.abs(wsum[u] + inter[u] * qn[u]), floor[u]) for u in un]
    hh = [(wv[u] + inter[u] * qc[u]) / den[u] for u in un]
    ms = [jnp.mean(hh[u] * hh[u], axis=-1, keepdims=True) for u in un]
    scale = [lax.rsqrt(ms[u] + NORM_EPS) for u in un]
    for u, (bi, h) in enumerate(units):
        gate = jax.nn.sigmoid(o_ref[bi, :, hsl[h]].astype(F32))
        h_ref[bi, :, hsl[h]] = (gate * (hh[u] * scale[u] * ng_ref[:, hsl[h]])).astype(h_ref.dtype)
    for u, (bi, h) in enumerate(units):
        c_ref[bi, h] = decay[u] * c_mat[u] + kv[u]
        n_ref[bi, h:h + 1, :] = decay[u] * n_vec[u] + ksum[u]
        m_ref[bi, :, h:h + 1] = m_new[u]


def _mlstm(cols, gates, c0, n0, m0, gate_bias, norm_g, L):
    b, t, _ = cols.shape
    nc = t // L
    bb = _batch_block(b, MLSTM_ROWS_PER_STEP)

    def col(idx):
        return pl.BlockSpec((bb, L, A_W), lambda i, c: (i, c, COL_A // A_W + idx))

    state = lambda shape: pl.BlockSpec(shape, lambda i, c: (i,) + (0,) * (len(shape) - 1))
    return pl.pallas_call(
        functools.partial(_mlstm_kernel, L=L),
        grid=(b // bb, nc),
        in_specs=[
            col(0), col(1), col(2), col(3),
            pl.BlockSpec((bb, L, LANES), lambda i, c: (i, c, S_AG // LANES)),
            state((bb, A_HEADS, A_DK, A_DK)), state((bb, A_HEADS, A_DK)), state((bb, 1, A_HEADS)),
            pl.BlockSpec((1, LANES), lambda i, c: (0, 0)),
            pl.BlockSpec((1, A_W), lambda i, c: (0, 0)),
        ],
        out_specs=[
            pl.BlockSpec((bb, L, A_W), lambda i, c: (i, c, 0)),
            state((bb, A_HEADS, A_DK, A_DK)), state((bb, A_HEADS, A_DK)), state((bb, 1, A_HEADS)),
        ],
        out_shape=[
            jax.ShapeDtypeStruct((b, t, A_W), BF16),
            jax.ShapeDtypeStruct((b, A_HEADS, A_DK, A_DK), F32),
            jax.ShapeDtypeStruct((b, A_HEADS, A_DK), F32),
            jax.ShapeDtypeStruct((b, 1, A_HEADS), F32),
        ],
        compiler_params=_cparams("parallel", "arbitrary"),
    )(cols, cols, cols, cols, gates, c0, n0, m0, gate_bias, norm_g)


def _swa_kernel(*refs, n_kv, first_valid_fn):
    q_ref = refs[0]
    k_refs = refs[1:1 + n_kv]
    v_refs = refs[1 + n_kv:1 + 2 * n_kv]
    bias_ref, sink_ref, o_ref = refs[1 + 2 * n_kv:]
    bb, lq, _ = q_ref.shape
    cat = lambda rs, bi: jnp.concatenate([r[bi].astype(F32) for r in rs], axis=0)
    k = [cat(k_refs, bi) for bi in range(bb)]
    v = [cat(v_refs, bi) for bi in range(bb)]
    lk = k[0].shape[0]
    first_valid = first_valid_fn(pl.program_id(1))
    valid = lax.broadcasted_iota(jnp.int32, (B_GROUP * lq, lk), 1) >= first_valid
    units = [(bi, n) for bi in range(bb) for n in range(B_KV)]
    un = range(len(units))
    head = lambda n, g: pl.ds((n * B_GROUP + g) * B_HD, B_HD)

    qn = [jnp.concatenate([q_ref[bi, :, head(n, g)].astype(F32) for g in range(B_GROUP)], axis=0)
          for bi, n in units]
    s = [lax.dot_general(qn[u], k[bi][:, n * B_HD:(n + 1) * B_HD], NT, preferred_element_type=F32)
         for u, (bi, n) in enumerate(units)]
    sk = [sink_ref[n] for _, n in units]
    s = [jnp.where(valid, s[u] * (B_HD ** -0.5) + bias_ref[n], -1e30) for u, (bi, n) in enumerate(units)]
    mx = [jnp.max(s[u], axis=-1, keepdims=True) for u in un]
    mx = [jnp.maximum(mx[u], sk[u]) for u in un]
    p = [jnp.exp(s[u] - mx[u]) for u in un]
    pv = [jnp.dot(p[u].astype(v[bi].dtype), v[bi][:, n * B_HD:(n + 1) * B_HD], preferred_element_type=F32)
          for u, (bi, n) in enumerate(units)]
    psum = [jnp.sum(p[u], axis=-1, keepdims=True) for u in un]
    den = [psum[u] + jnp.exp(sk[u] - mx[u]) for u in un]
    o = [pv[u] / den[u] for u in un]
    for u, (bi, n) in enumerate(units):
        for g in range(B_GROUP):
            o_ref[bi, :, head(n, g)] = o[u][g * lq:(g + 1) * lq, :].astype(o_ref.dtype)


def _swa_prompt(cols, bias, sinks):
    b, t, _ = cols.shape
    nc = t // CHUNK
    n_kv = WIN_CHUNKS + 1
    kw = B_KV * B_HD
    bb = _batch_block(b, SWA_ROWS_PER_STEP)

    def kv_spec(col0, back):
        return pl.BlockSpec((bb, CHUNK, kw), lambda i, c: (i, jnp.maximum(c - back, 0), col0 // kw))

    backs = list(range(WIN_CHUNKS, -1, -1))
    return pl.pallas_call(
        functools.partial(_swa_kernel, n_kv=n_kv, first_valid_fn=lambda c: (WIN_CHUNKS - c) * CHUNK),
        grid=(b // bb, nc),
        in_specs=(
            [pl.BlockSpec((bb, CHUNK, B_W), lambda i, c: (i, c, S_BQ // B_W))]
            + [kv_spec(S_BK, bk) for bk in backs]
            + [kv_spec(S_BV, bk) for bk in backs]
            + [pl.BlockSpec(bias.shape, lambda i, c: (0, 0, 0)),
               pl.BlockSpec(sinks.shape, lambda i, c: (0, 0, 0))]
        ),
        out_specs=pl.BlockSpec((bb, CHUNK, B_W), lambda i, c: (i, c, 0)),
        out_shape=jax.ShapeDtypeStruct((b, t, B_W), BF16),
        compiler_params=_cparams("parallel", "parallel"),
    )(cols, *([cols] * (2 * n_kv)), bias, sinks)


def _swa_sample(cols, kf, vf, bias, sinks):
    b, t, _ = cols.shape
    lk = kf.shape[1]
    kw = B_KV * B_HD
    bb = _batch_block(b, SWA_ROWS_PER_STEP)
    return pl.pallas_call(
        functools.partial(_swa_kernel, n_kv=1, first_valid_fn=lambda c: 0),
        grid=(b // bb, 1),
        in_specs=[
            pl.BlockSpec((bb, t, B_W), lambda i, c: (i, 0, S_BQ // B_W)),
            pl.BlockSpec((bb, lk, kw), lambda i, c: (i, 0, 0)),
            pl.BlockSpec((bb, lk, kw), lambda i, c: (i, 0, 0)),
            pl.BlockSpec(bias.shape, lambda i, c: (0, 0, 0)),
            pl.BlockSpec(sinks.shape, lambda i, c: (0, 0, 0)),
        ],
        out_specs=pl.BlockSpec((bb, t, B_W), lambda i, c: (i, 0, 0)),
        out_shape=jax.ShapeDtypeStruct((b, t, B_W), BF16),
        compiler_params=_cparams("parallel", "arbitrary"),
    )(cols, kf, vf, bias, sinks)


def _rwkv_row(bi, c_ref, mu_ref, w0_ref, ww2_ref, a0_ref, wa2_ref, wg2_ref, kk_ref, ka_ref, rk_ref, gg_ref, gb_ref,
              y_ref, s_ref, sh_ref, L):
    dot = functools.partial(jnp.dot, preferred_element_type=F32)
    dot_nt = lambda a_, b_: lax.dot_general(a_, b_, NT, preferred_element_type=F32)
    dot_tn = lambda a_, b_: lax.dot_general(a_, b_, TN, preferred_element_type=F32)
    heads = range(C_HEADS)
    hsl = [slice(h * C_HD, (h + 1) * C_HD) for h in heads]
    pairs = range(C_W // LANES)
    psl = [slice(p * LANES, (p + 1) * LANES) for p in pairs]
    low = lax.broadcasted_iota(jnp.int32, (L, LANES), 1) < C_HD

    def per_head_sum(x):
        out = []
        for p in pairs:
            t = x[:, psl[p]]
            first = jnp.sum(jnp.where(low, t, 0.0), axis=-1, keepdims=True)
            second = jnp.sum(jnp.where(low, 0.0, t), axis=-1, keepdims=True)
            out.append(jnp.where(low, first, second))
        return jnp.concatenate(out, axis=1)

    cc = c_ref[bi][:, :C_COLS].astype(F32)
    row = lax.broadcasted_iota(jnp.int32, (L, C_COLS), 0)
    prev = jnp.where(row == 0, sh_ref[bi], pltpu.roll(cc, 1, axis=0))
    sh_ref[bi] = cc[L - 1:L, :]
    xm = cc + (prev - cc) * mu_ref[...]
    r = xm[:, 0:C_W]
    k = xm[:, C_W:2 * C_W]
    v = xm[:, 2 * C_W:3 * C_W]
    o1 = 3 * C_W
    wl = xm[:, o1:o1 + C_DECAY_LORA]
    al = xm[:, o1 + C_DECAY_LORA:o1 + C_DECAY_LORA + C_AAA_LORA]
    gl = xm[:, o1 + C_DECAY_LORA + C_AAA_LORA:C_COLS]
    yield
    wlog = -_softplus(-(w0_ref[...] + dot(jnp.tanh(wl), ww2_ref[...]))) - 0.5
    lw = -jnp.exp(wlog)
    a = jax.nn.sigmoid(a0_ref[...] + dot(al, wa2_ref[...]))
    g = dot(jax.nn.sigmoid(gl), wg2_ref[...])
    yield
    cum = _cumsum_rows(lw)
    kkf = k * kk_ref[...]
    kx = k * (1.0 + (a - 1.0) * ka_ref[...])
    kk_sq = per_head_sum(kkf * kkf)
    rk = per_head_sum(r * kx * rk_ref[...])
    yield
    cum_last = cum[L - 1:L, :]
    e_in = jnp.exp(cum)
    e_ex = jnp.exp(cum - lw)
    e_neg = jnp.exp(-cum)
    e_last = jnp.exp(cum_last - cum)
    w_chunk = jnp.exp(cum_last)
    kk = kkf / jnp.maximum(jnp.sqrt(kk_sq), 1e-12)
    yield
    b = kk * a
    kt = kk * e_ex
    rt = r * e_in
    bh = b * e_neg
    kh = kx * e_neg
    kl = kx * e_last
    bl = b * e_last
    yield
    lhs = [jnp.concatenate([kt[:, hs], rt[:, hs]], axis=0) for hs in hsl]
    s0 = [s_ref[bi, h] for h in heads]
    rhs = [jnp.concatenate([bh[:, hsl[h]], kh[:, hsl[h]], s0[h]], axis=0) for h in heads]
    yield
    zeros = jnp.zeros((L, C_HD), F32)
    v_h = [v[:, hs] for hs in hsl]
    v0 = [jnp.concatenate([zeros, v_h[h]], axis=0) for h in heads]
    klbl = [jnp.concatenate([kl[:, hs], bl[:, hs]], axis=0) for hs in hsl]
    yield

    gs = [dot_nt(lhs[h], rhs[h]) for h in heads]
    yield
    r2 = lax.broadcasted_iota(jnp.int32, (L, 2 * L), 0)
    c2 = lax.broadcasted_iota(jnp.int32, (L, 2 * L), 1)
    second = c2 >= L
    cc2 = jnp.where(second, c2 - L, c2)
    eye = (lax.broadcasted_iota(jnp.int32, (L, L), 0) == lax.broadcasted_iota(jnp.int32, (L, L), 1)).astype(F32)
    tb = [jnp.where(_tri(L, strict=True), gs[h][:L, :L], 0.0) for h in heads]
    tk0 = [jnp.where(second & (cc2 < r2), gs[h][:L, :2 * L], 0.0) for h in heads]
    yield
    qq = [jnp.where(cc2 <= r2, jnp.where(second, gs[h][L:, :2 * L], -gs[h][L:, :2 * L]), 0.0) for h in heads]
    ks = [gs[h][:, 2 * L:] for h in heads]
    x = [eye - tb[h] for h in heads]
    p = [dot(tb[h], tb[h]) for h in heads]
    tkv = [dot(tk0[h], v0[h]) for h in heads]
    yield
    n_sq = int(math.log2(L)) - 1
    for i in range(n_sq):
        if i + 1 < n_sq:
            xp = [dot(jnp.concatenate([x[h], p[h]], axis=0), p[h]) for h in heads]
            x = [x[h] + xp[h][:L] for h in heads]
            p = [xp[h][L:] for h in heads]
        else:
            x = [x[h] + dot(x[h], p[h]) for h in heads]
        yield
    uu = [dot(x[h], ks[h][:L] + tkv[h]) for h in heads]
    yield
    y = [ks[h][L:] + dot(qq[h], jnp.concatenate([uu[h], v_h[h]], axis=0)) for h in heads]
    ds = [dot_tn(jnp.concatenate([v_h[h], -uu[h]], axis=0), klbl[h]) for h in heads]
    yield
    for h in heads:
        s_ref[bi, h] = s0[h] * w_chunk[:, hsl[h]] + ds[h]
    mean = [jnp.mean(y[h], axis=-1, keepdims=True) for h in heads]
    yield
    yc = [y[h] - mean[h] for h in heads]
    var = [jnp.mean(jnp.square(yc[h]), axis=-1, keepdims=True) for h in heads]
    yield
    yn = [yc[h] * lax.rsqrt(var[h] + GN_EPS) for h in heads]
    yn = jnp.concatenate(yn, axis=1)
    yield
    y_ref[bi] = ((yn * gg_ref[...] + gb_ref[...] + rk * v) * g).astype(y_ref.dtype)


def _rwkv_kernel(c_ref, sh0_ref, s0_ref, mu_ref, w0_ref, ww2_ref, a0_ref, wa2_ref, wg2_ref,
                 kk_ref, ka_ref, rk_ref, gg_ref, gb_ref, y_ref, s_ref, sh_ref, *, L):
    @pl.when(pl.program_id(1) == 0)
    def _():
        s_ref[...] = s0_ref[...]
        sh_ref[...] = sh0_ref[...]

    rows = [_rwkv_row(bi, c_ref, mu_ref, w0_ref, ww2_ref, a0_ref, wa2_ref, wg2_ref, kk_ref, ka_ref, rk_ref,
                      gg_ref, gb_ref, y_ref, s_ref, sh_ref, L) for bi in range(c_ref.shape[0])]
    _run_skewed(rows, RWKV_ROW_SKEW)


def _rwkv(cols, shift0, s0, lp, L):
    b, t, _ = cols.shape
    nc = t // L
    state = lambda shape: pl.BlockSpec(shape, lambda i, c: (i,) + (0,) * (len(shape) - 1))
    par = lambda arr: pl.BlockSpec(arr.shape, lambda i, c: (0, 0))
    params = [lp['c_mu'], lp['c_w0'], lp['c_w_w2'], lp['c_a0'], lp['c_w_a2'], lp['c_w_g2'],
              lp['c_k_k'], lp['c_k_a'], lp['c_r_k'], lp['c_gn_g'], lp['c_gn_b']]
    bb = _batch_block(b, RWKV_ROWS_PER_STEP)
    return pl.pallas_call(
        functools.partial(_rwkv_kernel, L=L),
        grid=(b // bb, nc),
        in_specs=[pl.BlockSpec((bb, L, COL_C_PAD), lambda i, c: (i, c, COL_C // COL_C_PAD)),
                  state((bb, 1, C_COLS)), state((bb, C_HEADS, C_HD, C_HD))] + [par(p) for p in params],
        out_specs=[pl.BlockSpec((bb, L, C_W), lambda i, c: (i, c, 0)),
                   state((bb, C_HEADS, C_HD, C_HD)), state((bb, 1, C_COLS))],
        out_shape=[jax.ShapeDtypeStruct((b, t, C_W), BF16),
                   jax.ShapeDtypeStruct((b, C_HEADS, C_HD, C_HD), F32),
                   jax.ShapeDtypeStruct((b, 1, C_COLS), F32)],
        compiler_params=_cparams("parallel", "arbitrary"),
    )(cols, shift0, s0, *params)


def _merge_kernel(x_ref, gpre_ref, wg_ref, ha_ref, hb_ref, hc_ref, wa_ref, wb_ref, wc_ref, wo_ref,
                  ng_ref, o_ref):
    d = x_ref.shape[1]
    x = x_ref[...]
    xn = (x * lax.rsqrt(jnp.mean(x * x, axis=-1, keepdims=True) + NORM_EPS) * gpre_ref[...]).astype(BF16)

    def branch(idx, h_ref, w_ref):
        gate = jax.nn.sigmoid(jnp.dot(xn, wg_ref[:, idx * d:(idx + 1) * d], preferred_element_type=F32))
        return gate * jnp.dot(h_ref[...].astype(BF16), w_ref[...], preferred_element_type=F32)

    merged = branch(0, ha_ref, wa_ref) + branch(1, hb_ref, wb_ref) + branch(2, hc_ref, wc_ref)
    mix = jnp.dot(merged.astype(BF16), wo_ref[...], preferred_element_type=F32)
    y = mix * lax.rsqrt(jnp.mean(mix * mix, axis=-1, keepdims=True) + NORM_EPS) * ng_ref[...]
    o_ref[...] = x + y


def _merge(x, gpre, wg, h_a, h_b, h_c, wa, wb, wc, wo, ng, tm):
    t, d = x.shape
    tok = lambda w: pl.BlockSpec((tm, w), lambda i: (i, 0))
    full = lambda arr: pl.BlockSpec(arr.shape, lambda i: (0, 0))
    return pl.pallas_call(
        _merge_kernel,
        grid=(t // tm,),
        in_specs=[tok(d), full(gpre), full(wg), tok(A_W), tok(B_W), tok(C_W),
                  full(wa), full(wb), full(wc), full(wo), full(ng)],
        out_specs=tok(d),
        out_shape=jax.ShapeDtypeStruct((t, d), F32),
        compiler_params=_cparams("parallel"),
    )(x, gpre, wg, h_a, h_b, h_c, wa, wb, wc, wo, ng)


def _ffn_kernel(x_ref, gpre_ref, wi_ref, wo_ref, gpost_ref, o_ref, *, tf):
    x = x_ref[...]
    xn = (x * lax.rsqrt(jnp.mean(x * x, axis=-1, keepdims=True) + NORM_EPS) * gpre_ref[...]).astype(BF16)
    f = None
    for c in range(D_FF // tf):
        gate = jnp.dot(xn, wi_ref[:, c * tf:(c + 1) * tf], preferred_element_type=F32)
        up = jnp.dot(xn, wi_ref[:, D_FF + c * tf:D_FF + (c + 1) * tf], preferred_element_type=F32)
        act = (gate * jax.nn.sigmoid(gate) * up).astype(BF16)
        part = jnp.dot(act, wo_ref[c * tf:(c + 1) * tf, :], preferred_element_type=F32)
        f = part if f is None else f + part
    y = f * lax.rsqrt(jnp.mean(f * f, axis=-1, keepdims=True) + NORM_EPS) * gpost_ref[...]
    o_ref[...] = x + y


def _ffn(x, gpre, w_in, w_out, gpost, tm, tf):
    t, d = x.shape
    full = lambda arr: pl.BlockSpec(arr.shape, lambda i: (0, 0))
    return pl.pallas_call(
        functools.partial(_ffn_kernel, tf=tf),
        grid=(t // tm,),
        in_specs=[pl.BlockSpec((tm, d), lambda i: (i, 0)), full(gpre), full(w_in), full(w_out), full(gpost)],
        out_specs=pl.BlockSpec((tm, d), lambda i: (i, 0)),
        out_shape=jax.ShapeDtypeStruct((t, d), F32),
        compiler_params=_cparams("parallel"),
    )(x, gpre, w_in, w_out, gpost)


def _rel_bucket(rel):
    half = NUM_BUCKETS // 2
    exact = half // 2
    n = jnp.abs(rel)
    far = exact + (jnp.log(jnp.maximum(n, 1).astype(F32) / exact)
                   / math.log(REL_MAX_DIST / exact) * (half - exact)).astype(jnp.int32)
    far = jnp.minimum(far, half - 1)
    return jnp.where(rel > 0, half, 0) + jnp.where(n < exact, n, far)


def _rel_bias(table, n_q, n_before):
    rel = (jnp.arange(n_before + n_q)[None, :] - n_before) - jnp.arange(n_q)[:, None]
    onehot = (_rel_bucket(rel)[None] == jnp.arange(NUM_BUCKETS)[:, None, None]).astype(F32)
    bias = jnp.einsum('bh,bqk->hqk', table.astype(F32), onehot, precision=HI)
    return bias.reshape(B_KV, B_GROUP * n_q, n_before + n_q)


def _reorder_w_in(w):
    a_main = 4 * A_W
    a_all = a_main + 2 * A_HEADS
    b_all = B_W + 2 * B_KV * B_HD
    wa, wag = w[..., :a_main], w[..., a_main:a_all]
    wb = w[..., a_all:a_all + b_all]
    wc = w[..., a_all + b_all:a_all + b_all + C_COLS]
    wg = w[..., a_all + b_all + C_COLS:]
    z = lambda n: jnp.zeros(w.shape[:-1] + (n,), w.dtype)
    out = jnp.concatenate([wa, wc, z(COL_BQ - COL_C - C_COLS), wb, wag, z(N_COLS - COL_AG - 2 * A_HEADS)], axis=-1)
    assert out.shape[-1] == N_COLS
    return out.astype(BF16), wg.astype(BF16)


def _layer(x, st, lp, bias, L, tm):
    b, t, d = x.shape
    swa_k, swa_v, mc, mn, mm, rs, rshift = st
    x2 = x.reshape(b * t, d)
    cols2, side2 = _proj_in(x2, lp['norm_mix_pre'], lp['w_in'], tm, PROJ_IN_TN)
    cols = cols2.reshape(b, t, N_COLS)
    side = side2.reshape(b, t, SIDE_W)

    h_a, mc, mn, mm = _mlstm(cols, side, mc, mn, mm, lp['a_gate_bias'], lp['a_norm'], L)

    lq = bias.shape[1] // B_GROUP
    sinks = jnp.repeat(lp['b_sinks'][0], lq).reshape(B_KV, B_GROUP * lq, 1)
    keep = min(WINDOW, t)
    k_new = side[:, t - keep:, S_BK:S_BK + B_KV * B_HD]
    v_new = side[:, t - keep:, S_BV:S_BV + B_KV * B_HD]
    if swa_k is None:
        h_b = _swa_prompt(side, bias, sinks)
        swa_k, swa_v = k_new, v_new
    else:
        kf = jnp.concatenate([swa_k, k_new], axis=1)
        vf = jnp.concatenate([swa_v, v_new], axis=1)
        h_b = _swa_sample(side, kf, vf, bias, sinks)
        swa_k, swa_v = kf[:, t:], vf[:, t:]

    h_c, rs, rshift = _rwkv(cols, rshift, rs, lp, L)

    x2 = _merge(x2, lp['norm_mix_pre'], lp['w_gate'], h_a.reshape(b * t, A_W), h_b.reshape(b * t, B_W),
                h_c.reshape(b * t, C_W), lp['w_branch_a'], lp['w_branch_b'], lp['w_branch_c'], lp['w_out'],
                lp['norm_mix_post'], tm // 2)
    x2 = _ffn(x2, lp['norm_ffn_pre'], lp['w_ffn_in'], lp['w_ffn_out'], lp['norm_ffn_post'], tm, 256)
    return x2.reshape(b, t, d), (swa_k, swa_v, mc, mn, mm, rs, rshift)


def _trunk(x, layer_states, layers, bias, L, tm):
    new = []
    for l in range(DEPTH):
        x, st = _layer(x, layer_states[l], layers[l], bias, L, tm)
        new.append(st)
    return x, [jnp.stack([s[i] for s in new]) for i in range(7)]


def kernel(x_prompt, x_sample, cache_swa_k, cache_swa_v, state_mlstm_c, state_mlstm_n, state_mlstm_m, state_rwkv_s, state_rwkv_shift, w_in, norm_mix_pre, norm_mix_post, norm_ffn_pre, norm_ffn_post, a_gate_bias, a_norm, rel_bias, b_sinks, c_mu, c_w0, c_w_w2, c_a0, c_w_a2, c_w_g2, c_k_k, c_k_a, c_r_k, c_gn_g, c_gn_b, w_branch_a, w_branch_b, w_branch_c, w_out, w_ffn_in, w_ffn_out):
    bp, tp, _ = x_prompt.shape
    bs, ts, _ = x_sample.shape
    kw = B_KV * B_HD
    w_in_p, w_gate_p = _reorder_w_in(w_in)
    gate_bias = jnp.pad(a_gate_bias, ((0, 0), (0, LANES - 2 * A_HEADS)))
    row = lambda p, l: p[l][None, :]
    layers = []
    for l in range(DEPTH):
        layers.append({
            'w_in': w_in_p[l], 'w_gate': w_gate_p[l], 'norm_mix_pre': row(norm_mix_pre, l), 'norm_mix_post': row(norm_mix_post, l),
            'norm_ffn_pre': row(norm_ffn_pre, l), 'norm_ffn_post': row(norm_ffn_post, l),
            'a_gate_bias': row(gate_bias, l), 'a_norm': row(a_norm, l), 'b_sinks': row(b_sinks, l),
            'c_mu': row(c_mu, l), 'c_w0': row(c_w0, l), 'c_w_w2': c_w_w2[l], 'c_a0': row(c_a0, l),
            'c_w_a2': c_w_a2[l], 'c_w_g2': c_w_g2[l], 'c_k_k': row(c_k_k, l), 'c_k_a': row(c_k_a, l),
            'c_r_k': row(c_r_k, l), 'c_gn_g': row(c_gn_g, l), 'c_gn_b': row(c_gn_b, l),
            'w_branch_a': w_branch_a[l].astype(BF16), 'w_branch_b': w_branch_b[l].astype(BF16),
            'w_branch_c': w_branch_c[l].astype(BF16), 'w_out': w_out[l].astype(BF16),
            'w_ffn_in': w_ffn_in[l].astype(BF16), 'w_ffn_out': w_ffn_out[l].astype(BF16),
        })

    fresh = (None, None,
             jnp.zeros((bp, A_HEADS, A_DK, A_DK), F32), jnp.zeros((bp, A_HEADS, A_DK), F32),
             jnp.zeros((bp, 1, A_HEADS), F32), jnp.zeros((bp, C_HEADS, C_HD, C_HD), F32),
             jnp.zeros((bp, 1, C_COLS), F32))
    lp_chunk = min(CHUNK, tp)
    y_prompt, p_st = _trunk(x_prompt, [fresh] * DEPTH, layers,
                            _rel_bias(rel_bias, lp_chunk, WIN_CHUNKS * CHUNK), lp_chunk, min(1024, bp * tp))

    n_before = cache_swa_k.shape[2]
    carried = [(cache_swa_k[l].reshape(bs, n_before, kw), cache_swa_v[l].reshape(bs, n_before, kw),
                state_mlstm_c[l], state_mlstm_n[l], state_mlstm_m[l][:, None, :],
                state_rwkv_s[l], state_rwkv_shift[l]) for l in range(DEPTH)]
    ls_chunk = min(CHUNK, ts)
    y_sample, s_st = _trunk(x_sample, carried, layers, _rel_bias(rel_bias, ts, n_before), ls_chunk,
                            min(1024, bs * ts))

    def finish(st, b):
        swa_k, swa_v, mc, mn, mm, rs, rshift = st
        n_rows = swa_k.shape[2]
        return (swa_k.reshape(DEPTH, b, n_rows, B_KV, B_HD), swa_v.reshape(DEPTH, b, n_rows, B_KV, B_HD),
                mc, mn, mm.reshape(DEPTH, b, A_HEADS), rs, rshift)

    return (y_prompt, y_sample) + finish(p_st, bp) + finish(s_st, bs)
```

```python
import functools
import math

import jax
import jax.numpy as jnp
from jax import lax
from jax.experimental import pallas as pl
from jax.experimental.pallas import tpu as pltpu

F32 = jnp.float32
BF16 = jnp.bfloat16

D_MODEL = 1024
DEPTH = 4
CHUNK = 64
NORM_EPS = 1e-6

A_HEADS = 4
A_DK = 128
A_W = 512
B_HEADS = 8
B_KV = 2
B_GROUP = 4
B_HD = 64
B_W = 512
WINDOW = 128
WIN_CHUNKS = 2
NUM_BUCKETS = 32
REL_MAX_DIST = 256
C_HEADS = 8
C_HD = 64
C_W = 512
C_DECAY_LORA = 64
C_AAA_LORA = 64
C_GATE_LORA = 128
C_COLS = 1792
GN_EPS = 64e-5
D_FF = 2816

COL_A = 0
COL_C = 2048
COL_C_PAD = 2048
COL_BQ = 4096
COL_BK = 4608
COL_BV = 4736
COL_AG = 4864
N_COLS = 5120
PROJ_IN_TN = 2560

LANES = 128
SIDE_COL0 = COL_BQ
SIDE_W = COL_AG + LANES - COL_BQ
S_BQ, S_BK, S_BV, S_AG = 0, COL_BK - COL_BQ, COL_BV - COL_BQ, COL_AG - COL_BQ
VMEM_LIMIT = 56 * 1024 * 1024
HI = lax.Precision.HIGHEST

NT = (((1,), (1,)), ((), ()))
TN = (((0,), (0,)), ((), ()))


MLSTM_ROWS_PER_STEP = 4
SWA_ROWS_PER_STEP = 4
RWKV_ROWS_PER_STEP = 4
RWKV_ROW_SKEW = 7


def _batch_block(b, want):
    return max(d for d in range(1, want + 1) if b % d == 0)


def _cparams(*sem):
    return pltpu.CompilerParams(dimension_semantics=sem, vmem_limit_bytes=VMEM_LIMIT)


def _run_skewed(programs, skew):
    pending, active, step = list(programs), [], 0
    while pending or active:
        if pending and step % skew == 0:
            active.append(pending.pop(0))
        for prog in list(active):
            if next(prog, StopIteration) is StopIteration:
                active.remove(prog)
        step += 1


def _tri(n, strict=False):
    r = lax.broadcasted_iota(jnp.int32, (n, n), 0)
    c = lax.broadcasted_iota(jnp.int32, (n, n), 1)
    return (c < r) if strict else (c <= r)


def _cumsum_rows(x):
    row = lax.broadcasted_iota(jnp.int32, x.shape, 0)
    shift = 1
    while shift < x.shape[0]:
        x = x + jnp.where(row >= shift, pltpu.roll(x, shift, axis=0), 0.0)
        shift *= 2
    return x


def _softplus(z):
    return jnp.maximum(z, 0.0) + jnp.log1p(jnp.exp(-jnp.abs(z)))


def _cast_kernel(w_ref, o_ref):
    o_ref[...] = w_ref[...].astype(o_ref.dtype)


def _to_bf16(w, tr):
    depth, rows, cols = w.shape
    spec = pl.BlockSpec((1, tr, cols), lambda l, i: (l, i, 0))
    return pl.pallas_call(
        _cast_kernel, grid=(depth, rows // tr), in_specs=[spec], out_specs=spec,
        out_shape=jax.ShapeDtypeStruct(w.shape, BF16), compiler_params=_cparams("parallel", "parallel"),
    )(w)


def _split_w_in_kernel(w_ref, cols_ref, gate_ref):
    a_main = 4 * A_W
    a_all = a_main + 2 * A_HEADS
    b_all = B_W + 2 * B_KV * B_HD
    c0 = a_all + b_all
    dt = cols_ref.dtype
    cols_ref[0, :, COL_A:COL_A + a_main] = w_ref[0, :, 0:a_main].astype(dt)
    cols_ref[0, :, COL_C:COL_C + C_COLS] = w_ref[0, :, c0:c0 + C_COLS].astype(dt)
    cols_ref[0, :, COL_C + C_COLS:COL_BQ] = jnp.zeros((w_ref.shape[1], COL_BQ - COL_C - C_COLS), dt)
    cols_ref[0, :, COL_BQ:COL_BQ + b_all] = w_ref[0, :, a_all:a_all + b_all].astype(dt)
    cols_ref[0, :, COL_AG:N_COLS] = jnp.zeros((w_ref.shape[1], N_COLS - COL_AG), dt)
    cols_ref[0, :, COL_AG:COL_AG + 2 * A_HEADS] = w_ref[0, :, a_main:a_all].astype(dt)
    gate_ref[0] = w_ref[0, :, c0 + C_COLS:].astype(dt)


def _split_w_in(w, tr):
    depth, d, n_in = w.shape
    n_gate = n_in - (4 * A_W + 2 * A_HEADS + B_W + 2 * B_KV * B_HD + C_COLS)
    spec = lambda n: pl.BlockSpec((1, tr, n), lambda l, i: (l, i, 0))
    return pl.pallas_call(
        _split_w_in_kernel, grid=(depth, d // tr), in_specs=[spec(n_in)], out_specs=[spec(N_COLS), spec(n_gate)],
        out_shape=[jax.ShapeDtypeStruct((depth, d, N_COLS), BF16), jax.ShapeDtypeStruct((depth, d, n_gate), BF16)],
        compiler_params=_cparams("parallel", "parallel"),
    )(w)


def _proj_in_kernel(x_ref, g_ref, w_ref, o_ref, gate_ref, xn_ref, *, tn):
    j = pl.program_id(1)

    @pl.when(j == 0)
    def _():
        x = x_ref[...]
        y = x * lax.rsqrt(jnp.mean(x * x, axis=-1, keepdims=True) + NORM_EPS) * g_ref[...]
        xn_ref[...] = y.astype(BF16)

    acc = jnp.dot(xn_ref[...], w_ref[...], preferred_element_type=F32)
    o_ref[...] = acc.astype(BF16)

    @pl.when(j == SIDE_COL0 // tn)
    def _():
        gate_ref[...] = acc[:, SIDE_COL0 % tn:SIDE_COL0 % tn + SIDE_W]


def _proj_in(x, g, w, layer, tm, tn):
    assert SIDE_COL0 // tn == (SIDE_COL0 + SIDE_W - 1) // tn
    t, d = x.shape
    n = w.shape[2]
    return pl.pallas_call(
        functools.partial(_proj_in_kernel, tn=tn),
        grid=(t // tm, n // tn),
        in_specs=[
            pl.BlockSpec((tm, d), lambda i, j: (i, 0)),
            pl.BlockSpec((1, d), lambda i, j: (0, 0)),
            pl.BlockSpec((None, d, tn), lambda i, j: (layer, 0, j)),
        ],
        out_specs=[pl.BlockSpec((tm, tn), lambda i, j: (i, j)),
                   pl.BlockSpec((tm, SIDE_W), lambda i, j: (i, 0))],
        out_shape=[jax.ShapeDtypeStruct((t, n), BF16), jax.ShapeDtypeStruct((t, SIDE_W), F32)],
        scratch_shapes=[pltpu.VMEM((tm, d), BF16)],
        compiler_params=_cparams("parallel", "arbitrary"),
    )(x, g, w)


def _mlstm_kernel(q_ref, k_ref, v_ref, o_ref, g_ref, c0_ref, n0_ref, m0_ref, gb_ref, ng_ref,
                  h_ref, c_ref, n_ref, m_ref, *, L):
    @pl.when(pl.program_id(1) == 0)
    def _():
        c_ref[...] = c0_ref[...]
        n_ref[...] = n0_ref[...]
        m_ref[...] = m0_ref[...]

    bb = q_ref.shape[0]
    lane = lax.broadcasted_iota(jnp.int32, (L, LANES), 1)
    sel = (lax.broadcasted_iota(jnp.int32, (8, LANES), 0)
           == lax.broadcasted_iota(jnp.int32, (8, LANES), 1)).astype(F32)
    tri_f = _tri(L).astype(F32)
    z, zrow = [], []
    for bi in range(bb):
        gates = g_ref[bi] + gb_ref[...]
        lf = jnp.minimum(gates, 0.0) - jnp.log1p(jnp.exp(-jnp.abs(gates)))
        cum = jnp.dot(tri_f, lf, precision=HI, preferred_element_type=F32)
        z.append(jnp.where(lane < A_HEADS, gates, cum))
        zrow.append(lax.dot_general(sel, z[bi], NT, precision=HI, preferred_element_type=F32))
    causal = _tri(L)
    hsl = [pl.ds(h * A_DK, A_DK) for h in range(A_HEADS)]
    units = [(bi, h) for bi in range(bb) for h in range(A_HEADS)]
    un = range(len(units))

    q = [q_ref[bi, :, hsl[h]].astype(F32) for bi, h in units]
    k = [k_ref[bi, :, hsl[h]].astype(F32) * (A_DK ** -0.5) for bi, h in units]
    v = [v_ref[bi, :, hsl[h]].astype(F32) for bi, h in units]
    c_mat = [c_ref[bi, h] for bi, h in units]
    n_vec = [n_ref[bi, h:h + 1, :] for bi, h in units]
    m_prev = [m_ref[bi, :, h:h + 1] for bi, h in units]
    f_col = [z[bi][:, A_HEADS + h:A_HEADS + h + 1] for bi, h in units]
    li_col = [z[bi][:, h:h + 1] for bi, h in units]

    s = [lax.dot_general(q[u], k[u], NT, preferred_element_type=F32) for u in un]
    qc = [jnp.dot(q[u], c_mat[u], preferred_element_type=F32) for u in un]

    dlog = [jnp.where(causal, f_col[u] - zrow[bi][A_HEADS + h:A_HEADS + h + 1, :] + zrow[bi][h:h + 1, :], -jnp.inf)
            for u, (bi, h) in enumerate(units)]
    dmax = [jnp.max(dlog[u], axis=-1, keepdims=True) for u in un]
    qn = [jnp.sum(q[u] * n_vec[u], axis=-1, keepdims=True) for u in un]
    m_inter = [f_col[u] + m_prev[u] for u in un]
    m_t = [jnp.maximum(m_inter[u], dmax[u]) for u in un]
    w = [jnp.exp(dlog[u] - m_t[u]) * s[u] for u in un]
    inter = [jnp.exp(m_inter[u] - m_t[u]) for u in un]
    m_new = [m_t[u][L - 1:L, :] for u in un]
    f_last = [f_col[u][L - 1:L, :] for u in un]
    kw = [jnp.exp(f_last[u] - f_col[u] + li_col[u] - m_new[u]) * k[u] for u in un]
    decay = [jnp.exp(f_last[u] + m_prev[u] - m_new[u]) for u in un]

    wv = [jnp.dot(w[u], v[u], preferred_element_type=F32) for u in un]
    kv = [lax.dot_general(kw[u], v[u], TN, preferred_element_type=F32) for u in un]

    wsum = [jnp.sum(w[u], axis=-1, keepdims=True) for u in un]
    ksum = [jnp.sum(kw[u], axis=0, keepdims=True) for u in un]
    floor = [jnp.exp(-m_t[u]) for u in un]
    den = [jnp.maximum(jnp.abs(wsum[u] + inter[u] * qn[u]), floor[u]) for u in un]
    hh = [(wv[u] + inter[u] * qc[u]) / den[u] for u in un]
    ms = [jnp.mean(hh[u] * hh[u], axis=-1, keepdims=True) for u in un]
    scale = [lax.rsqrt(ms[u] + NORM_EPS) for u in un]
    for u, (bi, h) in enumerate(units):
        gate = jax.nn.sigmoid(o_ref[bi, :, hsl[h]].astype(F32))
        h_ref[bi, :, hsl[h]] = (gate * (hh[u] * scale[u] * ng_ref[:, hsl[h]])).astype(h_ref.dtype)
    for u, (bi, h) in enumerate(units):
        c_ref[bi, h] = decay[u] * c_mat[u] + kv[u]
        n_ref[bi, h:h + 1, :] = decay[u] * n_vec[u] + ksum[u]
        m_ref[bi, :, h:h + 1] = m_new[u]


def _mlstm(cols, gates, c0, n0, m0, gate_bias, norm_g, L):
    b, t, _ = cols.shape
    nc = t // L
    bb = _batch_block(b, MLSTM_ROWS_PER_STEP)

    def col(idx):
        return pl.BlockSpec((bb, L, A_W), lambda i, c: (i, c, COL_A // A_W + idx))

    state = lambda shape: pl.BlockSpec(shape, lambda i, c: (i,) + (0,) * (len(shape) - 1))
    return pl.pallas_call(
        functools.partial(_mlstm_kernel, L=L),
        grid=(b // bb, nc),
        in_specs=[
            col(0), col(1), col(2), col(3),
            pl.BlockSpec((bb, L, LANES), lambda i, c: (i, c, S_AG // LANES)),
            state((bb, A_HEADS, A_DK, A_DK)), state((bb, A_HEADS, A_DK)), state((bb, 1, A_HEADS)),
            pl.BlockSpec((1, LANES), lambda i, c: (0, 0)),
            pl.BlockSpec((1, A_W), lambda i, c: (0, 0)),
        ],
        out_specs=[
            pl.BlockSpec((bb, L, A_W), lambda i, c: (i, c, 0)),
            state((bb, A_HEADS, A_DK, A_DK)), state((bb, A_HEADS, A_DK)), state((bb, 1, A_HEADS)),
        ],
        out_shape=[
            jax.ShapeDtypeStruct((b, t, A_W), BF16),
            jax.ShapeDtypeStruct((b, A_HEADS, A_DK, A_DK), F32),
            jax.ShapeDtypeStruct((b, A_HEADS, A_DK), F32),
            jax.ShapeDtypeStruct((b, 1, A_HEADS), F32),
        ],
        compiler_params=_cparams("parallel", "arbitrary"),
    )(cols, cols, cols, cols, gates, c0, n0, m0, gate_bias, norm_g)


def _swa_kernel(*refs, n_kv, first_valid_fn):
    q_ref = refs[0]
    k_refs = refs[1:1 + n_kv]
    v_refs = refs[1 + n_kv:1 + 2 * n_kv]
    bias_ref, sink_ref, o_ref = refs[1 + 2 * n_kv:]
    bb, lq, _ = q_ref.shape
    cat = lambda rs, bi: jnp.concatenate([r[bi].astype(F32) for r in rs], axis=0)
    k = [cat(k_refs, bi) for bi in range(bb)]
    v = [cat(v_refs, bi) for bi in range(bb)]
    lk = k[0].shape[0]
    first_valid = first_valid_fn(pl.program_id(1))
    valid = lax.broadcasted_iota(jnp.int32, (B_GROUP * lq, lk), 1) >= first_valid
    units = [(bi, n) for bi in range(bb) for n in range(B_KV)]
    un = range(len(units))
    head = lambda n, g: pl.ds((n * B_GROUP + g) * B_HD, B_HD)

    qn = [jnp.concatenate([q_ref[bi, :, head(n, g)].astype(F32) for g in range(B_GROUP)], axis=0)
          for bi, n in units]
    s = [lax.dot_general(qn[u], k[bi][:, n * B_HD:(n + 1) * B_HD], NT, preferred_element_type=F32)
         for u, (bi, n) in enumerate(units)]
    sk = [sink_ref[n] for _, n in units]
    s = [jnp.where(valid, s[u] * (B_HD ** -0.5) + bias_ref[n], -1e30) for u, (bi, n) in enumerate(units)]
    mx = [jnp.max(s[u], axis=-1, keepdims=True) for u in un]
    mx = [jnp.maximum(mx[u], sk[u]) for u in un]
    p = [jnp.exp(s[u] - mx[u]) for u in un]
    pv = [jnp.dot(p[u].astype(v[bi].dtype), v[bi][:, n * B_HD:(n + 1) * B_HD], preferred_element_type=F32)
          for u, (bi, n) in enumerate(units)]
    psum = [jnp.sum(p[u], axis=-1, keepdims=True) for u in un]
    den = [psum[u] + jnp.exp(sk[u] - mx[u]) for u in un]
    o = [pv[u] / den[u] for u in un]
    for u, (bi, n) in enumerate(units):
        for g in range(B_GROUP):
            o_ref[bi, :, head(n, g)] = o[u][g * lq:(g + 1) * lq, :].astype(o_ref.dtype)


def _swa_prompt(cols, bias, sinks):
    b, t, _ = cols.shape
    nc = t // CHUNK
    n_kv = WIN_CHUNKS + 1
    kw = B_KV * B_HD
    bb = _batch_block(b, SWA_ROWS_PER_STEP)

    def kv_spec(col0, back):
        return pl.BlockSpec((bb, CHUNK, kw), lambda i, c: (i, jnp.maximum(c - back, 0), col0 // kw))

    backs = list(range(WIN_CHUNKS, -1, -1))
    return pl.pallas_call(
        functools.partial(_swa_kernel, n_kv=n_kv, first_valid_fn=lambda c: (WIN_CHUNKS - c) * CHUNK),
        grid=(b // bb, nc),
        in_specs=(
            [pl.BlockSpec((bb, CHUNK, B_W), lambda i, c: (i, c, S_BQ // B_W))]
            + [kv_spec(S_BK, bk) for bk in backs]
            + [kv_spec(S_BV, bk) for bk in backs]
            + [pl.BlockSpec(bias.shape, lambda i, c: (0, 0, 0)),
               pl.BlockSpec(sinks.shape, lambda i, c: (0, 0, 0))]
        ),
        out_specs=pl.BlockSpec((bb, CHUNK, B_W), lambda i, c: (i, c, 0)),
        out_shape=jax.ShapeDtypeStruct((b, t, B_W), BF16),
        compiler_params=_cparams("parallel", "parallel"),
    )(cols, *([cols] * (2 * n_kv)), bias, sinks)


def _swa_sample(cols, kf, vf, bias, sinks):
    b, t, _ = cols.shape
    lk = kf.shape[1]
    kw = B_KV * B_HD
    bb = _batch_block(b, SWA_ROWS_PER_STEP)
    return pl.pallas_call(
        functools.partial(_swa_kernel, n_kv=1, first_valid_fn=lambda c: 0),
        grid=(b // bb, 1),
        in_specs=[
            pl.BlockSpec((bb, t, B_W), lambda i, c: (i, 0, S_BQ // B_W)),
            pl.BlockSpec((bb, lk, kw), lambda i, c: (i, 0, 0)),
            pl.BlockSpec((bb, lk, kw), lambda i, c: (i, 0, 0)),
            pl.BlockSpec(bias.shape, lambda i, c: (0, 0, 0)),
            pl.BlockSpec(sinks.shape, lambda i, c: (0, 0, 0)),
        ],
        out_specs=pl.BlockSpec((bb, t, B_W), lambda i, c: (i, 0, 0)),
        out_shape=jax.ShapeDtypeStruct((b, t, B_W), BF16),
        compiler_params=_cparams("parallel", "arbitrary"),
    )(cols, kf, vf, bias, sinks)


def _rwkv_row(bi, c_ref, mu_ref, vec_ref, lora_ref, y_ref, s_ref, sh_ref, L):
    w0, a0, k_k, k_a, r_k, gn_g, gn_b = (vec_ref[i:i + 1, :] for i in range(7))
    ww2 = lora_ref[0:C_DECAY_LORA]
    wa2 = lora_ref[C_DECAY_LORA:C_DECAY_LORA + C_AAA_LORA]
    wg2 = lora_ref[C_DECAY_LORA + C_AAA_LORA:C_DECAY_LORA + C_AAA_LORA + C_GATE_LORA]
    dot = functools.partial(jnp.dot, preferred_element_type=F32)
    dot_nt = lambda a_, b_: lax.dot_general(a_, b_, NT, preferred_element_type=F32)
    dot_tn = lambda a_, b_: lax.dot_general(a_, b_, TN, preferred_element_type=F32)
    heads = range(C_HEADS)
    hsl = [slice(h * C_HD, (h + 1) * C_HD) for h in heads]
    pairs = range(C_W // LANES)
    psl = [slice(p * LANES, (p + 1) * LANES) for p in pairs]
    low = lax.broadcasted_iota(jnp.int32, (L, LANES), 1) < C_HD

    def per_head_sum(x):
        out = []
        for p in pairs:
            t = x[:, psl[p]]
            first = jnp.sum(jnp.where(low, t, 0.0), axis=-1, keepdims=True)
            second = jnp.sum(jnp.where(low, 0.0, t), axis=-1, keepdims=True)
            out.append(jnp.where(low, first, second))
        return jnp.concatenate(out, axis=1)

    cc = c_ref[bi][:, :C_COLS].astype(F32)
    row = lax.broadcasted_iota(jnp.int32, (L, C_COLS), 0)
    prev = jnp.where(row == 0, sh_ref[bi], pltpu.roll(cc, 1, axis=0))
    sh_ref[bi] = cc[L - 1:L, :]
    xm = cc + (prev - cc) * mu_ref[...]
    r = xm[:, 0:C_W]
    k = xm[:, C_W:2 * C_W]
    v = xm[:, 2 * C_W:3 * C_W]
    o1 = 3 * C_W
    wl = xm[:, o1:o1 + C_DECAY_LORA]
    al = xm[:, o1 + C_DECAY_LORA:o1 + C_DECAY_LORA + C_AAA_LORA]
    gl = xm[:, o1 + C_DECAY_LORA + C_AAA_LORA:C_COLS]
    yield
    wlog = -_softplus(-(w0 + dot(jnp.tanh(wl), ww2))) - 0.5
    lw = -jnp.exp(wlog)
    a = jax.nn.sigmoid(a0 + dot(al, wa2))
    g = dot(jax.nn.sigmoid(gl), wg2)
    yield
    cum = _cumsum_rows(lw)
    kkf = k * k_k
    kx = k * (1.0 + (a - 1.0) * k_a)
    kk_sq = per_head_sum(kkf * kkf)
    rk = per_head_sum(r * kx * r_k)
    yield
    cum_last = cum[L - 1:L, :]
    e_in = jnp.exp(cum)
    e_ex = jnp.exp(cum - lw)
    e_neg = jnp.exp(-cum)
    e_last = jnp.exp(cum_last - cum)
    w_chunk = jnp.exp(cum_last)
    kk = kkf / jnp.maximum(jnp.sqrt(kk_sq), 1e-12)
    yield
    b = kk * a
    kt = kk * e_ex
    rt = r * e_in
    bh = b * e_neg
    kh = kx * e_neg
    kl = kx * e_last
    bl = b * e_last
    yield
    lhs = [jnp.concatenate([kt[:, hs], rt[:, hs]], axis=0) for hs in hsl]
    s0 = [s_ref[bi, h] for h in heads]
    rhs = [jnp.concatenate([bh[:, hsl[h]], kh[:, hsl[h]], s0[h]], axis=0) for h in heads]
    yield
    zeros = jnp.zeros((L, C_HD), F32)
    v_h = [v[:, hs] for hs in hsl]
    v0 = [jnp.concatenate([zeros, v_h[h]], axis=0) for h in heads]
    klbl = [jnp.concatenate([kl[:, hs], bl[:, hs]], axis=0) for hs in hsl]
    yield

    gs = [dot_nt(lhs[h], rhs[h]) for h in heads]
    yield
    r2 = lax.broadcasted_iota(jnp.int32, (L, 2 * L), 0)
    c2 = lax.broadcasted_iota(jnp.int32, (L, 2 * L), 1)
    second = c2 >= L
    cc2 = jnp.where(second, c2 - L, c2)
    eye = (lax.broadcasted_iota(jnp.int32, (L, L), 0) == lax.broadcasted_iota(jnp.int32, (L, L), 1)).astype(F32)
    tb = [jnp.where(_tri(L, strict=True), gs[h][:L, :L], 0.0) for h in heads]
    tk0 = [jnp.where(second & (cc2 < r2), gs[h][:L, :2 * L], 0.0) for h in heads]
    yield
    qq = [jnp.where(cc2 <= r2, jnp.where(second, gs[h][L:, :2 * L], -gs[h][L:, :2 * L]), 0.0) for h in heads]
    ks = [gs[h][:, 2 * L:] for h in heads]
    x = [eye - tb[h] for h in heads]
    p = [dot(tb[h], tb[h]) for h in heads]
    tkv = [dot(tk0[h], v0[h]) for h in heads]
    yield
    n_sq = int(math.log2(L)) - 1
    for i in range(n_sq):
        if i + 1 < n_sq:
            xp = [dot(jnp.concatenate([x[h], p[h]], axis=0), p[h]) for h in heads]
            x = [x[h] + xp[h][:L] for h in heads]
            p = [xp[h][L:] for h in heads]
        else:
            x = [x[h] + dot(x[h], p[h]) for h in heads]
        yield
    uu = [dot(x[h], ks[h][:L] + tkv[h]) for h in heads]
    yield
    y = [ks[h][L:] + dot(qq[h], jnp.concatenate([uu[h], v_h[h]], axis=0)) for h in heads]
    ds = [dot_tn(jnp.concatenate([v_h[h], -uu[h]], axis=0), klbl[h]) for h in heads]
    yield
    for h in heads:
        s_ref[bi, h] = s0[h] * w_chunk[:, hsl[h]] + ds[h]
    mean = [jnp.mean(y[h], axis=-1, keepdims=True) for h in heads]
    yield
    yc = [y[h] - mean[h] for h in heads]
    var = [jnp.mean(jnp.square(yc[h]), axis=-1, keepdims=True) for h in heads]
    yield
    yn = [yc[h] * lax.rsqrt(var[h] + GN_EPS) for h in heads]
    yn = jnp.concatenate(yn, axis=1)
    yield
    y_ref[bi] = ((yn * gn_g + gn_b + rk * v) * g).astype(y_ref.dtype)


def _rwkv_kernel(c_ref, sh0_ref, s0_ref, mu_ref, vec_ref, lora_ref, y_ref, s_ref, sh_ref, *, L):
    @pl.when(pl.program_id(1) == 0)
    def _():
        s_ref[...] = s0_ref[...]
        sh_ref[...] = sh0_ref[...]

    rows = [_rwkv_row(bi, c_ref, mu_ref, vec_ref, lora_ref, y_ref, s_ref, sh_ref, L)
            for bi in range(c_ref.shape[0])]
    _run_skewed(rows, RWKV_ROW_SKEW)


def _rwkv(cols, shift0, s0, lp, L):
    b, t, _ = cols.shape
    nc = t // L
    state = lambda shape: pl.BlockSpec(shape, lambda i, c: (i,) + (0,) * (len(shape) - 1))
    par = lambda arr: pl.BlockSpec(arr.shape, lambda i, c: (0, 0))
    vec = jnp.concatenate([lp['c_w0'], lp['c_a0'], lp['c_k_k'], lp['c_k_a'], lp['c_r_k'], lp['c_gn_g'],
                           lp['c_gn_b'], jnp.zeros_like(lp['c_w0'])], axis=0)
    lora = jnp.concatenate([lp['c_w_w2'], lp['c_w_a2'], lp['c_w_g2']], axis=0)
    params = [lp['c_mu'], vec, lora]
    bb = _batch_block(b, RWKV_ROWS_PER_STEP)
    return pl.pallas_call(
        functools.partial(_rwkv_kernel, L=L),
        grid=(b // bb, nc),
        in_specs=[pl.BlockSpec((bb, L, COL_C_PAD), lambda i, c: (i, c, COL_C // COL_C_PAD)),
                  state((bb, 1, C_COLS)), state((bb, C_HEADS, C_HD, C_HD))] + [par(p) for p in params],
        out_specs=[pl.BlockSpec((bb, L, C_W), lambda i, c: (i, c, 0)),
                   state((bb, C_HEADS, C_HD, C_HD)), state((bb, 1, C_COLS))],
        out_shape=[jax.ShapeDtypeStruct((b, t, C_W), BF16),
                   jax.ShapeDtypeStruct((b, C_HEADS, C_HD, C_HD), F32),
                   jax.ShapeDtypeStruct((b, 1, C_COLS), F32)],
        compiler_params=_cparams("parallel", "arbitrary"),
    )(cols, shift0, s0, *params)


def _merge_kernel(x_ref, gpre_ref, wg_ref, ha_ref, hb_ref, hc_ref, wa_ref, wb_ref, wc_ref, wo_ref,
                  ng_ref, o_ref):
    d = x_ref.shape[1]
    x = x_ref[...]
    xn = (x * lax.rsqrt(jnp.mean(x * x, axis=-1, keepdims=True) + NORM_EPS) * gpre_ref[...]).astype(BF16)

    def branch(idx, h_ref, w_ref):
        gate = jax.nn.sigmoid(jnp.dot(xn, wg_ref[:, idx * d:(idx + 1) * d], preferred_element_type=F32))
        return gate * jnp.dot(h_ref[...].astype(BF16), w_ref[...], preferred_element_type=F32)

    merged = branch(0, ha_ref, wa_ref) + branch(1, hb_ref, wb_ref) + branch(2, hc_ref, wc_ref)
    mix = jnp.dot(merged.astype(BF16), wo_ref[...], preferred_element_type=F32)
    y = mix * lax.rsqrt(jnp.mean(mix * mix, axis=-1, keepdims=True) + NORM_EPS) * ng_ref[...]
    o_ref[...] = x + y


def _layer_weight(arr, layer):
    return pl.BlockSpec((None,) + arr.shape[1:], lambda *_: (layer, 0, 0), pipeline_mode=pl.Buffered(1))


def _merge(x, gpre, wg, h_a, h_b, h_c, wa, wb, wc, wo, ng, layer, tm):
    t, d = x.shape
    tok = lambda w: pl.BlockSpec((tm, w), lambda i: (i, 0))
    full = lambda arr: pl.BlockSpec(arr.shape, lambda i: (0, 0))
    lw = lambda arr: _layer_weight(arr, layer)
    return pl.pallas_call(
        _merge_kernel,
        grid=(t // tm,),
        in_specs=[tok(d), full(gpre), lw(wg), tok(A_W), tok(B_W), tok(C_W),
                  lw(wa), lw(wb), lw(wc), lw(wo), full(ng)],
        out_specs=tok(d),
        out_shape=jax.ShapeDtypeStruct((t, d), F32),
        compiler_params=_cparams("parallel"),
    )(x, gpre, wg, h_a, h_b, h_c, wa, wb, wc, wo, ng)


def _ffn_kernel(x_ref, gpre_ref, wi_ref, wo_ref, gpost_ref, o_ref, *, tf):
    x = x_ref[...]
    xn = (x * lax.rsqrt(jnp.mean(x * x, axis=-1, keepdims=True) + NORM_EPS) * gpre_ref[...]).astype(BF16)
    f = None
    for c in range(D_FF // tf):
        gate = jnp.dot(xn, wi_ref[:, c * tf:(c + 1) * tf], preferred_element_type=F32)
        up = jnp.dot(xn, wi_ref[:, D_FF + c * tf:D_FF + (c + 1) * tf], preferred_element_type=F32)
        act = (gate * jax.nn.sigmoid(gate) * up).astype(BF16)
        part = jnp.dot(act, wo_ref[c * tf:(c + 1) * tf, :], preferred_element_type=F32)
        f = part if f is None else f + part
    y = f * lax.rsqrt(jnp.mean(f * f, axis=-1, keepdims=True) + NORM_EPS) * gpost_ref[...]
    o_ref[...] = x + y


def _ffn(x, gpre, w_in, w_out, gpost, layer, tm, tf):
    t, d = x.shape
    full = lambda arr: pl.BlockSpec(arr.shape, lambda i: (0, 0))
    return pl.pallas_call(
        functools.partial(_ffn_kernel, tf=tf),
        grid=(t // tm,),
        in_specs=[pl.BlockSpec((tm, d), lambda i: (i, 0)), full(gpre), _layer_weight(w_in, layer),
                  _layer_weight(w_out, layer), full(gpost)],
        out_specs=pl.BlockSpec((tm, d), lambda i: (i, 0)),
        out_shape=jax.ShapeDtypeStruct((t, d), F32),
        compiler_params=_cparams("parallel"),
    )(x, gpre, w_in, w_out, gpost)


def _rel_bucket(rel):
    half = NUM_BUCKETS // 2
    exact = half // 2
    n = jnp.abs(rel)
    far = exact + (jnp.log(jnp.maximum(n, 1).astype(F32) / exact)
                   / math.log(REL_MAX_DIST / exact) * (half - exact)).astype(jnp.int32)
    far = jnp.minimum(far, half - 1)
    return jnp.where(rel > 0, half, 0) + jnp.where(n < exact, n, far)


def _rel_bias(table, n_q, n_before):
    rel = (jnp.arange(n_before + n_q)[None, :] - n_before) - jnp.arange(n_q)[:, None]
    onehot = (_rel_bucket(rel)[None] == jnp.arange(NUM_BUCKETS)[:, None, None]).astype(F32)
    bias = jnp.einsum('bh,bqk->hqk', table.astype(F32), onehot, precision=HI)
    return bias.reshape(B_KV, B_GROUP * n_q, n_before + n_q)


def _layer(x, st, lp, bias, L, tm):
    b, t, d = x.shape
    swa_k, swa_v, mc, mn, mm, rs, rshift = st
    x2 = x.reshape(b * t, d)
    layer = lp['layer']
    cols2, side2 = _proj_in(x2, lp['norm_mix_pre'], lp['w_in'], layer, tm, PROJ_IN_TN)
    cols = cols2.reshape(b, t, N_COLS)
    side = side2.reshape(b, t, SIDE_W)

    h_a, mc, mn, mm = _mlstm(cols, side, mc, mn, mm, lp['a_gate_bias'], lp['a_norm'], L)

    lq = bias.shape[1] // B_GROUP
    sinks = jnp.repeat(lp['b_sinks'][0], lq).reshape(B_KV, B_GROUP * lq, 1)
    keep = min(WINDOW, t)
    k_new = side[:, t - keep:, S_BK:S_BK + B_KV * B_HD]
    v_new = side[:, t - keep:, S_BV:S_BV + B_KV * B_HD]
    if swa_k is None:
        h_b = _swa_prompt(side, bias, sinks)
        swa_k, swa_v = k_new, v_new
    else:
        kf = jnp.concatenate([swa_k, k_new], axis=1)
        vf = jnp.concatenate([swa_v, v_new], axis=1)
        h_b = _swa_sample(side, kf, vf, bias, sinks)
        swa_k, swa_v = kf[:, t:], vf[:, t:]

    h_c, rs, rshift = _rwkv(cols, rshift, rs, lp, L)

    x2 = _merge(x2, lp['norm_mix_pre'], lp['w_gate'], h_a.reshape(b * t, A_W), h_b.reshape(b * t, B_W),
                h_c.reshape(b * t, C_W), lp['w_branch_a'], lp['w_branch_b'], lp['w_branch_c'], lp['w_out'],
                lp['norm_mix_post'], layer, tm)
    x2 = _ffn(x2, lp['norm_ffn_pre'], lp['w_ffn_in'], lp['w_ffn_out'], lp['norm_ffn_post'], layer, tm, 256)
    return x2.reshape(b, t, d), (swa_k, swa_v, mc, mn, mm, rs, rshift)


def _trunk(x, layer_states, layers, bias, L, tm):
    new = []
    for l in range(DEPTH):
        x, st = _layer(x, layer_states[l], layers[l], bias, L, tm)
        new.append(st)
    return x, [jnp.stack([s[i] for s in new]) for i in range(7)]


def kernel(x_prompt, x_sample, cache_swa_k, cache_swa_v, state_mlstm_c, state_mlstm_n, state_mlstm_m, state_rwkv_s, state_rwkv_shift, w_in, norm_mix_pre, norm_mix_post, norm_ffn_pre, norm_ffn_post, a_gate_bias, a_norm, rel_bias, b_sinks, c_mu, c_w0, c_w_w2, c_a0, c_w_a2, c_w_g2, c_k_k, c_k_a, c_r_k, c_gn_g, c_gn_b, w_branch_a, w_branch_b, w_branch_c, w_out, w_ffn_in, w_ffn_out):
    bp, tp, _ = x_prompt.shape
    bs, ts, _ = x_sample.shape
    kw = B_KV * B_HD
    w_in_p, w_gate_p = _split_w_in(w_in, 256)
    stacked = {
        'w_in': w_in_p, 'w_gate': w_gate_p,
        'w_branch_a': _to_bf16(w_branch_a, A_W), 'w_branch_b': _to_bf16(w_branch_b, B_W),
        'w_branch_c': _to_bf16(w_branch_c, C_W), 'w_out': _to_bf16(w_out, D_MODEL),
        'w_ffn_in': _to_bf16(w_ffn_in, D_MODEL // 2), 'w_ffn_out': _to_bf16(w_ffn_out, D_FF // 2),
    }
    gate_bias = jnp.pad(a_gate_bias, ((0, 0), (0, LANES - 2 * A_HEADS)))
    row = lambda p, l: p[l][None, :]
    layers = []
    for l in range(DEPTH):
        layers.append({
            **stacked, 'layer': l, 'norm_mix_pre': row(norm_mix_pre, l), 'norm_mix_post': row(norm_mix_post, l),
            'norm_ffn_pre': row(norm_ffn_pre, l), 'norm_ffn_post': row(norm_ffn_post, l),
            'a_gate_bias': row(gate_bias, l), 'a_norm': row(a_norm, l), 'b_sinks': row(b_sinks, l),
            'c_mu': row(c_mu, l), 'c_w0': row(c_w0, l), 'c_w_w2': c_w_w2[l], 'c_a0': row(c_a0, l),
            'c_w_a2': c_w_a2[l], 'c_w_g2': c_w_g2[l], 'c_k_k': row(c_k_k, l), 'c_k_a': row(c_k_a, l),
            'c_r_k': row(c_r_k, l), 'c_gn_g': row(c_gn_g, l), 'c_gn_b': row(c_gn_b, l),
        })

    fresh = (None, None,
             jnp.zeros((bp, A_HEADS, A_DK, A_DK), F32), jnp.zeros((bp, A_HEADS, A_DK), F32),
             jnp.zeros((bp, 1, A_HEADS), F32), jnp.zeros((bp, C_HEADS, C_HD, C_HD), F32),
             jnp.zeros((bp, 1, C_COLS), F32))
    lp_chunk = min(CHUNK, tp)
    y_prompt, p_st = _trunk(x_prompt, [fresh] * DEPTH, layers,
                            _rel_bias(rel_bias, lp_chunk, WIN_CHUNKS * CHUNK), lp_chunk, min(1024, bp * tp))

    n_before = cache_swa_k.shape[2]
    carried = [(cache_swa_k[l].reshape(bs, n_before, kw), cache_swa_v[l].reshape(bs, n_before, kw),
                state_mlstm_c[l], state_mlstm_n[l], state_mlstm_m[l][:, None, :],
                state_rwkv_s[l], state_rwkv_shift[l]) for l in range(DEPTH)]
    ls_chunk = min(CHUNK, ts)
    y_sample, s_st = _trunk(x_sample, carried, layers, _rel_bias(rel_bias, ts, n_before), ls_chunk,
                            min(1024, bs * ts))

    def finish(st, b):
        swa_k, swa_v, mc, mn, mm, rs, rshift = st
        n_rows = swa_k.shape[2]
        return (swa_k.reshape(DEPTH, b, n_rows, B_KV, B_HD), swa_v.reshape(DEPTH, b, n_rows, B_KV, B_HD),
                mc, mn, mm.reshape(DEPTH, b, A_HEADS), rs, rshift)

    return (y_prompt, y_sample) + finish(p_st, bp) + finish(s_st, bs)
```

```python
import functools
import math

import jax
import jax.numpy as jnp
from jax import lax
from jax.experimental import pallas as pl
from jax.experimental.pallas import tpu as pltpu

F32 = jnp.float32
BF16 = jnp.bfloat16

D_MODEL = 1024
DEPTH = 4
CHUNK = 64
NORM_EPS = 1e-6

A_HEADS = 4
A_DK = 128
A_W = 512
B_HEADS = 8
B_KV = 2
B_GROUP = 4
B_HD = 64
B_W = 512
WINDOW = 128
WIN_CHUNKS = 2
NUM_BUCKETS = 32
REL_MAX_DIST = 256
C_HEADS = 8
C_HD = 64
C_W = 512
C_DECAY_LORA = 64
C_AAA_LORA = 64
C_GATE_LORA = 128
C_COLS = 1792
GN_EPS = 64e-5
D_FF = 2816

COL_A = 0
COL_C = 2048
COL_C_PAD = 2048
COL_BQ = 4096
COL_BK = 4608
COL_BV = 4736
COL_AG = 4864
N_COLS = 5120
PROJ_IN_TN = 2560

LANES = 128
SIDE_COL0 = COL_BQ
SIDE_W = COL_AG + LANES - COL_BQ
S_BQ, S_BK, S_BV, S_AG = 0, COL_BK - COL_BQ, COL_BV - COL_BQ, COL_AG - COL_BQ
VMEM_LIMIT = 56 * 1024 * 1024
HI = lax.Precision.HIGHEST

NT = (((1,), (1,)), ((), ()))
TN = (((0,), (0,)), ((), ()))


MIXER_ROWS_PER_STEP = 4
RWKV_ROW_SKEW = 7
MLSTM_FIRST_STEP, MLSTM_STEP_PERIOD = 2, 4
SWA_FIRST_STEP, SWA_STEP_PERIOD = 4, 5


def _batch_block(b, want):
    return max(d for d in range(1, want + 1) if b % d == 0)


def _cparams(*sem):
    return pltpu.CompilerParams(dimension_semantics=sem, vmem_limit_bytes=VMEM_LIMIT)


def _tri(n, strict=False):
    r = lax.broadcasted_iota(jnp.int32, (n, n), 0)
    c = lax.broadcasted_iota(jnp.int32, (n, n), 1)
    return (c < r) if strict else (c <= r)


def _cumsum_rows(x):
    row = lax.broadcasted_iota(jnp.int32, x.shape, 0)
    shift = 1
    while shift < x.shape[0]:
        x = x + jnp.where(row >= shift, pltpu.roll(x, shift, axis=0), 0.0)
        shift *= 2
    return x


def _cast_kernel(w_ref, o_ref):
    o_ref[...] = w_ref[...].astype(o_ref.dtype)


def _to_bf16(w, tr):
    depth, rows, cols = w.shape
    spec = pl.BlockSpec((1, tr, cols), lambda l, i: (l, i, 0))
    return pl.pallas_call(
        _cast_kernel, grid=(depth, rows // tr), in_specs=[spec], out_specs=spec,
        out_shape=jax.ShapeDtypeStruct(w.shape, BF16), compiler_params=_cparams("parallel", "parallel"),
    )(w)


def _split_w_in_kernel(w_ref, cols_ref, gate_ref):
    a_main = 4 * A_W
    a_all = a_main + 2 * A_HEADS
    b_all = B_W + 2 * B_KV * B_HD
    c0 = a_all + b_all
    dt = cols_ref.dtype
    cols_ref[0, :, COL_A:COL_A + a_main] = w_ref[0, :, 0:a_main].astype(dt)
    cols_ref[0, :, COL_C:COL_C + C_COLS] = w_ref[0, :, c0:c0 + C_COLS].astype(dt)
    cols_ref[0, :, COL_C + C_COLS:COL_BQ] = jnp.zeros((w_ref.shape[1], COL_BQ - COL_C - C_COLS), dt)
    cols_ref[0, :, COL_BQ:COL_BQ + b_all] = w_ref[0, :, a_all:a_all + b_all].astype(dt)
    cols_ref[0, :, COL_AG:N_COLS] = jnp.zeros((w_ref.shape[1], N_COLS - COL_AG), dt)
    cols_ref[0, :, COL_AG:COL_AG + 2 * A_HEADS] = w_ref[0, :, a_main:a_all].astype(dt)
    gate_ref[0] = w_ref[0, :, c0 + C_COLS:].astype(dt)


def _split_w_in(w, tr):
    depth, d, n_in = w.shape
    n_gate = n_in - (4 * A_W + 2 * A_HEADS + B_W + 2 * B_KV * B_HD + C_COLS)
    spec = lambda n: pl.BlockSpec((1, tr, n), lambda l, i: (l, i, 0))
    return pl.pallas_call(
        _split_w_in_kernel, grid=(depth, d // tr), in_specs=[spec(n_in)], out_specs=[spec(N_COLS), spec(n_gate)],
        out_shape=[jax.ShapeDtypeStruct((depth, d, N_COLS), BF16), jax.ShapeDtypeStruct((depth, d, n_gate), BF16)],
        compiler_params=_cparams("parallel", "parallel"),
    )(w)


def _proj_in_kernel(x_ref, g_ref, w_ref, o_ref, gate_ref, xn_ref, *, tn):
    j = pl.program_id(1)

    @pl.when(j == 0)
    def _():
        x = x_ref[...]
        y = x * lax.rsqrt(jnp.mean(x * x, axis=-1, keepdims=True) + NORM_EPS) * g_ref[...]
        xn_ref[...] = y.astype(BF16)

    acc = jnp.dot(xn_ref[...], w_ref[...], preferred_element_type=F32)
    o_ref[...] = acc.astype(BF16)

    @pl.when(j == SIDE_COL0 // tn)
    def _():
        gate_ref[...] = acc[:, SIDE_COL0 % tn:SIDE_COL0 % tn + SIDE_W]


def _proj_in(x, g, w, layer, tm, tn):
    assert SIDE_COL0 // tn == (SIDE_COL0 + SIDE_W - 1) // tn
    t, d = x.shape
    n = w.shape[2]
    return pl.pallas_call(
        functools.partial(_proj_in_kernel, tn=tn),
        grid=(t // tm, n // tn),
        in_specs=[
            pl.BlockSpec((tm, d), lambda i, j: (i, 0)),
            pl.BlockSpec((1, d), lambda i, j: (0, 0)),
            pl.BlockSpec((None, d, tn), lambda i, j: (layer, 0, j)),
        ],
        out_specs=[pl.BlockSpec((tm, tn), lambda i, j: (i, j)),
                   pl.BlockSpec((tm, SIDE_W), lambda i, j: (i, 0))],
        out_shape=[jax.ShapeDtypeStruct((t, n), BF16), jax.ShapeDtypeStruct((t, SIDE_W), F32)],
        scratch_shapes=[pltpu.VMEM((tm, d), BF16)],
        compiler_params=_cparams("parallel", "arbitrary"),
    )(x, g, w)


def _run_programs(entries):
    live = list(entries)
    step = 0
    while live:
        for entry in list(live):
            prog, first, period = entry
            if step >= first and (step - first) % period == 0:
                if next(prog, StopIteration) is StopIteration:
                    live.remove(entry)
        step += 1


def _mlstm_program(a_ref, g_ref, gb_ref, ng_ref, h_ref, c_ref, n_ref, m_ref, L):
    bb = a_ref.shape[0]
    lane = lax.broadcasted_iota(jnp.int32, (L, LANES), 1)
    sel = (lax.broadcasted_iota(jnp.int32, (8, LANES), 0)
           == lax.broadcasted_iota(jnp.int32, (8, LANES), 1)).astype(F32)
    tri_f = _tri(L).astype(F32)
    z, zrow = [], []
    for bi in range(bb):
        gates = g_ref[bi] + gb_ref[...]
        lf = jnp.minimum(gates, 0.0) - jnp.log1p(jnp.exp(-jnp.abs(gates)))
        cum = jnp.dot(tri_f, lf, precision=HI, preferred_element_type=F32)
        z.append(jnp.where(lane < A_HEADS, gates, cum))
        zrow.append(lax.dot_general(sel, z[bi], NT, precision=HI, preferred_element_type=F32))
    yield
    causal = _tri(L)
    col = lambda part, h: pl.ds(part * A_W + h * A_DK, A_DK)
    hsl = [pl.ds(h * A_DK, A_DK) for h in range(A_HEADS)]
    units = [(bi, h) for bi in range(bb) for h in range(A_HEADS)]
    un = range(len(units))

    q = [a_ref[bi, :, col(0, h)].astype(F32) for bi, h in units]
    k = [a_ref[bi, :, col(1, h)].astype(F32) * (A_DK ** -0.5) for bi, h in units]
    v = [a_ref[bi, :, col(2, h)].astype(F32) for bi, h in units]
    c_mat = [c_ref[bi, h] for bi, h in units]
    n_vec = [n_ref[bi, h:h + 1, :] for bi, h in units]
    m_prev = [m_ref[bi, :, h:h + 1] for bi, h in units]
    f_col = [z[bi][:, A_HEADS + h:A_HEADS + h + 1] for bi, h in units]
    li_col = [z[bi][:, h:h + 1] for bi, h in units]
    yield
    s = [lax.dot_general(q[u], k[u], NT, preferred_element_type=F32) for u in un]
    qc = [jnp.dot(q[u], c_mat[u], preferred_element_type=F32) for u in un]
    yield
    dlog = [jnp.where(causal, f_col[u] - zrow[bi][A_HEADS + h:A_HEADS + h + 1, :] + zrow[bi][h:h + 1, :], -jnp.inf)
            for u, (bi, h) in enumerate(units)]
    dmax = [jnp.max(dlog[u], axis=-1, keepdims=True) for u in un]
    qn = [jnp.sum(q[u] * n_vec[u], axis=-1, keepdims=True) for u in un]
    yield
    m_inter = [f_col[u] + m_prev[u] for u in un]
    m_t = [jnp.maximum(m_inter[u], dmax[u]) for u in un]
    w = [jnp.exp(dlog[u] - m_t[u]) * s[u] for u in un]
    inter = [jnp.exp(m_inter[u] - m_t[u]) for u in un]
    yield
    m_new = [m_t[u][L - 1:L, :] for u in un]
    f_last = [f_col[u][L - 1:L, :] for u in un]
    kw = [jnp.exp(f_last[u] - f_col[u] + li_col[u] - m_new[u]) * k[u] for u in un]
    decay = [jnp.exp(f_last[u] + m_prev[u] - m_new[u]) for u in un]
    yield
    wv = [jnp.dot(w[u], v[u], preferred_element_type=F32) for u in un]
    kv = [lax.dot_general(kw[u], v[u], TN, preferred_element_type=F32) for u in un]
    yield
    wsum = [jnp.sum(w[u], axis=-1, keepdims=True) for u in un]
    ksum = [jnp.sum(kw[u], axis=0, keepdims=True) for u in un]
    floor = [jnp.exp(-m_t[u]) for u in un]
    yield
    den = [jnp.maximum(jnp.abs(wsum[u] + inter[u] * qn[u]), floor[u]) for u in un]
    hh = [(wv[u] + inter[u] * qc[u]) / den[u] for u in un]
    ms = [jnp.mean(hh[u] * hh[u], axis=-1, keepdims=True) for u in un]
    yield
    scale = [lax.rsqrt(ms[u] + NORM_EPS) for u in un]
    for u, (bi, h) in enumerate(units):
        gate = jax.nn.sigmoid(a_ref[bi, :, col(3, h)].astype(F32))
        h_ref[bi, :, hsl[h]] = (gate * (hh[u] * scale[u] * ng_ref[:, hsl[h]])).astype(h_ref.dtype)
    yield
    for u, (bi, h) in enumerate(units):
        c_ref[bi, h] = decay[u] * c_mat[u] + kv[u]
        n_ref[bi, h:h + 1, :] = decay[u] * n_vec[u] + ksum[u]
        m_ref[bi, :, h:h + 1] = m_new[u]


def _swa_program(q_ref, k_refs, v_refs, bias_ref, sink_ref, o_ref, first_valid):
    bb, lq, _ = q_ref.shape
    cat = lambda rs, bi: jnp.concatenate([r[bi].astype(F32) for r in rs], axis=0)
    k = [cat(k_refs, bi) for bi in range(bb)]
    v = [cat(v_refs, bi) for bi in range(bb)]
    lk = k[0].shape[0]
    valid = lax.broadcasted_iota(jnp.int32, (B_GROUP * lq, lk), 1) >= first_valid
    units = [(bi, n) for bi in range(bb) for n in range(B_KV)]
    un = range(len(units))
    head = lambda n, g: pl.ds((n * B_GROUP + g) * B_HD, B_HD)
    yield
    qn = [jnp.concatenate([q_ref[bi, :, head(n, g)].astype(F32) for g in range(B_GROUP)], axis=0)
          for bi, n in units]
    yield
    s = [lax.dot_general(qn[u], k[bi][:, n * B_HD:(n + 1) * B_HD], NT, preferred_element_type=F32)
         for u, (bi, n) in enumerate(units)]
    sk = [sink_ref[n] for _, n in units]
    yield
    s = [jnp.where(valid, s[u] * (B_HD ** -0.5) + bias_ref[n], -1e30) for u, (bi, n) in enumerate(units)]
    mx = [jnp.max(s[u], axis=-1, keepdims=True) for u in un]
    yield
    mx = [jnp.maximum(mx[u], sk[u]) for u in un]
    p = [jnp.exp(s[u] - mx[u]) for u in un]
    yield
    pv = [jnp.dot(p[u], v[bi][:, n * B_HD:(n + 1) * B_HD], preferred_element_type=F32)
          for u, (bi, n) in enumerate(units)]
    psum = [jnp.sum(p[u], axis=-1, keepdims=True) for u in un]
    yield
    den = [psum[u] + jnp.exp(sk[u] - mx[u]) for u in un]
    o = [pv[u] / den[u] for u in un]
    yield
    for u, (bi, n) in enumerate(units):
        for g in range(B_GROUP):
            o_ref[bi, :, head(n, g)] = o[u][g * lq:(g + 1) * lq, :].astype(o_ref.dtype)


def _rwkv_row(bi, c_ref, mu_ref, vec_ref, lora_ref, y_ref, s_ref, sh_ref, L):
    w0, a0, k_k, k_a, r_k, gn_g, gn_b = (vec_ref[i:i + 1, :] for i in range(7))
    ww2 = lora_ref[0:C_DECAY_LORA]
    wa2 = lora_ref[C_DECAY_LORA:C_DECAY_LORA + C_AAA_LORA]
    wg2 = lora_ref[C_DECAY_LORA + C_AAA_LORA:C_DECAY_LORA + C_AAA_LORA + C_GATE_LORA]
    dot = functools.partial(jnp.dot, preferred_element_type=F32)
    dot_nt = lambda a_, b_: lax.dot_general(a_, b_, NT, preferred_element_type=F32)
    dot_tn = lambda a_, b_: lax.dot_general(a_, b_, TN, preferred_element_type=F32)
    heads = range(C_HEADS)
    hsl = [slice(h * C_HD, (h + 1) * C_HD) for h in heads]
    pairs = range(C_W // LANES)
    psl = [slice(p * LANES, (p + 1) * LANES) for p in pairs]
    low = lax.broadcasted_iota(jnp.int32, (L, LANES), 1) < C_HD

    def per_head_sum(x):
        out = []
        for p in pairs:
            t = x[:, psl[p]]
            first = jnp.sum(jnp.where(low, t, 0.0), axis=-1, keepdims=True)
            second = jnp.sum(jnp.where(low, 0.0, t), axis=-1, keepdims=True)
            out.append(jnp.where(low, first, second))
        return jnp.concatenate(out, axis=1)

    cc = c_ref[bi][:, :C_COLS].astype(F32)
    row = lax.broadcasted_iota(jnp.int32, (L, C_COLS), 0)
    prev = jnp.where(row == 0, sh_ref[bi], pltpu.roll(cc, 1, axis=0))
    sh_ref[bi] = cc[L - 1:L, :]
    xm = cc + (prev - cc) * mu_ref[...]
    r = xm[:, 0:C_W]
    k = xm[:, C_W:2 * C_W]
    v = xm[:, 2 * C_W:3 * C_W]
    o1 = 3 * C_W
    wl = xm[:, o1:o1 + C_DECAY_LORA]
    al = xm[:, o1 + C_DECAY_LORA:o1 + C_DECAY_LORA + C_AAA_LORA]
    gl = xm[:, o1 + C_DECAY_LORA + C_AAA_LORA:C_COLS]
    yield
    lw = -math.exp(-0.5) * jax.nn.sigmoid(w0 + dot(jnp.tanh(wl), ww2))
    a = jax.nn.sigmoid(a0 + dot(al, wa2))
    g = dot(jax.nn.sigmoid(gl), wg2)
    yield
    cum = _cumsum_rows(lw)
    kkf = k * k_k
    kx = k * (1.0 + (a - 1.0) * k_a)
    kk_sq = per_head_sum(kkf * kkf)
    rk = per_head_sum(r * kx * r_k)
    yield
    cum_last = cum[L - 1:L, :]
    e_in = jnp.exp(cum)
    e_ex = jnp.exp(cum - lw)
    e_neg = jnp.exp(-cum)
    e_last = jnp.exp(cum_last - cum)
    w_chunk = jnp.exp(cum_last)
    kk = kkf * lax.rsqrt(jnp.maximum(kk_sq, 1e-24))
    yield
    b = kk * a
    kt = kk * e_ex
    rt = r * e_in
    bh = b * e_neg
    kh = kx * e_neg
    kl = kx * e_last
    bl = b * e_last
    yield
    lhs = [jnp.concatenate([kt[:, hs], rt[:, hs]], axis=0) for hs in hsl]
    s0 = [s_ref[bi, h] for h in heads]
    rhs = [jnp.concatenate([bh[:, hsl[h]], kh[:, hsl[h]], s0[h]], axis=0) for h in heads]
    yield
    zeros = jnp.zeros((L, C_HD), F32)
    v_h = [v[:, hs] for hs in hsl]
    v0 = [jnp.concatenate([zeros, v_h[h]], axis=0) for h in heads]
    klbl = [jnp.concatenate([kl[:, hs], bl[:, hs]], axis=0) for hs in hsl]
    yield

    gs = [dot_nt(lhs[h], rhs[h]) for h in heads]
    yield
    r2 = lax.broadcasted_iota(jnp.int32, (L, 2 * L), 0)
    c2 = lax.broadcasted_iota(jnp.int32, (L, 2 * L), 1)
    second = c2 >= L
    cc2 = jnp.where(second, c2 - L, c2)
    eye = (lax.broadcasted_iota(jnp.int32, (L, L), 0) == lax.broadcasted_iota(jnp.int32, (L, L), 1)).astype(F32)
    tb = [jnp.where(_tri(L, strict=True), gs[h][:L, :L], 0.0) for h in heads]
    tk0 = [jnp.where(second & (cc2 < r2), gs[h][:L, :2 * L], 0.0) for h in heads]
    yield
    qq = [jnp.where(cc2 <= r2, jnp.where(second, gs[h][L:, :2 * L], -gs[h][L:, :2 * L]), 0.0) for h in heads]
    ks = [gs[h][:, 2 * L:] for h in heads]
    x = [eye - tb[h] for h in heads]
    p = [dot(tb[h], tb[h]) for h in heads]
    tkv = [dot(tk0[h], v0[h]) for h in heads]
    yield
    n_sq = int(math.log2(L)) - 1
    for i in range(n_sq):
        if i + 1 < n_sq:
            xp = [dot(jnp.concatenate([x[h], p[h]], axis=0), p[h]) for h in heads]
            x = [x[h] + xp[h][:L] for h in heads]
            p = [xp[h][L:] for h in heads]
        else:
            x = [x[h] + dot(x[h], p[h]) for h in heads]
        yield
    uu = [dot(x[h], ks[h][:L] + tkv[h]) for h in heads]
    yield
    y = [ks[h][L:] + dot(qq[h], jnp.concatenate([uu[h], v_h[h]], axis=0)) for h in heads]
    ds = [dot_tn(jnp.concatenate([v_h[h], -uu[h]], axis=0), klbl[h]) for h in heads]
    yield
    for h in heads:
        s_ref[bi, h] = s0[h] * w_chunk[:, hsl[h]] + ds[h]
    mean = [jnp.mean(y[h], axis=-1, keepdims=True) for h in heads]
    yield
    yc = [y[h] - mean[h] for h in heads]
    var = [jnp.mean(jnp.square(yc[h]), axis=-1, keepdims=True) for h in heads]
    yield
    yn = [yc[h] * lax.rsqrt(var[h] + GN_EPS) for h in heads]
    yn = jnp.concatenate(yn, axis=1)
    yield
    y_ref[bi] = ((yn * gn_g + gn_b + rk * v) * g).astype(y_ref.dtype)


def _mixers_kernel(*refs, L, n_kv, first_valid_fn):
    a_ref, g_ref, c0_ref, n0_ref, m0_ref, gb_ref, ng_ref, q_ref = refs[:8]
    k_refs = refs[8:8 + n_kv]
    v_refs = refs[8 + n_kv:8 + 2 * n_kv]
    (bias_ref, sink_ref, cc_ref, sh0_ref, s0_ref, mu_ref, vec_ref, lora_ref,
     h_ref, c_ref, n_ref, m_ref, o_ref, y_ref, s_ref, sh_ref) = refs[8 + 2 * n_kv:]

    @pl.when(pl.program_id(1) == 0)
    def _():
        c_ref[...] = c0_ref[...]
        n_ref[...] = n0_ref[...]
        m_ref[...] = m0_ref[...]
        s_ref[...] = s0_ref[...]
        sh_ref[...] = sh0_ref[...]

    entries = [(_rwkv_row(bi, cc_ref, mu_ref, vec_ref, lora_ref, y_ref, s_ref, sh_ref, L), bi * RWKV_ROW_SKEW, 1)
               for bi in range(cc_ref.shape[0])]
    entries.append((_mlstm_program(a_ref, g_ref, gb_ref, ng_ref, h_ref, c_ref, n_ref, m_ref, L),
                    MLSTM_FIRST_STEP, MLSTM_STEP_PERIOD))
    entries.append((_swa_program(q_ref, k_refs, v_refs, bias_ref, sink_ref, o_ref,
                                 first_valid_fn(pl.program_id(1))), SWA_FIRST_STEP, SWA_STEP_PERIOD))
    _run_programs(entries)


def _mixers(cols, side, st, lp, bias, sinks, L, kf=None, vf=None):
    b, t, _ = cols.shape
    nc = t // L
    bb = _batch_block(b, MIXER_ROWS_PER_STEP)
    mc, mn, mm, rs, rshift = st
    kw = B_KV * B_HD
    state = lambda shape: pl.BlockSpec(shape, lambda i, c: (i,) + (0,) * (len(shape) - 1))
    full = lambda arr: pl.BlockSpec(arr.shape, lambda i, c: (0,) * arr.ndim)
    tok = lambda arr_w, blk: pl.BlockSpec((bb, L, arr_w), lambda i, c: (i, c, blk))
    vec = jnp.concatenate([lp['c_w0'], lp['c_a0'], lp['c_k_k'], lp['c_k_a'], lp['c_r_k'], lp['c_gn_g'],
                           lp['c_gn_b'], jnp.zeros_like(lp['c_w0'])], axis=0)
    lora = jnp.concatenate([lp['c_w_w2'], lp['c_w_a2'], lp['c_w_g2']], axis=0)

    if kf is None:
        backs = list(range(WIN_CHUNKS, -1, -1))
        kv_spec = lambda col0, back: pl.BlockSpec(
            (bb, CHUNK, kw), lambda i, c: (i, jnp.maximum(c - back, 0), col0 // kw))
        kv_specs = [kv_spec(S_BK, bk) for bk in backs] + [kv_spec(S_BV, bk) for bk in backs]
        kv_args = [side] * (2 * len(backs))
        n_kv = len(backs)
        first_valid_fn = lambda c: (WIN_CHUNKS - c) * CHUNK
    else:
        lk = kf.shape[1]
        kv_specs = [pl.BlockSpec((bb, lk, kw), lambda i, c: (i, 0, 0))] * 2
        kv_args = [kf, vf]
        n_kv = 1
        first_valid_fn = lambda c: 0

    h_a, mc, mn, mm, h_b, h_c, rs, rshift = pl.pallas_call(
        functools.partial(_mixers_kernel, L=L, n_kv=n_kv, first_valid_fn=first_valid_fn),
        grid=(b // bb, nc),
        in_specs=([tok(4 * A_W, COL_A // (4 * A_W)), tok(LANES, S_AG // LANES),
                   state((bb, A_HEADS, A_DK, A_DK)), state((bb, A_HEADS, A_DK)), state((bb, 1, A_HEADS)),
                   full(lp['a_gate_bias']), full(lp['a_norm']), tok(B_W, S_BQ // B_W)]
                  + kv_specs
                  + [full(bias), full(sinks), tok(COL_C_PAD, COL_C // COL_C_PAD),
                     state((bb, 1, C_COLS)), state((bb, C_HEADS, C_HD, C_HD)),
                     full(lp['c_mu']), full(vec), full(lora)]),
        out_specs=[tok(A_W, 0), state((bb, A_HEADS, A_DK, A_DK)), state((bb, A_HEADS, A_DK)),
                   state((bb, 1, A_HEADS)), tok(B_W, 0), tok(C_W, 0),
                   state((bb, C_HEADS, C_HD, C_HD)), state((bb, 1, C_COLS))],
        out_shape=[jax.ShapeDtypeStruct((b, t, A_W), BF16),
                   jax.ShapeDtypeStruct((b, A_HEADS, A_DK, A_DK), F32),
                   jax.ShapeDtypeStruct((b, A_HEADS, A_DK), F32),
                   jax.ShapeDtypeStruct((b, 1, A_HEADS), F32),
                   jax.ShapeDtypeStruct((b, t, B_W), BF16),
                   jax.ShapeDtypeStruct((b, t, C_W), BF16),
                   jax.ShapeDtypeStruct((b, C_HEADS, C_HD, C_HD), F32),
                   jax.ShapeDtypeStruct((b, 1, C_COLS), F32)],
        compiler_params=_cparams("parallel", "arbitrary"),
    )(cols, side, mc, mn, mm, lp['a_gate_bias'], lp['a_norm'], side, *kv_args, bias, sinks, cols, rshift, rs,
      lp['c_mu'], vec, lora)
    return h_a, h_b, h_c, (mc, mn, mm, rs, rshift)


def _merge_kernel(x_ref, gpre_ref, wg_ref, ha_ref, hb_ref, hc_ref, wa_ref, wb_ref, wc_ref, wo_ref,
                  ng_ref, o_ref):
    d = x_ref.shape[1]
    x = x_ref[...]
    xn = (x * lax.rsqrt(jnp.mean(x * x, axis=-1, keepdims=True) + NORM_EPS) * gpre_ref[...]).astype(BF16)

    def branch(idx, h_ref, w_ref):
        gate = jax.nn.sigmoid(jnp.dot(xn, wg_ref[:, idx * d:(idx + 1) * d], preferred_element_type=F32))
        return gate * jnp.dot(h_ref[...].astype(BF16), w_ref[...], preferred_element_type=F32)

    merged = branch(0, ha_ref, wa_ref) + branch(1, hb_ref, wb_ref) + branch(2, hc_ref, wc_ref)
    mix = jnp.dot(merged.astype(BF16), wo_ref[...], preferred_element_type=F32)
    y = mix * lax.rsqrt(jnp.mean(mix * mix, axis=-1, keepdims=True) + NORM_EPS) * ng_ref[...]
    o_ref[...] = x + y


def _layer_weight(arr, layer):
    return pl.BlockSpec((None,) + arr.shape[1:], lambda *_: (layer, 0, 0), pipeline_mode=pl.Buffered(1))


def _merge(x, gpre, wg, h_a, h_b, h_c, wa, wb, wc, wo, ng, layer, tm):
    t, d = x.shape
    tok = lambda w: pl.BlockSpec((tm, w), lambda i: (i, 0))
    full = lambda arr: pl.BlockSpec(arr.shape, lambda i: (0, 0))
    lw = lambda arr: _layer_weight(arr, layer)
    return pl.pallas_call(
        _merge_kernel,
        grid=(t // tm,),
        in_specs=[tok(d), full(gpre), lw(wg), tok(A_W), tok(B_W), tok(C_W),
                  lw(wa), lw(wb), lw(wc), lw(wo), full(ng)],
        out_specs=tok(d),
        out_shape=jax.ShapeDtypeStruct((t, d), F32),
        compiler_params=_cparams("parallel"),
    )(x, gpre, wg, h_a, h_b, h_c, wa, wb, wc, wo, ng)


def _ffn_kernel(x_ref, gpre_ref, wi_ref, wo_ref, gpost_ref, o_ref, *, tf):
    x = x_ref[...]
    xn = (x * lax.rsqrt(jnp.mean(x * x, axis=-1, keepdims=True) + NORM_EPS) * gpre_ref[...]).astype(BF16)
    f = None
    for c in range(D_FF // tf):
        gate = jnp.dot(xn, wi_ref[:, c * tf:(c + 1) * tf], preferred_element_type=F32)
        up = jnp.dot(xn, wi_ref[:, D_FF + c * tf:D_FF + (c + 1) * tf], preferred_element_type=F32)
        act = (gate * jax.nn.sigmoid(gate) * up).astype(BF16)
        part = jnp.dot(act, wo_ref[c * tf:(c + 1) * tf, :], preferred_element_type=F32)
        f = part if f is None else f + part
    y = f * lax.rsqrt(jnp.mean(f * f, axis=-1, keepdims=True) + NORM_EPS) * gpost_ref[...]
    o_ref[...] = x + y


def _ffn(x, gpre, w_in, w_out, gpost, layer, tm, tf):
    t, d = x.shape
    full = lambda arr: pl.BlockSpec(arr.shape, lambda i: (0, 0))
    return pl.pallas_call(
        functools.partial(_ffn_kernel, tf=tf),
        grid=(t // tm,),
        in_specs=[pl.BlockSpec((tm, d), lambda i: (i, 0)), full(gpre), _layer_weight(w_in, layer),
                  _layer_weight(w_out, layer), full(gpost)],
        out_specs=pl.BlockSpec((tm, d), lambda i: (i, 0)),
        out_shape=jax.ShapeDtypeStruct((t, d), F32),
        compiler_params=_cparams("parallel"),
    )(x, gpre, w_in, w_out, gpost)


def _rel_bucket(rel):
    half = NUM_BUCKETS // 2
    exact = half // 2
    n = jnp.abs(rel)
    far = exact + (jnp.log(jnp.maximum(n, 1).astype(F32) / exact)
                   / math.log(REL_MAX_DIST / exact) * (half - exact)).astype(jnp.int32)
    far = jnp.minimum(far, half - 1)
    return jnp.where(rel > 0, half, 0) + jnp.where(n < exact, n, far)


def _rel_bias(table, n_q, n_before):
    rel = (jnp.arange(n_before + n_q)[None, :] - n_before) - jnp.arange(n_q)[:, None]
    onehot = (_rel_bucket(rel)[None] == jnp.arange(NUM_BUCKETS)[:, None, None]).astype(F32)
    bias = jnp.einsum('bh,bqk->hqk', table.astype(F32), onehot, precision=HI)
    return bias.reshape(B_KV, B_GROUP * n_q, n_before + n_q)


def _layer(x, st, lp, bias, L, tm):
    b, t, d = x.shape
    swa_k, swa_v, mc, mn, mm, rs, rshift = st
    x2 = x.reshape(b * t, d)
    layer = lp['layer']
    cols2, side2 = _proj_in(x2, lp['norm_mix_pre'], lp['w_in'], layer, tm, PROJ_IN_TN)
    cols = cols2.reshape(b, t, N_COLS)
    side = side2.reshape(b, t, SIDE_W)

    lq = bias.shape[1] // B_GROUP
    sinks = jnp.repeat(lp['b_sinks'][0], lq).reshape(B_KV, B_GROUP * lq, 1)
    keep = min(WINDOW, t)
    k_new = side[:, t - keep:, S_BK:S_BK + B_KV * B_HD]
    v_new = side[:, t - keep:, S_BV:S_BV + B_KV * B_HD]
    if swa_k is None:
        kf = vf = None
        swa_k, swa_v = k_new, v_new
    else:
        kf = jnp.concatenate([swa_k, k_new], axis=1)
        vf = jnp.concatenate([swa_v, v_new], axis=1)
        swa_k, swa_v = kf[:, t:], vf[:, t:]
    h_a, h_b, h_c, (mc, mn, mm, rs, rshift) = _mixers(cols, side, (mc, mn, mm, rs, rshift), lp, bias, sinks, L,
                                                      kf, vf)

    x2 = _merge(x2, lp['norm_mix_pre'], lp['w_gate'], h_a.reshape(b * t, A_W), h_b.reshape(b * t, B_W),
                h_c.reshape(b * t, C_W), lp['w_branch_a'], lp['w_branch_b'], lp['w_branch_c'], lp['w_out'],
                lp['norm_mix_post'], layer, tm)
    x2 = _ffn(x2, lp['norm_ffn_pre'], lp['w_ffn_in'], lp['w_ffn_out'], lp['norm_ffn_post'], layer, tm, 256)
    return x2.reshape(b, t, d), (swa_k, swa_v, mc, mn, mm, rs, rshift)


def _trunk(x, layer_states, layers, bias, L, tm):
    new = []
    for l in range(DEPTH):
        x, st = _layer(x, layer_states[l], layers[l], bias, L, tm)
        new.append(st)
    return x, [jnp.stack([s[i] for s in new]) for i in range(7)]


def kernel(x_prompt, x_sample, cache_swa_k, cache_swa_v, state_mlstm_c, state_mlstm_n, state_mlstm_m, state_rwkv_s, state_rwkv_shift, w_in, norm_mix_pre, norm_mix_post, norm_ffn_pre, norm_ffn_post, a_gate_bias, a_norm, rel_bias, b_sinks, c_mu, c_w0, c_w_w2, c_a0, c_w_a2, c_w_g2, c_k_k, c_k_a, c_r_k, c_gn_g, c_gn_b, w_branch_a, w_branch_b, w_branch_c, w_out, w_ffn_in, w_ffn_out):
    bp, tp, _ = x_prompt.shape
    bs, ts, _ = x_sample.shape
    kw = B_KV * B_HD
    w_in_p, w_gate_p = _split_w_in(w_in, 256)
    stacked = {
        'w_in': w_in_p, 'w_gate': w_gate_p,
        'w_branch_a': _to_bf16(w_branch_a, A_W), 'w_branch_b': _to_bf16(w_branch_b, B_W),
        'w_branch_c': _to_bf16(w_branch_c, C_W), 'w_out': _to_bf16(w_out, D_MODEL),
        'w_ffn_in': _to_bf16(w_ffn_in, D_MODEL // 2), 'w_ffn_out': _to_bf16(w_ffn_out, D_FF // 2),
    }
    gate_bias = jnp.pad(a_gate_bias, ((0, 0), (0, LANES - 2 * A_HEADS)))
    row = lambda p, l: p[l][None, :]
    layers = []
    for l in range(DEPTH):
        layers.append({
            **stacked, 'layer': l, 'norm_mix_pre': row(norm_mix_pre, l), 'norm_mix_post': row(norm_mix_post, l),
            'norm_ffn_pre': row(norm_ffn_pre, l), 'norm_ffn_post': row(norm_ffn_post, l),
            'a_gate_bias': row(gate_bias, l), 'a_norm': row(a_norm, l), 'b_sinks': row(b_sinks, l),
            'c_mu': row(c_mu, l), 'c_w0': row(c_w0, l), 'c_w_w2': c_w_w2[l], 'c_a0': row(c_a0, l),
            'c_w_a2': c_w_a2[l], 'c_w_g2': c_w_g2[l], 'c_k_k': row(c_k_k, l), 'c_k_a': row(c_k_a, l),
            'c_r_k': row(c_r_k, l), 'c_gn_g': row(c_gn_g, l), 'c_gn_b': row(c_gn_b, l),
        })

    fresh = (None, None,
             jnp.zeros((bp, A_HEADS, A_DK, A_DK), F32), jnp.zeros((bp, A_HEADS, A_DK), F32),
             jnp.zeros((bp, 1, A_HEADS), F32), jnp.zeros((bp, C_HEADS, C_HD, C_HD), F32),
             jnp.zeros((bp, 1, C_COLS), F32))
    lp_chunk = min(CHUNK, tp)
    y_prompt, p_st = _trunk(x_prompt, [fresh] * DEPTH, layers,
                            _rel_bias(rel_bias, lp_chunk, WIN_CHUNKS * CHUNK), lp_chunk, min(1024, bp * tp))

    n_before = cache_swa_k.shape[2]
    carried = [(cache_swa_k[l].reshape(bs, n_before, kw), cache_swa_v[l].reshape(bs, n_before, kw),
                state_mlstm_c[l], state_mlstm_n[l], state_mlstm_m[l][:, None, :],
                state_rwkv_s[l], state_rwkv_shift[l]) for l in range(DEPTH)]
    ls_chunk = min(CHUNK, ts)
    y_sample, s_st = _trunk(x_sample, carried, layers, _rel_bias(rel_bias, ts, n_before), ls_chunk,
                            min(1024, bs * ts))

    def finish(st, b):
        swa_k, swa_v, mc, mn, mm, rs, rshift = st
        n_rows = swa_k.shape[2]
        return (swa_k.reshape(DEPTH, b, n_rows, B_KV, B_HD), swa_v.reshape(DEPTH, b, n_rows, B_KV, B_HD),
                mc, mn, mm.reshape(DEPTH, b, A_HEADS), rs, rshift)

    return (y_prompt, y_sample) + finish(p_st, bp) + finish(s_st, bs)
```

```python
import functools
import math

import jax
import jax.numpy as jnp
from jax import lax
from jax.experimental import pallas as pl
from jax.experimental.pallas import tpu as pltpu

F32 = jnp.float32
BF16 = jnp.bfloat16

D_MODEL = 1024
DEPTH = 4
CHUNK = 64
NORM_EPS = 1e-6

A_HEADS = 4
A_DK = 128
A_W = 512
B_HEADS = 8
B_KV = 2
B_GROUP = 4
B_HD = 64
B_W = 512
WINDOW = 128
WIN_CHUNKS = 2
NUM_BUCKETS = 32
REL_MAX_DIST = 256
C_HEADS = 8
C_HD = 64
C_W = 512
C_DECAY_LORA = 64
C_AAA_LORA = 64
C_GATE_LORA = 128
C_COLS = 1792
GN_EPS = 64e-5
D_FF = 2816

COL_A = 0
COL_C = 2048
COL_C_PAD = 2048
COL_BQ = 4096
COL_BK = 4608
COL_BV = 4736
COL_AG = 4864
N_COLS = 5120
PROJ_IN_TN = 2560

LANES = 128
SIDE_COL0 = COL_BQ
SIDE_W = COL_AG + LANES - COL_BQ
S_BQ, S_BK, S_BV, S_AG = 0, COL_BK - COL_BQ, COL_BV - COL_BQ, COL_AG - COL_BQ
VMEM_LIMIT = 56 * 1024 * 1024
HI = lax.Precision.HIGHEST

NT = (((1,), (1,)), ((), ()))
TN = (((0,), (0,)), ((), ()))


MIXER_ROWS_PER_STEP = 4
RWKV_ROW_SKEW = 7
MLSTM_FIRST_STEP, MLSTM_STEP_PERIOD = 2, 4
SWA_FIRST_STEP, SWA_STEP_PERIOD = 4, 5


def _batch_block(b, want):
    return max(d for d in range(1, want + 1) if b % d == 0)


def _cparams(*sem):
    return pltpu.CompilerParams(dimension_semantics=sem, vmem_limit_bytes=VMEM_LIMIT)


def _tri(n, strict=False):
    r = lax.broadcasted_iota(jnp.int32, (n, n), 0)
    c = lax.broadcasted_iota(jnp.int32, (n, n), 1)
    return (c < r) if strict else (c <= r)


def _mxu_row_sums(x, ones):
    hi = x.astype(BF16)
    lo = (x - hi.astype(F32)).astype(BF16)
    return jnp.dot(hi, ones, preferred_element_type=F32) + jnp.dot(lo, ones, preferred_element_type=F32)


def _cumsum_rows(x):
    row = lax.broadcasted_iota(jnp.int32, x.shape, 0)
    shift = 1
    while shift < x.shape[0]:
        x = x + jnp.where(row >= shift, pltpu.roll(x, shift, axis=0), 0.0)
        shift *= 2
    return x


def _cast_kernel(w_ref, o_ref):
    o_ref[...] = w_ref[...].astype(o_ref.dtype)


def _to_bf16(w, tr):
    depth, rows, cols = w.shape
    spec = pl.BlockSpec((1, tr, cols), lambda l, i: (l, i, 0))
    return pl.pallas_call(
        _cast_kernel, grid=(depth, rows // tr), in_specs=[spec], out_specs=spec,
        out_shape=jax.ShapeDtypeStruct(w.shape, BF16), compiler_params=_cparams("parallel", "parallel"),
    )(w)


def _split_w_in_kernel(w_ref, cols_ref, gate_ref):
    a_main = 4 * A_W
    a_all = a_main + 2 * A_HEADS
    b_all = B_W + 2 * B_KV * B_HD
    c0 = a_all + b_all
    dt = cols_ref.dtype
    cols_ref[0, :, COL_A:COL_A + a_main] = w_ref[0, :, 0:a_main].astype(dt)
    cols_ref[0, :, COL_C:COL_C + C_COLS] = w_ref[0, :, c0:c0 + C_COLS].astype(dt)
    cols_ref[0, :, COL_C + C_COLS:COL_BQ] = jnp.zeros((w_ref.shape[1], COL_BQ - COL_C - C_COLS), dt)
    cols_ref[0, :, COL_BQ:COL_BQ + b_all] = w_ref[0, :, a_all:a_all + b_all].astype(dt)
    cols_ref[0, :, COL_AG:N_COLS] = jnp.zeros((w_ref.shape[1], N_COLS - COL_AG), dt)
    cols_ref[0, :, COL_AG:COL_AG + 2 * A_HEADS] = w_ref[0, :, a_main:a_all].astype(dt)
    gate_ref[0] = w_ref[0, :, c0 + C_COLS:].astype(dt)


def _split_w_in(w, tr):
    depth, d, n_in = w.shape
    n_gate = n_in - (4 * A_W + 2 * A_HEADS + B_W + 2 * B_KV * B_HD + C_COLS)
    spec = lambda n: pl.BlockSpec((1, tr, n), lambda l, i: (l, i, 0))
    return pl.pallas_call(
        _split_w_in_kernel, grid=(depth, d // tr), in_specs=[spec(n_in)], out_specs=[spec(N_COLS), spec(n_gate)],
        out_shape=[jax.ShapeDtypeStruct((depth, d, N_COLS), BF16), jax.ShapeDtypeStruct((depth, d, n_gate), BF16)],
        compiler_params=_cparams("parallel", "parallel"),
    )(w)


def _proj_in_kernel(x_ref, g_ref, w_ref, o_ref, gate_ref, xn_ref, *, tn):
    j = pl.program_id(1)

    @pl.when(j == 0)
    def _():
        x = x_ref[...]
        y = x * lax.rsqrt(jnp.mean(x * x, axis=-1, keepdims=True) + NORM_EPS) * g_ref[...]
        xn_ref[...] = y.astype(BF16)

    acc = jnp.dot(xn_ref[...], w_ref[...], preferred_element_type=F32)
    o_ref[...] = acc.astype(BF16)

    @pl.when(j == SIDE_COL0 // tn)
    def _():
        gate_ref[...] = acc[:, SIDE_COL0 % tn:SIDE_COL0 % tn + SIDE_W]


def _proj_in(x, g, w, layer, tm, tn):
    assert SIDE_COL0 // tn == (SIDE_COL0 + SIDE_W - 1) // tn
    t, d = x.shape
    n = w.shape[2]
    return pl.pallas_call(
        functools.partial(_proj_in_kernel, tn=tn),
        grid=(t // tm, n // tn),
        in_specs=[
            pl.BlockSpec((tm, d), lambda i, j: (i, 0)),
            pl.BlockSpec((1, d), lambda i, j: (0, 0)),
            pl.BlockSpec((None, d, tn), lambda i, j: (layer, 0, j)),
        ],
        out_specs=[pl.BlockSpec((tm, tn), lambda i, j: (i, j)),
                   pl.BlockSpec((tm, SIDE_W), lambda i, j: (i, 0))],
        out_shape=[jax.ShapeDtypeStruct((t, n), BF16), jax.ShapeDtypeStruct((t, SIDE_W), F32)],
        scratch_shapes=[pltpu.VMEM((tm, d), BF16)],
        compiler_params=_cparams("parallel", "arbitrary"),
    )(x, g, w)


def _run_programs(entries):
    live = list(entries)
    step = 0
    while live:
        for entry in list(live):
            prog, first, period = entry
            if step >= first and (step - first) % period == 0:
                if next(prog, StopIteration) is StopIteration:
                    live.remove(entry)
        step += 1


def _mlstm_program(a_ref, g_ref, gb_ref, ng_ref, h_ref, c_ref, n_ref, m_ref, L):
    bb = a_ref.shape[0]
    lane = lax.broadcasted_iota(jnp.int32, (L, LANES), 1)
    sel = (lax.broadcasted_iota(jnp.int32, (8, LANES), 0)
           == lax.broadcasted_iota(jnp.int32, (8, LANES), 1)).astype(F32)
    tri_f = _tri(L).astype(F32)
    z, zrow = [], []
    for bi in range(bb):
        gates = g_ref[bi] + gb_ref[...]
        lf = jnp.minimum(gates, 0.0) - jnp.log1p(jnp.exp(-jnp.abs(gates)))
        cum = jnp.dot(tri_f, lf, precision=HI, preferred_element_type=F32)
        z.append(jnp.where(lane < A_HEADS, gates, cum))
        zrow.append(lax.dot_general(sel, z[bi], NT, precision=HI, preferred_element_type=F32))
    yield
    causal = _tri(L)
    col = lambda part, h: pl.ds(part * A_W + h * A_DK, A_DK)
    hsl = [pl.ds(h * A_DK, A_DK) for h in range(A_HEADS)]
    units = [(bi, h) for bi in range(bb) for h in range(A_HEADS)]
    un = range(len(units))

    q = [a_ref[bi, :, col(0, h)].astype(F32) for bi, h in units]
    k = [a_ref[bi, :, col(1, h)].astype(F32) * (A_DK ** -0.5) for bi, h in units]
    v = [a_ref[bi, :, col(2, h)].astype(F32) for bi, h in units]
    c_mat = [c_ref[bi, h] for bi, h in units]
    n_vec = [n_ref[bi, h:h + 1, :] for bi, h in units]
    m_prev = [m_ref[bi, :, h:h + 1] for bi, h in units]
    f_col = [z[bi][:, A_HEADS + h:A_HEADS + h + 1] for bi, h in units]
    li_col = [z[bi][:, h:h + 1] for bi, h in units]
    yield
    s = [lax.dot_general(q[u], k[u], NT, preferred_element_type=F32) for u in un]
    qc = [jnp.dot(q[u], c_mat[u], preferred_element_type=F32) for u in un]
    yield
    dlog = [jnp.where(causal, f_col[u] - zrow[bi][A_HEADS + h:A_HEADS + h + 1, :] + zrow[bi][h:h + 1, :], -jnp.inf)
            for u, (bi, h) in enumerate(units)]
    dmax = [jnp.max(dlog[u], axis=-1, keepdims=True) for u in un]
    qn = [jnp.sum(q[u] * n_vec[u], axis=-1, keepdims=True) for u in un]
    yield
    m_inter = [f_col[u] + m_prev[u] for u in un]
    m_t = [jnp.maximum(m_inter[u], dmax[u]) for u in un]
    w = [jnp.exp(dlog[u] - m_t[u]) * s[u] for u in un]
    inter = [jnp.exp(m_inter[u] - m_t[u]) for u in un]
    yield
    m_new = [m_t[u][L - 1:L, :] for u in un]
    f_last = [f_col[u][L - 1:L, :] for u in un]
    kw = [jnp.exp(f_last[u] - f_col[u] + li_col[u] - m_new[u]) * k[u] for u in un]
    decay = [jnp.exp(f_last[u] + m_prev[u] - m_new[u]) for u in un]
    yield
    wv = [jnp.dot(w[u], v[u], preferred_element_type=F32) for u in un]
    kv = [lax.dot_general(kw[u], v[u], TN, preferred_element_type=F32) for u in un]
    yield
    wsum = [jnp.sum(w[u], axis=-1, keepdims=True) for u in un]
    ksum = [jnp.sum(kw[u], axis=0, keepdims=True) for u in un]
    floor = [jnp.exp(-m_t[u]) for u in un]
    yield
    den = [jnp.maximum(jnp.abs(wsum[u] + inter[u] * qn[u]), floor[u]) for u in un]
    hh = [(wv[u] + inter[u] * qc[u]) / den[u] for u in un]
    ms = [jnp.mean(hh[u] * hh[u], axis=-1, keepdims=True) for u in un]
    yield
    scale = [lax.rsqrt(ms[u] + NORM_EPS) for u in un]
    for u, (bi, h) in enumerate(units):
        gate = jax.nn.sigmoid(a_ref[bi, :, col(3, h)].astype(F32))
        h_ref[bi, :, hsl[h]] = (gate * (hh[u] * scale[u] * ng_ref[:, hsl[h]])).astype(h_ref.dtype)
    yield
    for u, (bi, h) in enumerate(units):
        c_ref[bi, h] = decay[u] * c_mat[u] + kv[u]
        n_ref[bi, h:h + 1, :] = decay[u] * n_vec[u] + ksum[u]
        m_ref[bi, :, h:h + 1] = m_new[u]


def _swa_program(q_ref, k_refs, v_refs, bias_ref, sink_ref, o_ref, first_valid):
    bb, lq, _ = q_ref.shape
    cat = lambda rs, bi: jnp.concatenate([r[bi].astype(F32) for r in rs], axis=0)
    k = [cat(k_refs, bi) for bi in range(bb)]
    v = [cat(v_refs, bi) for bi in range(bb)]
    lk = k[0].shape[0]
    valid = lax.broadcasted_iota(jnp.int32, (B_GROUP * lq, lk), 1) >= first_valid
    units = [(bi, n) for bi in range(bb) for n in range(B_KV)]
    un = range(len(units))
    head = lambda n, g: pl.ds((n * B_GROUP + g) * B_HD, B_HD)
    yield
    qn = [jnp.concatenate([q_ref[bi, :, head(n, g)].astype(F32) for g in range(B_GROUP)], axis=0)
          for bi, n in units]
    yield
    s = [lax.dot_general(qn[u], k[bi][:, n * B_HD:(n + 1) * B_HD], NT, preferred_element_type=F32)
         for u, (bi, n) in enumerate(units)]
    sk = [sink_ref[n] for _, n in units]
    yield
    s = [jnp.where(valid, s[u] * (B_HD ** -0.5) + bias_ref[n], -1e30) for u, (bi, n) in enumerate(units)]
    mx = [jnp.max(s[u], axis=-1, keepdims=True) for u in un]
    yield
    mx = [jnp.maximum(mx[u], sk[u]) for u in un]
    p = [jnp.exp(s[u] - mx[u]) for u in un]
    yield
    pv = [jnp.dot(p[u], v[bi][:, n * B_HD:(n + 1) * B_HD], preferred_element_type=F32)
          for u, (bi, n) in enumerate(units)]
    ones_k = jnp.ones((lk, B_HD), F32)
    psum = [jnp.dot(p[u], ones_k, preferred_element_type=F32) for u in un]
    yield
    den = [psum[u] + jnp.exp(sk[u] - mx[u]) for u in un]
    o = [pv[u] / den[u] for u in un]
    yield
    for u, (bi, n) in enumerate(units):
        for g in range(B_GROUP):
            o_ref[bi, :, head(n, g)] = o[u][g * lq:(g + 1) * lq, :].astype(o_ref.dtype)


def _rwkv_row(bi, c_ref, mu_ref, vec_ref, lora_ref, y_ref, s_ref, sh_ref, L):
    w0, a0, k_k, k_a, r_k, gn_g, gn_b = (vec_ref[i:i + 1, :] for i in range(7))
    ww2 = lora_ref[0:C_DECAY_LORA]
    wa2 = lora_ref[C_DECAY_LORA:C_DECAY_LORA + C_AAA_LORA]
    wg2 = lora_ref[C_DECAY_LORA + C_AAA_LORA:C_DECAY_LORA + C_AAA_LORA + C_GATE_LORA]
    dot = functools.partial(jnp.dot, preferred_element_type=F32)
    dot_nt = lambda a_, b_: lax.dot_general(a_, b_, NT, preferred_element_type=F32)
    dot_tn = lambda a_, b_: lax.dot_general(a_, b_, TN, preferred_element_type=F32)
    heads = range(C_HEADS)
    hsl = [slice(h * C_HD, (h + 1) * C_HD) for h in heads]
    pairs = range(C_W // LANES)
    psl = [slice(p * LANES, (p + 1) * LANES) for p in pairs]

    upper_r = lax.broadcasted_iota(jnp.int32, (LANES, LANES), 0) >= C_HD
    upper_c = lax.broadcasted_iota(jnp.int32, (LANES, LANES), 1) >= C_HD
    ones_pair = (upper_r == upper_c).astype(BF16)
    ones_head = jnp.ones((C_HD, C_HD), BF16)

    def per_head_sum(x):
        return jnp.concatenate([_mxu_row_sums(x[:, psl[p]], ones_pair) for p in pairs], axis=1)

    cc = c_ref[bi][:, :C_COLS].astype(F32)
    row = lax.broadcasted_iota(jnp.int32, (L, C_COLS), 0)
    prev = jnp.where(row == 0, sh_ref[bi], pltpu.roll(cc, 1, axis=0))
    sh_ref[bi] = cc[L - 1:L, :]
    xm = cc + (prev - cc) * mu_ref[...]
    r = xm[:, 0:C_W]
    k = xm[:, C_W:2 * C_W]
    v = xm[:, 2 * C_W:3 * C_W]
    o1 = 3 * C_W
    wl = xm[:, o1:o1 + C_DECAY_LORA]
    al = xm[:, o1 + C_DECAY_LORA:o1 + C_DECAY_LORA + C_AAA_LORA]
    gl = xm[:, o1 + C_DECAY_LORA + C_AAA_LORA:C_COLS]
    yield
    lw = -math.exp(-0.5) * jax.nn.sigmoid(w0 + dot(jnp.tanh(wl), ww2))
    a = jax.nn.sigmoid(a0 + dot(al, wa2))
    g = dot(jax.nn.sigmoid(gl), wg2)
    yield
    cum = _cumsum_rows(lw)
    kkf = k * k_k
    kx = k * (1.0 + (a - 1.0) * k_a)
    kk_sq = per_head_sum(kkf * kkf)
    rk = per_head_sum(r * kx * r_k)
    yield
    cum_last = cum[L - 1:L, :]
    e_in = jnp.exp(cum)
    e_ex = jnp.exp(cum - lw)
    e_neg = jnp.exp(-cum)
    e_last = jnp.exp(cum_last - cum)
    w_chunk = jnp.exp(cum_last)
    kk = kkf * lax.rsqrt(jnp.maximum(kk_sq, 1e-24))
    yield
    b = kk * a
    kt = kk * e_ex
    rt = r * e_in
    bh = b * e_neg
    kh = kx * e_neg
    kl = kx * e_last
    bl = b * e_last
    yield
    lhs = [jnp.concatenate([kt[:, hs], rt[:, hs]], axis=0) for hs in hsl]
    s0 = [s_ref[bi, h] for h in heads]
    rhs = [jnp.concatenate([bh[:, hsl[h]], kh[:, hsl[h]], s0[h]], axis=0) for h in heads]
    yield
    zeros = jnp.zeros((L, C_HD), F32)
    v_h = [v[:, hs] for hs in hsl]
    v0 = [jnp.concatenate([zeros, v_h[h]], axis=0) for h in heads]
    klbl = [jnp.concatenate([kl[:, hs], bl[:, hs]], axis=0) for hs in hsl]
    yield

    gs = [dot_nt(lhs[h], rhs[h]) for h in heads]
    yield
    r2 = lax.broadcasted_iota(jnp.int32, (L, 2 * L), 0)
    c2 = lax.broadcasted_iota(jnp.int32, (L, 2 * L), 1)
    second = c2 >= L
    cc2 = jnp.where(second, c2 - L, c2)
    eye = (lax.broadcasted_iota(jnp.int32, (L, L), 0) == lax.broadcasted_iota(jnp.int32, (L, L), 1)).astype(F32)
    tb = [jnp.where(_tri(L, strict=True), gs[h][:L, :L], 0.0) for h in heads]
    tk0 = [jnp.where(second & (cc2 < r2), gs[h][:L, :2 * L], 0.0) for h in heads]
    yield
    qq = [jnp.where(cc2 <= r2, jnp.where(second, gs[h][L:, :2 * L], -gs[h][L:, :2 * L]), 0.0) for h in heads]
    ks = [gs[h][:, 2 * L:] for h in heads]
    x = [eye - tb[h] for h in heads]
    p = [dot(tb[h], tb[h]) for h in heads]
    tkv = [dot(tk0[h], v0[h]) for h in heads]
    yield
    n_sq = int(math.log2(L)) - 1
    for i in range(n_sq):
        if i + 1 < n_sq:
            xp = [dot(jnp.concatenate([x[h], p[h]], axis=0), p[h]) for h in heads]
            x = [x[h] + xp[h][:L] for h in heads]
            p = [xp[h][L:] for h in heads]
        else:
            x = [x[h] + dot(x[h], p[h]) for h in heads]
        yield
    uu = [dot(x[h], ks[h][:L] + tkv[h]) for h in heads]
    yield
    y = [ks[h][L:] + dot(qq[h], jnp.concatenate([uu[h], v_h[h]], axis=0)) for h in heads]
    ds = [dot_tn(jnp.concatenate([v_h[h], -uu[h]], axis=0), klbl[h]) for h in heads]
    yield
    for h in heads:
        s_ref[bi, h] = s0[h] * w_chunk[:, hsl[h]] + ds[h]
    mean = [_mxu_row_sums(y[h], ones_head) * (1.0 / C_HD) for h in heads]
    yield
    yc = [y[h] - mean[h] for h in heads]
    var = [_mxu_row_sums(jnp.square(yc[h]), ones_head) * (1.0 / C_HD) for h in heads]
    yield
    yn = [yc[h] * lax.rsqrt(var[h] + GN_EPS) for h in heads]
    yn = jnp.concatenate(yn, axis=1)
    yield
    y_ref[bi] = ((yn * gn_g + gn_b + rk * v) * g).astype(y_ref.dtype)


def _mixers_kernel(*refs, L, n_kv, first_valid_fn):
    a_ref, g_ref, c0_ref, n0_ref, m0_ref, gb_ref, ng_ref, q_ref = refs[:8]
    k_refs = refs[8:8 + n_kv]
    v_refs = refs[8 + n_kv:8 + 2 * n_kv]
    (bias_ref, sink_ref, cc_ref, sh0_ref, s0_ref, mu_ref, vec_ref, lora_ref,
     h_ref, c_ref, n_ref, m_ref, o_ref, y_ref, s_ref, sh_ref) = refs[8 + 2 * n_kv:]

    @pl.when(pl.program_id(1) == 0)
    def _():
        c_ref[...] = c0_ref[...]
        n_ref[...] = n0_ref[...]
        m_ref[...] = m0_ref[...]
        s_ref[...] = s0_ref[...]
        sh_ref[...] = sh0_ref[...]

    entries = [(_rwkv_row(bi, cc_ref, mu_ref, vec_ref, lora_ref, y_ref, s_ref, sh_ref, L), bi * RWKV_ROW_SKEW, 1)
               for bi in range(cc_ref.shape[0])]
    entries.append((_mlstm_program(a_ref, g_ref, gb_ref, ng_ref, h_ref, c_ref, n_ref, m_ref, L),
                    MLSTM_FIRST_STEP, MLSTM_STEP_PERIOD))
    entries.append((_swa_program(q_ref, k_refs, v_refs, bias_ref, sink_ref, o_ref,
                                 first_valid_fn(pl.program_id(1))), SWA_FIRST_STEP, SWA_STEP_PERIOD))
    _run_programs(entries)


def _mixers(cols, side, st, lp, bias, sinks, L, kf=None, vf=None):
    b, t, _ = cols.shape
    nc = t // L
    bb = _batch_block(b, MIXER_ROWS_PER_STEP)
    mc, mn, mm, rs, rshift = st
    kw = B_KV * B_HD
    state = lambda shape: pl.BlockSpec(shape, lambda i, c: (i,) + (0,) * (len(shape) - 1))
    full = lambda arr: pl.BlockSpec(arr.shape, lambda i, c: (0,) * arr.ndim)
    tok = lambda arr_w, blk: pl.BlockSpec((bb, L, arr_w), lambda i, c: (i, c, blk))
    vec = jnp.concatenate([lp['c_w0'], lp['c_a0'], lp['c_k_k'], lp['c_k_a'], lp['c_r_k'], lp['c_gn_g'],
                           lp['c_gn_b'], jnp.zeros_like(lp['c_w0'])], axis=0)
    lora = jnp.concatenate([lp['c_w_w2'], lp['c_w_a2'], lp['c_w_g2']], axis=0)

    if kf is None:
        backs = list(range(WIN_CHUNKS, -1, -1))
        kv_spec = lambda col0, back: pl.BlockSpec(
            (bb, CHUNK, kw), lambda i, c: (i, jnp.maximum(c - back, 0), col0 // kw))
        kv_specs = [kv_spec(S_BK, bk) for bk in backs] + [kv_spec(S_BV, bk) for bk in backs]
        kv_args = [side] * (2 * len(backs))
        n_kv = len(backs)
        first_valid_fn = lambda c: (WIN_CHUNKS - c) * CHUNK
    else:
        lk = kf.shape[1]
        kv_specs = [pl.BlockSpec((bb, lk, kw), lambda i, c: (i, 0, 0))] * 2
        kv_args = [kf, vf]
        n_kv = 1
        first_valid_fn = lambda c: 0

    h_a, mc, mn, mm, h_b, h_c, rs, rshift = pl.pallas_call(
        functools.partial(_mixers_kernel, L=L, n_kv=n_kv, first_valid_fn=first_valid_fn),
        grid=(b // bb, nc),
        in_specs=([tok(4 * A_W, COL_A // (4 * A_W)), tok(LANES, S_AG // LANES),
                   state((bb, A_HEADS, A_DK, A_DK)), state((bb, A_HEADS, A_DK)), state((bb, 1, A_HEADS)),
                   full(lp['a_gate_bias']), full(lp['a_norm']), tok(B_W, S_BQ // B_W)]
                  + kv_specs
                  + [full(bias), full(sinks), tok(COL_C_PAD, COL_C // COL_C_PAD),
                     state((bb, 1, C_COLS)), state((bb, C_HEADS, C_HD, C_HD)),
                     full(lp['c_mu']), full(vec), full(lora)]),
        out_specs=[tok(A_W, 0), state((bb, A_HEADS, A_DK, A_DK)), state((bb, A_HEADS, A_DK)),
                   state((bb, 1, A_HEADS)), tok(B_W, 0), tok(C_W, 0),
                   state((bb, C_HEADS, C_HD, C_HD)), state((bb, 1, C_COLS))],
        out_shape=[jax.ShapeDtypeStruct((b, t, A_W), BF16),
                   jax.ShapeDtypeStruct((b, A_HEADS, A_DK, A_DK), F32),
                   jax.ShapeDtypeStruct((b, A_HEADS, A_DK), F32),
                   jax.ShapeDtypeStruct((b, 1, A_HEADS), F32),
                   jax.ShapeDtypeStruct((b, t, B_W), BF16),
                   jax.ShapeDtypeStruct((b, t, C_W), BF16),
                   jax.ShapeDtypeStruct((b, C_HEADS, C_HD, C_HD), F32),
                   jax.ShapeDtypeStruct((b, 1, C_COLS), F32)],
        compiler_params=_cparams("parallel", "arbitrary"),
    )(cols, side, mc, mn, mm, lp['a_gate_bias'], lp['a_norm'], side, *kv_args, bias, sinks, cols, rshift, rs,
      lp['c_mu'], vec, lora)
    return h_a, h_b, h_c, (mc, mn, mm, rs, rshift)


def _merge_kernel(x_ref, gpre_ref, wg_ref, ha_ref, hb_ref, hc_ref, wa_ref, wb_ref, wc_ref, wo_ref,
                  ng_ref, o_ref):
    d = x_ref.shape[1]
    x = x_ref[...]
    xn = (x * lax.rsqrt(jnp.mean(x * x, axis=-1, keepdims=True) + NORM_EPS) * gpre_ref[...]).astype(BF16)

    def branch(idx, h_ref, w_ref):
        gate = jax.nn.sigmoid(jnp.dot(xn, wg_ref[:, idx * d:(idx + 1) * d], preferred_element_type=F32))
        return gate * jnp.dot(h_ref[...].astype(BF16), w_ref[...], preferred_element_type=F32)

    merged = branch(0, ha_ref, wa_ref) + branch(1, hb_ref, wb_ref) + branch(2, hc_ref, wc_ref)
    mix = jnp.dot(merged.astype(BF16), wo_ref[...], preferred_element_type=F32)
    y = mix * lax.rsqrt(jnp.mean(mix * mix, axis=-1, keepdims=True) + NORM_EPS) * ng_ref[...]
    o_ref[...] = x + y


def _layer_weight(arr, layer):
    return pl.BlockSpec((None,) + arr.shape[1:], lambda *_: (layer, 0, 0), pipeline_mode=pl.Buffered(1))


def _merge(x, gpre, wg, h_a, h_b, h_c, wa, wb, wc, wo, ng, layer, tm):
    t, d = x.shape
    tok = lambda w: pl.BlockSpec((tm, w), lambda i: (i, 0))
    full = lambda arr: pl.BlockSpec(arr.shape, lambda i: (0, 0))
    lw = lambda arr: _layer_weight(arr, layer)
    return pl.pallas_call(
        _merge_kernel,
        grid=(t // tm,),
        in_specs=[tok(d), full(gpre), lw(wg), tok(A_W), tok(B_W), tok(C_W),
                  lw(wa), lw(wb), lw(wc), lw(wo), full(ng)],
        out_specs=tok(d),
        out_shape=jax.ShapeDtypeStruct((t, d), F32),
        compiler_params=_cparams("parallel"),
    )(x, gpre, wg, h_a, h_b, h_c, wa, wb, wc, wo, ng)


def _ffn_kernel(x_ref, gpre_ref, wi_ref, wo_ref, gpost_ref, o_ref, *, tf):
    x = x_ref[...]
    xn = (x * lax.rsqrt(jnp.mean(x * x, axis=-1, keepdims=True) + NORM_EPS) * gpre_ref[...]).astype(BF16)
    f = None
    for c in range(D_FF // tf):
        gate = jnp.dot(xn, wi_ref[:, c * tf:(c + 1) * tf], preferred_element_type=F32)
        up = jnp.dot(xn, wi_ref[:, D_FF + c * tf:D_FF + (c + 1) * tf], preferred_element_type=F32)
        act = (gate * jax.nn.sigmoid(gate) * up).astype(BF16)
        part = jnp.dot(act, wo_ref[c * tf:(c + 1) * tf, :], preferred_element_type=F32)
        f = part if f is None else f + part
    y = f * lax.rsqrt(jnp.mean(f * f, axis=-1, keepdims=True) + NORM_EPS) * gpost_ref[...]
    o_ref[...] = x + y


def _ffn(x, gpre, w_in, w_out, gpost, layer, tm, tf):
    t, d = x.shape
    full = lambda arr: pl.BlockSpec(arr.shape, lambda i: (0, 0))
    return pl.pallas_call(
        functools.partial(_ffn_kernel, tf=tf),
        grid=(t // tm,),
        in_specs=[pl.BlockSpec((tm, d), lambda i: (i, 0)), full(gpre), _layer_weight(w_in, layer),
                  _layer_weight(w_out, layer), full(gpost)],
        out_specs=pl.BlockSpec((tm, d), lambda i: (i, 0)),
        out_shape=jax.ShapeDtypeStruct((t, d), F32),
        compiler_params=_cparams("parallel"),
    )(x, gpre, w_in, w_out, gpost)


def _rel_bucket(rel):
    half = NUM_BUCKETS // 2
    exact = half // 2
    n = jnp.abs(rel)
    far = exact + (jnp.log(jnp.maximum(n, 1).astype(F32) / exact)
                   / math.log(REL_MAX_DIST / exact) * (half - exact)).astype(jnp.int32)
    far = jnp.minimum(far, half - 1)
    return jnp.where(rel > 0, half, 0) + jnp.where(n < exact, n, far)


def _rel_bias(table, n_q, n_before):
    rel = (jnp.arange(n_before + n_q)[None, :] - n_before) - jnp.arange(n_q)[:, None]
    onehot = (_rel_bucket(rel)[None] == jnp.arange(NUM_BUCKETS)[:, None, None]).astype(F32)
    bias = jnp.einsum('bh,bqk->hqk', table.astype(F32), onehot, precision=HI)
    return bias.reshape(B_KV, B_GROUP * n_q, n_before + n_q)


def _layer(x, st, lp, bias, L, tm):
    b, t, d = x.shape
    swa_k, swa_v, mc, mn, mm, rs, rshift = st
    x2 = x.reshape(b * t, d)
    layer = lp['layer']
    cols2, side2 = _proj_in(x2, lp['norm_mix_pre'], lp['w_in'], layer, tm, PROJ_IN_TN)
    cols = cols2.reshape(b, t, N_COLS)
    side = side2.reshape(b, t, SIDE_W)

    lq = bias.shape[1] // B_GROUP
    sinks = jnp.repeat(lp['b_sinks'][0], lq).reshape(B_KV, B_GROUP * lq, 1)
    keep = min(WINDOW, t)
    k_new = side[:, t - keep:, S_BK:S_BK + B_KV * B_HD]
    v_new = side[:, t - keep:, S_BV:S_BV + B_KV * B_HD]
    if swa_k is None:
        kf = vf = None
        swa_k, swa_v = k_new, v_new
    else:
        kf = jnp.concatenate([swa_k, k_new], axis=1)
        vf = jnp.concatenate([swa_v, v_new], axis=1)
        swa_k, swa_v = kf[:, t:], vf[:, t:]
    h_a, h_b, h_c, (mc, mn, mm, rs, rshift) = _mixers(cols, side, (mc, mn, mm, rs, rshift), lp, bias, sinks, L,
                                                      kf, vf)

    x2 = _merge(x2, lp['norm_mix_pre'], lp['w_gate'], h_a.reshape(b * t, A_W), h_b.reshape(b * t, B_W),
                h_c.reshape(b * t, C_W), lp['w_branch_a'], lp['w_branch_b'], lp['w_branch_c'], lp['w_out'],
                lp['norm_mix_post'], layer, tm)
    x2 = _ffn(x2, lp['norm_ffn_pre'], lp['w_ffn_in'], lp['w_ffn_out'], lp['norm_ffn_post'], layer, tm, 256)
    return x2.reshape(b, t, d), (swa_k, swa_v, mc, mn, mm, rs, rshift)


def _trunk(x, layer_states, layers, bias, L, tm):
    new = []
    for l in range(DEPTH):
        x, st = _layer(x, layer_states[l], layers[l], bias, L, tm)
        new.append(st)
    return x, [jnp.stack([s[i] for s in new]) for i in range(7)]


def kernel(x_prompt, x_sample, cache_swa_k, cache_swa_v, state_mlstm_c, state_mlstm_n, state_mlstm_m, state_rwkv_s, state_rwkv_shift, w_in, norm_mix_pre, norm_mix_post, norm_ffn_pre, norm_ffn_post, a_gate_bias, a_norm, rel_bias, b_sinks, c_mu, c_w0, c_w_w2, c_a0, c_w_a2, c_w_g2, c_k_k, c_k_a, c_r_k, c_gn_g, c_gn_b, w_branch_a, w_branch_b, w_branch_c, w_out, w_ffn_in, w_ffn_out):
    bp, tp, _ = x_prompt.shape
    bs, ts, _ = x_sample.shape
    kw = B_KV * B_HD
    w_in_p, w_gate_p = _split_w_in(w_in, 256)
    stacked = {
        'w_in': w_in_p, 'w_gate': w_gate_p,
        'w_branch_a': _to_bf16(w_branch_a, A_W), 'w_branch_b': _to_bf16(w_branch_b, B_W),
        'w_branch_c': _to_bf16(w_branch_c, C_W), 'w_out': _to_bf16(w_out, D_MODEL),
        'w_ffn_in': _to_bf16(w_ffn_in, D_MODEL // 2), 'w_ffn_out': _to_bf16(w_ffn_out, D_FF // 2),
    }
    gate_bias = jnp.pad(a_gate_bias, ((0, 0), (0, LANES - 2 * A_HEADS)))
    row = lambda p, l: p[l][None, :]
    layers = []
    for l in range(DEPTH):
        layers.append({
            **stacked, 'layer': l, 'norm_mix_pre': row(norm_mix_pre, l), 'norm_mix_post': row(norm_mix_post, l),
            'norm_ffn_pre': row(norm_ffn_pre, l), 'norm_ffn_post': row(norm_ffn_post, l),
            'a_gate_bias': row(gate_bias, l), 'a_norm': row(a_norm, l), 'b_sinks': row(b_sinks, l),
            'c_mu': row(c_mu, l), 'c_w0': row(c_w0, l), 'c_w_w2': c_w_w2[l], 'c_a0': row(c_a0, l),
            'c_w_a2': c_w_a2[l], 'c_w_g2': c_w_g2[l], 'c_k_k': row(c_k_k, l), 'c_k_a': row(c_k_a, l),
            'c_r_k': row(c_r_k, l), 'c_gn_g': row(c_gn_g, l), 'c_gn_b': row(c_gn_b, l),
        })

    fresh = (None, None,
             jnp.zeros((bp, A_HEADS, A_DK, A_DK), F32), jnp.zeros((bp, A_HEADS, A_DK), F32),
             jnp.zeros((bp, 1, A_HEADS), F32), jnp.zeros((bp, C_HEADS, C_HD, C_HD), F32),
             jnp.zeros((bp, 1, C_COLS), F32))
    lp_chunk = min(CHUNK, tp)
    y_prompt, p_st = _trunk(x_prompt, [fresh] * DEPTH, layers,
                            _rel_bias(rel_bias, lp_chunk, WIN_CHUNKS * CHUNK), lp_chunk, min(1024, bp * tp))

    n_before = cache_swa_k.shape[2]
    carried = [(cache_swa_k[l].reshape(bs, n_before, kw), cache_swa_v[l].reshape(bs, n_before, kw),
                state_mlstm_c[l], state_mlstm_n[l], state_mlstm_m[l][:, None, :],
                state_rwkv_s[l], state_rwkv_shift[l]) for l in range(DEPTH)]
    ls_chunk = min(CHUNK, ts)
    y_sample, s_st = _trunk(x_sample, carried, layers, _rel_bias(rel_bias, ts, n_before), ls_chunk,
                            min(1024, bs * ts))

    def finish(st, b):
        swa_k, swa_v, mc, mn, mm, rs, rshift = st
        n_rows = swa_k.shape[2]
        return (swa_k.reshape(DEPTH, b, n_rows, B_KV, B_HD), swa_v.reshape(DEPTH, b, n_rows, B_KV, B_HD),
                mc, mn, mm.reshape(DEPTH, b, A_HEADS), rs, rshift)

    return (y_prompt, y_sample) + finish(p_st, bp) + finish(s_st, bs)
```

```python
import functools
import math

import jax
import jax.numpy as jnp
from jax import lax
from jax.experimental import pallas as pl
from jax.experimental.pallas import tpu as pltpu

F32 = jnp.float32
BF16 = jnp.bfloat16

D_MODEL = 1024
DEPTH = 4
CHUNK = 64
NORM_EPS = 1e-6

A_HEADS = 4
A_DK = 128
A_W = 512
B_HEADS = 8
B_KV = 2
B_GROUP = 4
B_HD = 64
B_W = 512
WINDOW = 128
WIN_CHUNKS = 2
NUM_BUCKETS = 32
REL_MAX_DIST = 256
C_HEADS = 8
C_HD = 64
C_W = 512
C_DECAY_LORA = 64
C_AAA_LORA = 64
C_GATE_LORA = 128
C_COLS = 1792
GN_EPS = 64e-5
D_FF = 2816

COL_A = 0
COL_C = 2048
COL_C_PAD = 2048
COL_BQ = 4096
COL_BK = 4608
COL_BV = 4736
COL_AG = 4864
N_COLS = 5120
PROJ_IN_TN = 2560

LANES = 128
SIDE_COL0 = COL_BQ
SIDE_W = COL_AG + LANES - COL_BQ
S_BQ, S_BK, S_BV, S_AG = 0, COL_BK - COL_BQ, COL_BV - COL_BQ, COL_AG - COL_BQ
VMEM_LIMIT = 56 * 1024 * 1024
HI = lax.Precision.HIGHEST

NT = (((1,), (1,)), ((), ()))
TN = (((0,), (0,)), ((), ()))


MIXER_ROWS_PER_STEP = 4
RWKV_ROW_SKEW = 7
MLSTM_FIRST_STEP, MLSTM_STEP_PERIOD = 2, 4
SWA_FIRST_STEP, SWA_STEP_PERIOD = 4, 5


def _batch_block(b, want):
    return max(d for d in range(1, want + 1) if b % d == 0)


def _cparams(*sem):
    return pltpu.CompilerParams(dimension_semantics=sem, vmem_limit_bytes=VMEM_LIMIT)


def _tri(n, strict=False):
    r = lax.broadcasted_iota(jnp.int32, (n, n), 0)
    c = lax.broadcasted_iota(jnp.int32, (n, n), 1)
    return (c < r) if strict else (c <= r)


def _mxu_row_sums(x, ones):
    hi = x.astype(BF16)
    lo = (x - hi.astype(F32)).astype(BF16)
    return jnp.dot(hi, ones, preferred_element_type=F32) + jnp.dot(lo, ones, preferred_element_type=F32)


def _cumsum_rows(x):
    row = lax.broadcasted_iota(jnp.int32, x.shape, 0)
    shift = 1
    while shift < x.shape[0]:
        x = x + jnp.where(row >= shift, pltpu.roll(x, shift, axis=0), 0.0)
        shift *= 2
    return x


def _cast_kernel(w_ref, o_ref):
    o_ref[...] = w_ref[...].astype(o_ref.dtype)


def _to_bf16(w, tr):
    depth, rows, cols = w.shape
    spec = pl.BlockSpec((1, tr, cols), lambda l, i: (l, i, 0))
    return pl.pallas_call(
        _cast_kernel, grid=(depth, rows // tr), in_specs=[spec], out_specs=spec,
        out_shape=jax.ShapeDtypeStruct(w.shape, BF16), compiler_params=_cparams("parallel", "parallel"),
    )(w)


def _split_w_in_kernel(w_ref, cols_ref, gate_ref):
    a_main = 4 * A_W
    a_all = a_main + 2 * A_HEADS
    b_all = B_W + 2 * B_KV * B_HD
    c0 = a_all + b_all
    dt = cols_ref.dtype
    cols_ref[0, :, COL_A:COL_A + a_main] = w_ref[0, :, 0:a_main].astype(dt)
    cols_ref[0, :, COL_C:COL_C + C_COLS] = w_ref[0, :, c0:c0 + C_COLS].astype(dt)
    cols_ref[0, :, COL_C + C_COLS:COL_BQ] = jnp.zeros((w_ref.shape[1], COL_BQ - COL_C - C_COLS), dt)
    cols_ref[0, :, COL_BQ:COL_BQ + b_all] = w_ref[0, :, a_all:a_all + b_all].astype(dt)
    cols_ref[0, :, COL_AG:N_COLS] = jnp.zeros((w_ref.shape[1], N_COLS - COL_AG), dt)
    cols_ref[0, :, COL_AG:COL_AG + 2 * A_HEADS] = w_ref[0, :, a_main:a_all].astype(dt)
    gate_ref[0] = w_ref[0, :, c0 + C_COLS:].astype(dt)


def _split_w_in(w, tr):
    depth, d, n_in = w.shape
    n_gate = n_in - (4 * A_W + 2 * A_HEADS + B_W + 2 * B_KV * B_HD + C_COLS)
    spec = lambda n: pl.BlockSpec((1, tr, n), lambda l, i: (l, i, 0))
    return pl.pallas_call(
        _split_w_in_kernel, grid=(depth, d // tr), in_specs=[spec(n_in)], out_specs=[spec(N_COLS), spec(n_gate)],
        out_shape=[jax.ShapeDtypeStruct((depth, d, N_COLS), BF16), jax.ShapeDtypeStruct((depth, d, n_gate), BF16)],
        compiler_params=_cparams("parallel", "parallel"),
    )(w)


def _proj_in_kernel(x_ref, g_ref, w_ref, o_ref, gate_ref, xn_ref, *, tn):
    j = pl.program_id(1)

    @pl.when(j == 0)
    def _():
        x = x_ref[...]
        y = x * lax.rsqrt(jnp.mean(x * x, axis=-1, keepdims=True) + NORM_EPS) * g_ref[...]
        xn_ref[...] = y.astype(BF16)

    acc = jnp.dot(xn_ref[...], w_ref[...], preferred_element_type=F32)
    o_ref[...] = acc.astype(BF16)

    @pl.when(j == SIDE_COL0 // tn)
    def _():
        gate_ref[...] = acc[:, SIDE_COL0 % tn:SIDE_COL0 % tn + SIDE_W]


def _proj_in(x, g, w, layer, tm, tn):
    assert SIDE_COL0 // tn == (SIDE_COL0 + SIDE_W - 1) // tn
    t, d = x.shape
    n = w.shape[2]
    return pl.pallas_call(
        functools.partial(_proj_in_kernel, tn=tn),
        grid=(t // tm, n // tn),
        in_specs=[
            pl.BlockSpec((tm, d), lambda i, j: (i, 0)),
            pl.BlockSpec((1, d), lambda i, j: (0, 0)),
            pl.BlockSpec((None, d, tn), lambda i, j: (layer, 0, j)),
        ],
        out_specs=[pl.BlockSpec((tm, tn), lambda i, j: (i, j)),
                   pl.BlockSpec((tm, SIDE_W), lambda i, j: (i, 0))],
        out_shape=[jax.ShapeDtypeStruct((t, n), BF16), jax.ShapeDtypeStruct((t, SIDE_W), F32)],
        scratch_shapes=[pltpu.VMEM((tm, d), BF16)],
        compiler_params=_cparams("parallel", "arbitrary"),
    )(x, g, w)


def _run_programs(entries):
    live = list(entries)
    step = 0
    while live:
        for entry in list(live):
            prog, first, period = entry
            if step >= first and (step - first) % period == 0:
                if next(prog, StopIteration) is StopIteration:
                    live.remove(entry)
        step += 1


def _mlstm_program(a_ref, g_ref, gb_ref, ng_ref, h_ref, c_ref, n_ref, m_ref, L):
    bb = a_ref.shape[0]
    lane = lax.broadcasted_iota(jnp.int32, (L, LANES), 1)
    sel = (lax.broadcasted_iota(jnp.int32, (8, LANES), 0)
           == lax.broadcasted_iota(jnp.int32, (8, LANES), 1)).astype(F32)
    tri_f = _tri(L).astype(F32)
    z, zrow = [], []
    for bi in range(bb):
        gates = g_ref[bi] + gb_ref[...]
        lf = jnp.minimum(gates, 0.0) - jnp.log1p(jnp.exp(-jnp.abs(gates)))
        cum = jnp.dot(tri_f, lf, precision=HI, preferred_element_type=F32)
        z.append(jnp.where(lane < A_HEADS, gates, cum))
        zrow.append(lax.dot_general(sel, z[bi], NT, precision=HI, preferred_element_type=F32))
    yield
    causal = _tri(L)
    col = lambda part, h: pl.ds(part * A_W + h * A_DK, A_DK)
    hsl = [pl.ds(h * A_DK, A_DK) for h in range(A_HEADS)]
    units = [(bi, h) for bi in range(bb) for h in range(A_HEADS)]
    un = range(len(units))

    q = [a_ref[bi, :, col(0, h)].astype(F32) for bi, h in units]
    k = [a_ref[bi, :, col(1, h)].astype(F32) * (A_DK ** -0.5) for bi, h in units]
    v = [a_ref[bi, :, col(2, h)].astype(F32) for bi, h in units]
    c_mat = [c_ref[bi, h] for bi, h in units]
    n_vec = [n_ref[bi, h:h + 1, :] for bi, h in units]
    m_prev = [m_ref[bi, :, h:h + 1] for bi, h in units]
    f_col = [z[bi][:, A_HEADS + h:A_HEADS + h + 1] for bi, h in units]
    li_col = [z[bi][:, h:h + 1] for bi, h in units]
    yield
    s = [lax.dot_general(q[u], k[u], NT, preferred_element_type=F32) for u in un]
    qc = [jnp.dot(q[u], c_mat[u], preferred_element_type=F32) for u in un]
    yield
    dlog = [jnp.where(causal, f_col[u] - zrow[bi][A_HEADS + h:A_HEADS + h + 1, :] + zrow[bi][h:h + 1, :], -jnp.inf)
            for u, (bi, h) in enumerate(units)]
    dmax = [jnp.max(dlog[u], axis=-1, keepdims=True) for u in un]
    qn = [jnp.sum(q[u] * n_vec[u], axis=-1, keepdims=True) for u in un]
    yield
    m_inter = [f_col[u] + m_prev[u] for u in un]
    m_t = [jnp.maximum(m_inter[u], dmax[u]) for u in un]
    w = [jnp.exp(dlog[u] - m_t[u]) * s[u] for u in un]
    inter = [jnp.exp(m_inter[u] - m_t[u]) for u in un]
    yield
    m_new = [m_t[u][L - 1:L, :] for u in un]
    f_last = [f_col[u][L - 1:L, :] for u in un]
    kw = [jnp.exp(f_last[u] - f_col[u] + li_col[u] - m_new[u]) * k[u] for u in un]
    decay = [jnp.exp(f_last[u] + m_prev[u] - m_new[u]) for u in un]
    yield
    wv = [jnp.dot(w[u], v[u], preferred_element_type=F32) for u in un]
    kv = [lax.dot_general(kw[u], v[u], TN, preferred_element_type=F32) for u in un]
    yield
    wsum = [jnp.sum(w[u], axis=-1, keepdims=True) for u in un]
    ksum = [jnp.sum(kw[u], axis=0, keepdims=True) for u in un]
    floor = [jnp.exp(-m_t[u]) for u in un]
    yield
    den = [jnp.maximum(jnp.abs(wsum[u] + inter[u] * qn[u]), floor[u]) for u in un]
    hh = [(wv[u] + inter[u] * qc[u]) / den[u] for u in un]
    ms = [jnp.mean(hh[u] * hh[u], axis=-1, keepdims=True) for u in un]
    yield
    scale = [lax.rsqrt(ms[u] + NORM_EPS) for u in un]
    for u, (bi, h) in enumerate(units):
        gate = jax.nn.sigmoid(a_ref[bi, :, col(3, h)].astype(F32))
        h_ref[bi, :, hsl[h]] = (gate * (hh[u] * scale[u] * ng_ref[:, hsl[h]])).astype(h_ref.dtype)
    yield
    for u, (bi, h) in enumerate(units):
        c_ref[bi, h] = decay[u] * c_mat[u] + kv[u]
        n_ref[bi, h:h + 1, :] = decay[u] * n_vec[u] + ksum[u]
        m_ref[bi, :, h:h + 1] = m_new[u]


def _swa_program(q_ref, k_refs, v_refs, bias_ref, sink_ref, o_ref, first_valid):
    bb, lq, _ = q_ref.shape
    cat = lambda rs, bi: jnp.concatenate([r[bi].astype(F32) for r in rs], axis=0)
    k = [cat(k_refs, bi) for bi in range(bb)]
    v = [cat(v_refs, bi) for bi in range(bb)]
    lk = k[0].shape[0]
    valid = lax.broadcasted_iota(jnp.int32, (B_GROUP * lq, lk), 1) >= first_valid
    units = [(bi, n) for bi in range(bb) for n in range(B_KV)]
    un = range(len(units))
    head = lambda n, g: pl.ds((n * B_GROUP + g) * B_HD, B_HD)
    yield
    qn = [jnp.concatenate([q_ref[bi, :, head(n, g)].astype(F32) for g in range(B_GROUP)], axis=0)
          for bi, n in units]
    yield
    s = [lax.dot_general(qn[u], k[bi][:, n * B_HD:(n + 1) * B_HD], NT, preferred_element_type=F32)
         for u, (bi, n) in enumerate(units)]
    sk = [sink_ref[n] for _, n in units]
    yield
    s = [jnp.where(valid, s[u] * (B_HD ** -0.5) + bias_ref[n], -1e30) for u, (bi, n) in enumerate(units)]
    mx = [jnp.max(s[u], axis=-1, keepdims=True) for u in un]
    yield
    mx = [jnp.maximum(mx[u], sk[u]) for u in un]
    p = [jnp.exp(s[u] - mx[u]) for u in un]
    yield
    pv = [jnp.dot(p[u], v[bi][:, n * B_HD:(n + 1) * B_HD], preferred_element_type=F32)
          for u, (bi, n) in enumerate(units)]
    ones_k = jnp.ones((lk, B_HD), F32)
    psum = [jnp.dot(p[u], ones_k, preferred_element_type=F32) for u in un]
    yield
    den = [psum[u] + jnp.exp(sk[u] - mx[u]) for u in un]
    o = [pv[u] / den[u] for u in un]
    yield
    for u, (bi, n) in enumerate(units):
        for g in range(B_GROUP):
            o_ref[bi, :, head(n, g)] = o[u][g * lq:(g + 1) * lq, :].astype(o_ref.dtype)


def _rwkv_row(bi, c_ref, mu_ref, vec_ref, lora_ref, y_ref, s_ref, sh_ref, L):
    w0, a0, k_k, k_a, r_k, gn_g, gn_b = (vec_ref[i:i + 1, :] for i in range(7))
    ww2 = lora_ref[0:C_DECAY_LORA]
    wa2 = lora_ref[C_DECAY_LORA:C_DECAY_LORA + C_AAA_LORA]
    wg2 = lora_ref[C_DECAY_LORA + C_AAA_LORA:C_DECAY_LORA + C_AAA_LORA + C_GATE_LORA]
    dot = functools.partial(jnp.dot, preferred_element_type=F32)
    dot_nt = lambda a_, b_: lax.dot_general(a_, b_, NT, preferred_element_type=F32)
    dot_tn = lambda a_, b_: lax.dot_general(a_, b_, TN, preferred_element_type=F32)
    heads = range(C_HEADS)
    pairs = range(C_W // LANES)
    psl = [slice(p * LANES, (p + 1) * LANES) for p in pairs]

    upper_r = lax.broadcasted_iota(jnp.int32, (LANES, LANES), 0) >= C_HD
    upper_c = lax.broadcasted_iota(jnp.int32, (LANES, LANES), 1) >= C_HD
    ones_pair = (upper_r == upper_c).astype(BF16)

    def per_head_sum(x):
        return jnp.concatenate([_mxu_row_sums(x[:, psl[p]], ones_pair) for p in pairs], axis=1)

    cc = c_ref[bi][:, :C_COLS].astype(F32)
    row = lax.broadcasted_iota(jnp.int32, (L, C_COLS), 0)
    prev = jnp.where(row == 0, sh_ref[bi], pltpu.roll(cc, 1, axis=0))
    sh_ref[bi] = cc[L - 1:L, :]
    xm = cc + (prev - cc) * mu_ref[...]
    r = xm[:, 0:C_W]
    k = xm[:, C_W:2 * C_W]
    v = xm[:, 2 * C_W:3 * C_W]
    o1 = 3 * C_W
    wl = xm[:, o1:o1 + C_DECAY_LORA]
    al = xm[:, o1 + C_DECAY_LORA:o1 + C_DECAY_LORA + C_AAA_LORA]
    gl = xm[:, o1 + C_DECAY_LORA + C_AAA_LORA:C_COLS]
    yield
    lw = -math.exp(-0.5) * jax.nn.sigmoid(w0 + dot(jnp.tanh(wl), ww2))
    a = jax.nn.sigmoid(a0 + dot(al, wa2))
    g = dot(jax.nn.sigmoid(gl), wg2)
    yield
    cum = _cumsum_rows(lw)
    kkf = k * k_k
    kx = k * (1.0 + (a - 1.0) * k_a)
    kk_sq = per_head_sum(kkf * kkf)
    rk = per_head_sum(r * kx * r_k)
    yield
    cum_last = cum[L - 1:L, :]
    e_in = jnp.exp(cum)
    e_ex = jnp.exp(cum - lw)
    e_neg = jnp.exp(-cum)
    e_last = jnp.exp(cum_last - cum)
    w_chunk = jnp.exp(cum_last)
    kk = kkf * lax.rsqrt(jnp.maximum(kk_sq, 1e-24))
    yield
    b = kk * a
    kt = kk * e_ex
    rt = r * e_in
    bh = b * e_neg
    kh = kx * e_neg
    kl = kx * e_last
    bl = b * e_last
    yield
    low = lax.broadcasted_iota(jnp.int32, (L, LANES), 1) < C_HD
    low2 = lax.broadcasted_iota(jnp.int32, (2 * L, LANES), 1) < C_HD
    lhs = [jnp.concatenate([kt[:, ps], rt[:, ps]], axis=0) for ps in psl]
    lhs = [jnp.concatenate([jnp.where(low2, lhs[p], 0.0), jnp.where(low2, 0.0, lhs[p])], axis=0) for p in pairs]
    s0 = [s_ref[bi, p] for p in pairs]
    rhs = [jnp.concatenate([bh[:, psl[p]], kh[:, psl[p]], s0[p]], axis=0) for p in pairs]
    yield
    zeros = jnp.zeros((L, LANES), F32)
    v_p = [v[:, ps] for ps in psl]
    v0 = [jnp.concatenate([zeros, v_p[p]], axis=0) for p in pairs]
    klbl = [jnp.concatenate([kl[:, ps], bl[:, ps]], axis=0) for ps in psl]
    yield

    gs = [dot_nt(lhs[p], rhs[p]) for p in pairs]
    yield
    r2 = lax.broadcasted_iota(jnp.int32, (L, 2 * L), 0)
    c2 = lax.broadcasted_iota(jnp.int32, (L, 2 * L), 1)
    second = c2 >= L
    cc2 = jnp.where(second, c2 - L, c2)
    eye = (lax.broadcasted_iota(jnp.int32, (L, L), 0) == lax.broadcasted_iota(jnp.int32, (L, L), 1)).astype(F32)
    top = [gs[h // 2][(h % 2) * 2 * L:(h % 2) * 2 * L + L, :2 * L] for h in heads]
    bot = [gs[h // 2][(h % 2) * 2 * L + L:(h % 2 + 1) * 2 * L, :2 * L] for h in heads]
    tb = [jnp.where(_tri(L, strict=True), top[h][:, :L], 0.0) for h in heads]
    tk0 = [jnp.where(second & (cc2 < r2), top[h], 0.0) for h in heads]
    yield
    qq = [jnp.where(cc2 <= r2, jnp.where(second, bot[h], -bot[h]), 0.0) for h in heads]
    ks = [gs[p][:2 * L, 2 * L:] + gs[p][2 * L:, 2 * L:] for p in pairs]
    x = [eye - tb[h] for h in heads]
    p_ = [dot(tb[h], tb[h]) for h in heads]
    tkv = [dot(jnp.concatenate([tk0[2 * p], tk0[2 * p + 1]], axis=0), v0[p]) for p in pairs]
    yield
    n_sq = int(math.log2(L)) - 1
    for i in range(n_sq):
        if i + 1 < n_sq:
            xp = [dot(jnp.concatenate([x[h], p_[h]], axis=0), p_[h]) for h in heads]
            x = [x[h] + xp[h][:L] for h in heads]
            p_ = [xp[h][L:] for h in heads]
        else:
            x = [x[h] + dot(x[h], p_[h]) for h in heads]
        yield
    pick = lambda m: jnp.where(low, m[:L], m[L:])
    rhs_u = [ks[p][:L] + pick(tkv[p]) for p in pairs]
    uu = [pick(dot(jnp.concatenate([x[2 * p], x[2 * p + 1]], axis=0), rhs_u[p])) for p in pairs]
    yield
    y = [ks[p][L:] + pick(dot(jnp.concatenate([qq[2 * p], qq[2 * p + 1]], axis=0),
                              jnp.concatenate([uu[p], v_p[p]], axis=0))) for p in pairs]
    ds = [dot_tn(jnp.concatenate([v_p[p], -uu[p]], axis=0), klbl[p]) for p in pairs]
    yield
    for p in pairs:
        s_ref[bi, p] = s0[p] * w_chunk[:, psl[p]] + jnp.where(ones_pair > 0, ds[p], 0.0)
    mean = [_mxu_row_sums(y[p], ones_pair) * (1.0 / C_HD) for p in pairs]
    yield
    yc = [y[p] - mean[p] for p in pairs]
    var = [_mxu_row_sums(jnp.square(yc[p]), ones_pair) * (1.0 / C_HD) for p in pairs]
    yield
    yn = jnp.concatenate([yc[p] * lax.rsqrt(var[p] + GN_EPS) for p in pairs], axis=1)
    yield
    y_ref[bi] = ((yn * gn_g + gn_b + rk * v) * g).astype(y_ref.dtype)


def _mixers_kernel(*refs, L, n_kv, first_valid_fn):
    a_ref, g_ref, c0_ref, n0_ref, m0_ref, gb_ref, ng_ref, q_ref = refs[:8]
    k_refs = refs[8:8 + n_kv]
    v_refs = refs[8 + n_kv:8 + 2 * n_kv]
    (bias_ref, sink_ref, cc_ref, sh0_ref, s0_ref, mu_ref, vec_ref, lora_ref,
     h_ref, c_ref, n_ref, m_ref, o_ref, y_ref, s_ref, sh_ref) = refs[8 + 2 * n_kv:]

    @pl.when(pl.program_id(1) == 0)
    def _():
        c_ref[...] = c0_ref[...]
        n_ref[...] = n0_ref[...]
        m_ref[...] = m0_ref[...]
        s_ref[...] = s0_ref[...]
        sh_ref[...] = sh0_ref[...]

    entries = [(_rwkv_row(bi, cc_ref, mu_ref, vec_ref, lora_ref, y_ref, s_ref, sh_ref, L), bi * RWKV_ROW_SKEW, 1)
               for bi in range(cc_ref.shape[0])]
    entries.append((_mlstm_program(a_ref, g_ref, gb_ref, ng_ref, h_ref, c_ref, n_ref, m_ref, L),
                    MLSTM_FIRST_STEP, MLSTM_STEP_PERIOD))
    entries.append((_swa_program(q_ref, k_refs, v_refs, bias_ref, sink_ref, o_ref,
                                 first_valid_fn(pl.program_id(1))), SWA_FIRST_STEP, SWA_STEP_PERIOD))
    _run_programs(entries)


def _mixers(cols, side, st, lp, bias, sinks, L, kf=None, vf=None):
    b, t, _ = cols.shape
    nc = t // L
    bb = _batch_block(b, MIXER_ROWS_PER_STEP)
    mc, mn, mm, rs, rshift = st
    kw = B_KV * B_HD
    state = lambda shape: pl.BlockSpec(shape, lambda i, c: (i,) + (0,) * (len(shape) - 1))
    full = lambda arr: pl.BlockSpec(arr.shape, lambda i, c: (0,) * arr.ndim)
    tok = lambda arr_w, blk: pl.BlockSpec((bb, L, arr_w), lambda i, c: (i, c, blk))
    vec = jnp.concatenate([lp['c_w0'], lp['c_a0'], lp['c_k_k'], lp['c_k_a'], lp['c_r_k'], lp['c_gn_g'],
                           lp['c_gn_b'], jnp.zeros_like(lp['c_w0'])], axis=0)
    lora = jnp.concatenate([lp['c_w_w2'], lp['c_w_a2'], lp['c_w_g2']], axis=0)

    if kf is None:
        backs = list(range(WIN_CHUNKS, -1, -1))
        kv_spec = lambda col0, back: pl.BlockSpec(
            (bb, CHUNK, kw), lambda i, c: (i, jnp.maximum(c - back, 0), col0 // kw))
        kv_specs = [kv_spec(S_BK, bk) for bk in backs] + [kv_spec(S_BV, bk) for bk in backs]
        kv_args = [side] * (2 * len(backs))
        n_kv = len(backs)
        first_valid_fn = lambda c: (WIN_CHUNKS - c) * CHUNK
    else:
        lk = kf.shape[1]
        kv_specs = [pl.BlockSpec((bb, lk, kw), lambda i, c: (i, 0, 0))] * 2
        kv_args = [kf, vf]
        n_kv = 1
        first_valid_fn = lambda c: 0

    h_a, mc, mn, mm, h_b, h_c, rs, rshift = pl.pallas_call(
        functools.partial(_mixers_kernel, L=L, n_kv=n_kv, first_valid_fn=first_valid_fn),
        grid=(b // bb, nc),
        in_specs=([tok(4 * A_W, COL_A // (4 * A_W)), tok(LANES, S_AG // LANES),
                   state((bb, A_HEADS, A_DK, A_DK)), state((bb, A_HEADS, A_DK)), state((bb, 1, A_HEADS)),
                   full(lp['a_gate_bias']), full(lp['a_norm']), tok(B_W, S_BQ // B_W)]
                  + kv_specs
                  + [full(bias), full(sinks), tok(COL_C_PAD, COL_C // COL_C_PAD),
                     state((bb, 1, C_COLS)), state((bb, C_HEADS // 2, 2 * C_HD, 2 * C_HD)),
                     full(lp['c_mu']), full(vec), full(lora)]),
        out_specs=[tok(A_W, 0), state((bb, A_HEADS, A_DK, A_DK)), state((bb, A_HEADS, A_DK)),
                   state((bb, 1, A_HEADS)), tok(B_W, 0), tok(C_W, 0),
                   state((bb, C_HEADS // 2, 2 * C_HD, 2 * C_HD)), state((bb, 1, C_COLS))],
        out_shape=[jax.ShapeDtypeStruct((b, t, A_W), BF16),
                   jax.ShapeDtypeStruct((b, A_HEADS, A_DK, A_DK), F32),
                   jax.ShapeDtypeStruct((b, A_HEADS, A_DK), F32),
                   jax.ShapeDtypeStruct((b, 1, A_HEADS), F32),
                   jax.ShapeDtypeStruct((b, t, B_W), BF16),
                   jax.ShapeDtypeStruct((b, t, C_W), BF16),
                   jax.ShapeDtypeStruct((b, C_HEADS // 2, 2 * C_HD, 2 * C_HD), F32),
                   jax.ShapeDtypeStruct((b, 1, C_COLS), F32)],
        compiler_params=_cparams("parallel", "arbitrary"),
    )(cols, side, mc, mn, mm, lp['a_gate_bias'], lp['a_norm'], side, *kv_args, bias, sinks, cols, rshift, rs,
      lp['c_mu'], vec, lora)
    return h_a, h_b, h_c, (mc, mn, mm, rs, rshift)


def _merge_kernel(x_ref, gpre_ref, wg_ref, ha_ref, hb_ref, hc_ref, wa_ref, wb_ref, wc_ref, wo_ref,
                  ng_ref, o_ref):
    d = x_ref.shape[1]
    x = x_ref[...]
    xn = (x * lax.rsqrt(jnp.mean(x * x, axis=-1, keepdims=True) + NORM_EPS) * gpre_ref[...]).astype(BF16)

    def branch(idx, h_ref, w_ref):
        gate = jax.nn.sigmoid(jnp.dot(xn, wg_ref[:, idx * d:(idx + 1) * d], preferred_element_type=F32))
        return gate * jnp.dot(h_ref[...].astype(BF16), w_ref[...], preferred_element_type=F32)

    merged = branch(0, ha_ref, wa_ref) + branch(1, hb_ref, wb_ref) + branch(2, hc_ref, wc_ref)
    mix = jnp.dot(merged.astype(BF16), wo_ref[...], preferred_element_type=F32)
    y = mix * lax.rsqrt(jnp.mean(mix * mix, axis=-1, keepdims=True) + NORM_EPS) * ng_ref[...]
    o_ref[...] = x + y


def _layer_weight(arr, layer):
    return pl.BlockSpec((None,) + arr.shape[1:], lambda *_: (layer, 0, 0), pipeline_mode=pl.Buffered(1))


def _merge(x, gpre, wg, h_a, h_b, h_c, wa, wb, wc, wo, ng, layer, tm):
    t, d = x.shape
    tok = lambda w: pl.BlockSpec((tm, w), lambda i: (i, 0))
    full = lambda arr: pl.BlockSpec(arr.shape, lambda i: (0, 0))
    lw = lambda arr: _layer_weight(arr, layer)
    return pl.pallas_call(
        _merge_kernel,
        grid=(t // tm,),
        in_specs=[tok(d), full(gpre), lw(wg), tok(A_W), tok(B_W), tok(C_W),
                  lw(wa), lw(wb), lw(wc), lw(wo), full(ng)],
        out_specs=tok(d),
        out_shape=jax.ShapeDtypeStruct((t, d), F32),
        compiler_params=_cparams("parallel"),
    )(x, gpre, wg, h_a, h_b, h_c, wa, wb, wc, wo, ng)


def _ffn_kernel(x_ref, gpre_ref, wi_ref, wo_ref, gpost_ref, o_ref, *, tf):
    x = x_ref[...]
    xn = (x * lax.rsqrt(jnp.mean(x * x, axis=-1, keepdims=True) + NORM_EPS) * gpre_ref[...]).astype(BF16)
    f = None
    for c in range(D_FF // tf):
        gate = jnp.dot(xn, wi_ref[:, c * tf:(c + 1) * tf], preferred_element_type=F32)
        up = jnp.dot(xn, wi_ref[:, D_FF + c * tf:D_FF + (c + 1) * tf], preferred_element_type=F32)
        act = (gate * jax.nn.sigmoid(gate) * up).astype(BF16)
        part = jnp.dot(act, wo_ref[c * tf:(c + 1) * tf, :], preferred_element_type=F32)
        f = part if f is None else f + part
    y = f * lax.rsqrt(jnp.mean(f * f, axis=-1, keepdims=True) + NORM_EPS) * gpost_ref[...]
    o_ref[...] = x + y


def _ffn(x, gpre, w_in, w_out, gpost, layer, tm, tf):
    t, d = x.shape
    full = lambda arr: pl.BlockSpec(arr.shape, lambda i: (0, 0))
    return pl.pallas_call(
        functools.partial(_ffn_kernel, tf=tf),
        grid=(t // tm,),
        in_specs=[pl.BlockSpec((tm, d), lambda i: (i, 0)), full(gpre), _layer_weight(w_in, layer),
                  _layer_weight(w_out, layer), full(gpost)],
        out_specs=pl.BlockSpec((tm, d), lambda i: (i, 0)),
        out_shape=jax.ShapeDtypeStruct((t, d), F32),
        compiler_params=_cparams("parallel"),
    )(x, gpre, w_in, w_out, gpost)


def _rel_bucket(rel):
    half = NUM_BUCKETS // 2
    exact = half // 2
    n = jnp.abs(rel)
    far = exact + (jnp.log(jnp.maximum(n, 1).astype(F32) / exact)
                   / math.log(REL_MAX_DIST / exact) * (half - exact)).astype(jnp.int32)
    far = jnp.minimum(far, half - 1)
    return jnp.where(rel > 0, half, 0) + jnp.where(n < exact, n, far)


def _rel_bias(table, n_q, n_before):
    rel = (jnp.arange(n_before + n_q)[None, :] - n_before) - jnp.arange(n_q)[:, None]
    onehot = (_rel_bucket(rel)[None] == jnp.arange(NUM_BUCKETS)[:, None, None]).astype(F32)
    bias = jnp.einsum('bh,bqk->hqk', table.astype(F32), onehot, precision=HI)
    return bias.reshape(B_KV, B_GROUP * n_q, n_before + n_q)


def _pair_states(s):
    lead = s.shape[:-3]
    s = s.reshape(lead + (C_HEADS // 2, 2, C_HD, 1, C_HD))
    same = jnp.eye(2, dtype=s.dtype).reshape(2, 1, 2, 1)
    return (s * same).reshape(lead + (C_HEADS // 2, 2 * C_HD, 2 * C_HD))


def _unpair_states(s):
    lead = s.shape[:-3]
    s = s.reshape(lead + (C_HEADS // 2, 2, C_HD, 2, C_HD))
    return jnp.stack([s[..., 0, :, 0, :], s[..., 1, :, 1, :]], axis=-3).reshape(lead + (C_HEADS, C_HD, C_HD))


def _layer(x, st, lp, bias, L, tm):
    b, t, d = x.shape
    swa_k, swa_v, mc, mn, mm, rs, rshift = st
    x2 = x.reshape(b * t, d)
    layer = lp['layer']
    cols2, side2 = _proj_in(x2, lp['norm_mix_pre'], lp['w_in'], layer, tm, PROJ_IN_TN)
    cols = cols2.reshape(b, t, N_COLS)
    side = side2.reshape(b, t, SIDE_W)

    lq = bias.shape[1] // B_GROUP
    sinks = jnp.repeat(lp['b_sinks'][0], lq).reshape(B_KV, B_GROUP * lq, 1)
    keep = min(WINDOW, t)
    k_new = side[:, t - keep:, S_BK:S_BK + B_KV * B_HD]
    v_new = side[:, t - keep:, S_BV:S_BV + B_KV * B_HD]
    if swa_k is None:
        kf = vf = None
        swa_k, swa_v = k_new, v_new
    else:
        kf = jnp.concatenate([swa_k, k_new], axis=1)
        vf = jnp.concatenate([swa_v, v_new], axis=1)
        swa_k, swa_v = kf[:, t:], vf[:, t:]
    h_a, h_b, h_c, (mc, mn, mm, rs, rshift) = _mixers(cols, side, (mc, mn, mm, rs, rshift), lp, bias, sinks, L,
                                                      kf, vf)

    x2 = _merge(x2, lp['norm_mix_pre'], lp['w_gate'], h_a.reshape(b * t, A_W), h_b.reshape(b * t, B_W),
                h_c.reshape(b * t, C_W), lp['w_branch_a'], lp['w_branch_b'], lp['w_branch_c'], lp['w_out'],
                lp['norm_mix_post'], layer, tm)
    x2 = _ffn(x2, lp['norm_ffn_pre'], lp['w_ffn_in'], lp['w_ffn_out'], lp['norm_ffn_post'], layer, tm, 256)
    return x2.reshape(b, t, d), (swa_k, swa_v, mc, mn, mm, rs, rshift)


def _trunk(x, layer_states, layers, bias, L, tm):
    new = []
    for l in range(DEPTH):
        x, st = _layer(x, layer_states[l], layers[l], bias, L, tm)
        new.append(st)
    return x, [jnp.stack([s[i] for s in new]) for i in range(7)]


def kernel(x_prompt, x_sample, cache_swa_k, cache_swa_v, state_mlstm_c, state_mlstm_n, state_mlstm_m, state_rwkv_s, state_rwkv_shift, w_in, norm_mix_pre, norm_mix_post, norm_ffn_pre, norm_ffn_post, a_gate_bias, a_norm, rel_bias, b_sinks, c_mu, c_w0, c_w_w2, c_a0, c_w_a2, c_w_g2, c_k_k, c_k_a, c_r_k, c_gn_g, c_gn_b, w_branch_a, w_branch_b, w_branch_c, w_out, w_ffn_in, w_ffn_out):
    bp, tp, _ = x_prompt.shape
    bs, ts, _ = x_sample.shape
    kw = B_KV * B_HD
    w_in_p, w_gate_p = _split_w_in(w_in, 256)
    stacked = {
        'w_in': w_in_p, 'w_gate': w_gate_p,
        'w_branch_a': _to_bf16(w_branch_a, A_W), 'w_branch_b': _to_bf16(w_branch_b, B_W),
        'w_branch_c': _to_bf16(w_branch_c, C_W), 'w_out': _to_bf16(w_out, D_MODEL),
        'w_ffn_in': _to_bf16(w_ffn_in, D_MODEL // 2), 'w_ffn_out': _to_bf16(w_ffn_out, D_FF // 2),
    }
    gate_bias = jnp.pad(a_gate_bias, ((0, 0), (0, LANES - 2 * A_HEADS)))
    row = lambda p, l: p[l][None, :]
    layers = []
    for l in range(DEPTH):
        layers.append({
            **stacked, 'layer': l, 'norm_mix_pre': row(norm_mix_pre, l), 'norm_mix_post': row(norm_mix_post, l),
            'norm_ffn_pre': row(norm_ffn_pre, l), 'norm_ffn_post': row(norm_ffn_post, l),
            'a_gate_bias': row(gate_bias, l), 'a_norm': row(a_norm, l), 'b_sinks': row(b_sinks, l),
            'c_mu': row(c_mu, l), 'c_w0': row(c_w0, l), 'c_w_w2': c_w_w2[l], 'c_a0': row(c_a0, l),
            'c_w_a2': c_w_a2[l], 'c_w_g2': c_w_g2[l], 'c_k_k': row(c_k_k, l), 'c_k_a': row(c_k_a, l),
            'c_r_k': row(c_r_k, l), 'c_gn_g': row(c_gn_g, l), 'c_gn_b': row(c_gn_b, l),
        })

    fresh = (None, None,
             jnp.zeros((bp, A_HEADS, A_DK, A_DK), F32), jnp.zeros((bp, A_HEADS, A_DK), F32),
             jnp.zeros((bp, 1, A_HEADS), F32), jnp.zeros((bp, C_HEADS // 2, 2 * C_HD, 2 * C_HD), F32),
             jnp.zeros((bp, 1, C_COLS), F32))
    lp_chunk = min(CHUNK, tp)
    y_prompt, p_st = _trunk(x_prompt, [fresh] * DEPTH, layers,
                            _rel_bias(rel_bias, lp_chunk, WIN_CHUNKS * CHUNK), lp_chunk, min(1024, bp * tp))

    n_before = cache_swa_k.shape[2]
    carried = [(cache_swa_k[l].reshape(bs, n_before, kw), cache_swa_v[l].reshape(bs, n_before, kw),
                state_mlstm_c[l], state_mlstm_n[l], state_mlstm_m[l][:, None, :],
                _pair_states(state_rwkv_s[l]), state_rwkv_shift[l]) for l in range(DEPTH)]
    ls_chunk = min(CHUNK, ts)
    y_sample, s_st = _trunk(x_sample, carried, layers, _rel_bias(rel_bias, ts, n_before), ls_chunk,
                            min(1024, bs * ts))

    def finish(st, b):
        swa_k, swa_v, mc, mn, mm, rs, rshift = st
        n_rows = swa_k.shape[2]
        return (swa_k.reshape(DEPTH, b, n_rows, B_KV, B_HD), swa_v.reshape(DEPTH, b, n_rows, B_KV, B_HD),
                mc, mn, mm.reshape(DEPTH, b, A_HEADS), _unpair_states(rs), rshift)

    return (y_prompt, y_sample) + finish(p_st, bp) + finish(s_st, bs)
```

```python
import functools
import math

import jax
import jax.numpy as jnp
from jax import lax
from jax.experimental import pallas as pl
from jax.experimental.pallas import tpu as pltpu

F32 = jnp.float32
BF16 = jnp.bfloat16

D_MODEL = 1024
DEPTH = 4
CHUNK = 64
NORM_EPS = 1e-6

A_HEADS = 4
A_DK = 128
A_W = 512
B_HEADS = 8
B_KV = 2
B_GROUP = 4
B_HD = 64
B_W = 512
WINDOW = 128
WIN_CHUNKS = 2
NUM_BUCKETS = 32
REL_MAX_DIST = 256
C_HEADS = 8
C_HD = 64
C_W = 512
C_DECAY_LORA = 64
C_AAA_LORA = 64
C_GATE_LORA = 128
C_COLS = 1792
GN_EPS = 64e-5
D_FF = 2816

COL_A = 0
COL_C = 2048
COL_C_PAD = 2048
COL_BQ = 4096
COL_BK = 4608
COL_BV = 4736
COL_AG = 4864
N_COLS = 5120
PROJ_IN_TN = 2560

LANES = 128
SIDE_COL0 = COL_BQ
SIDE_W = COL_AG + LANES - COL_BQ
S_BQ, S_BK, S_BV, S_AG = 0, COL_BK - COL_BQ, COL_BV - COL_BQ, COL_AG - COL_BQ
VMEM_LIMIT = 56 * 1024 * 1024
HI = lax.Precision.HIGHEST

NT = (((1,), (1,)), ((), ()))
TN = (((0,), (0,)), ((), ()))


MIXER_ROWS_PER_STEP = 4
RWKV_ROW_SKEW = 7
MLSTM_FIRST_STEP, MLSTM_STEP_PERIOD = 2, 4
SWA_FIRST_STEP, SWA_STEP_PERIOD = 4, 5


def _batch_block(b, want):
    return max(d for d in range(1, want + 1) if b % d == 0)


def _cparams(*sem):
    return pltpu.CompilerParams(dimension_semantics=sem, vmem_limit_bytes=VMEM_LIMIT)


def _tri(n, strict=False):
    r = lax.broadcasted_iota(jnp.int32, (n, n), 0)
    c = lax.broadcasted_iota(jnp.int32, (n, n), 1)
    return (c < r) if strict else (c <= r)


def _mxu_row_sums(x, ones):
    hi = x.astype(BF16)
    lo = (x - hi.astype(F32)).astype(BF16)
    return jnp.dot(hi, ones, preferred_element_type=F32) + jnp.dot(lo, ones, preferred_element_type=F32)


def _cumsum_rows(x):
    row = lax.broadcasted_iota(jnp.int32, x.shape, 0)
    shift = 1
    while shift < x.shape[0]:
        x = x + jnp.where(row >= shift, pltpu.roll(x, shift, axis=0), 0.0)
        shift *= 2
    return x


def _cast_kernel(w_ref, o_ref):
    o_ref[...] = w_ref[...].astype(o_ref.dtype)


def _to_bf16(w, tr):
    depth, rows, cols = w.shape
    spec = pl.BlockSpec((1, tr, cols), lambda l, i: (l, i, 0))
    return pl.pallas_call(
        _cast_kernel, grid=(depth, rows // tr), in_specs=[spec], out_specs=spec,
        out_shape=jax.ShapeDtypeStruct(w.shape, BF16), compiler_params=_cparams("parallel", "parallel"),
    )(w)


def _split_w_in_kernel(w_ref, cols_ref, gate_ref):
    a_main = 4 * A_W
    a_all = a_main + 2 * A_HEADS
    b_all = B_W + 2 * B_KV * B_HD
    c0 = a_all + b_all
    dt = cols_ref.dtype
    cols_ref[0, :, COL_A:COL_A + a_main] = w_ref[0, :, 0:a_main].astype(dt)
    cols_ref[0, :, COL_C:COL_C + C_COLS] = w_ref[0, :, c0:c0 + C_COLS].astype(dt)
    cols_ref[0, :, COL_C + C_COLS:COL_BQ] = jnp.zeros((w_ref.shape[1], COL_BQ - COL_C - C_COLS), dt)
    cols_ref[0, :, COL_BQ:COL_BQ + b_all] = w_ref[0, :, a_all:a_all + b_all].astype(dt)
    cols_ref[0, :, COL_AG:N_COLS] = jnp.zeros((w_ref.shape[1], N_COLS - COL_AG), dt)
    cols_ref[0, :, COL_AG:COL_AG + 2 * A_HEADS] = w_ref[0, :, a_main:a_all].astype(dt)
    gate_ref[0] = w_ref[0, :, c0 + C_COLS:].astype(dt)


def _split_w_in(w, tr):
    depth, d, n_in = w.shape
    n_gate = n_in - (4 * A_W + 2 * A_HEADS + B_W + 2 * B_KV * B_HD + C_COLS)
    spec = lambda n: pl.BlockSpec((1, tr, n), lambda l, i: (l, i, 0))
    return pl.pallas_call(
        _split_w_in_kernel, grid=(depth, d // tr), in_specs=[spec(n_in)], out_specs=[spec(N_COLS), spec(n_gate)],
        out_shape=[jax.ShapeDtypeStruct((depth, d, N_COLS), BF16), jax.ShapeDtypeStruct((depth, d, n_gate), BF16)],
        compiler_params=_cparams("parallel", "parallel"),
    )(w)


def _proj_in_kernel(x_ref, g_ref, w_ref, o_ref, gate_ref, xn_ref, *, tn):
    j = pl.program_id(1)

    @pl.when(j == 0)
    def _():
        x = x_ref[...]
        y = x * lax.rsqrt(jnp.mean(x * x, axis=-1, keepdims=True) + NORM_EPS) * g_ref[...]
        xn_ref[...] = y.astype(BF16)

    acc = jnp.dot(xn_ref[...], w_ref[...], preferred_element_type=F32)
    o_ref[...] = acc.astype(BF16)

    @pl.when(j == SIDE_COL0 // tn)
    def _():
        gate_ref[...] = acc[:, SIDE_COL0 % tn:SIDE_COL0 % tn + SIDE_W]


def _proj_in(x, g, w, layer, tm, tn):
    assert SIDE_COL0 // tn == (SIDE_COL0 + SIDE_W - 1) // tn
    t, d = x.shape
    n = w.shape[2]
    return pl.pallas_call(
        functools.partial(_proj_in_kernel, tn=tn),
        grid=(t // tm, n // tn),
        in_specs=[
            pl.BlockSpec((tm, d), lambda i, j: (i, 0)),
            pl.BlockSpec((1, d), lambda i, j: (0, 0)),
            pl.BlockSpec((None, d, tn), lambda i, j: (layer, 0, j)),
        ],
        out_specs=[pl.BlockSpec((tm, tn), lambda i, j: (i, j)),
                   pl.BlockSpec((tm, SIDE_W), lambda i, j: (i, 0))],
        out_shape=[jax.ShapeDtypeStruct((t, n), BF16), jax.ShapeDtypeStruct((t, SIDE_W), F32)],
        scratch_shapes=[pltpu.VMEM((tm, d), BF16)],
        compiler_params=_cparams("parallel", "arbitrary"),
    )(x, g, w)


def _run_programs(entries):
    live = list(entries)
    step = 0
    while live:
        for entry in list(live):
            prog, first, period = entry
            if step >= first and (step - first) % period == 0:
                if next(prog, StopIteration) is StopIteration:
                    live.remove(entry)
        step += 1


def _mlstm_program(a_ref, g_ref, gb_ref, ng_ref, h_ref, c_ref, n_ref, m_ref, L):
    bb = a_ref.shape[0]
    lane = lax.broadcasted_iota(jnp.int32, (L, LANES), 1)
    sel = (lax.broadcasted_iota(jnp.int32, (8, LANES), 0)
           == lax.broadcasted_iota(jnp.int32, (8, LANES), 1)).astype(F32)
    tri_f = _tri(L).astype(F32)
    z, zrow = [], []
    for bi in range(bb):
        gates = g_ref[bi] + gb_ref[...]
        lf = jnp.minimum(gates, 0.0) - jnp.log1p(jnp.exp(-jnp.abs(gates)))
        cum = jnp.dot(tri_f, lf, precision=HI, preferred_element_type=F32)
        z.append(jnp.where(lane < A_HEADS, gates, cum))
        zrow.append(lax.dot_general(sel, z[bi], NT, precision=HI, preferred_element_type=F32))
    yield
    causal = _tri(L)
    col = lambda part, h: pl.ds(part * A_W + h * A_DK, A_DK)
    hsl = [pl.ds(h * A_DK, A_DK) for h in range(A_HEADS)]
    units = [(bi, h) for bi in range(bb) for h in range(A_HEADS)]
    un = range(len(units))

    q = [a_ref[bi, :, col(0, h)].astype(F32) for bi, h in units]
    k = [a_ref[bi, :, col(1, h)].astype(F32) * (A_DK ** -0.5) for bi, h in units]
    v = [a_ref[bi, :, col(2, h)].astype(F32) for bi, h in units]
    c_mat = [c_ref[bi, h] for bi, h in units]
    n_vec = [n_ref[bi, h:h + 1, :] for bi, h in units]
    m_prev = [m_ref[bi, :, h:h + 1] for bi, h in units]
    f_col = [z[bi][:, A_HEADS + h:A_HEADS + h + 1] for bi, h in units]
    li_col = [z[bi][:, h:h + 1] for bi, h in units]
    yield
    s = [lax.dot_general(q[u], k[u], NT, preferred_element_type=F32) for u in un]
    qc = [jnp.dot(q[u], c_mat[u], preferred_element_type=F32) for u in un]
    yield
    dlog = [jnp.where(causal, f_col[u] - zrow[bi][A_HEADS + h:A_HEADS + h + 1, :] + zrow[bi][h:h + 1, :], -jnp.inf)
            for u, (bi, h) in enumerate(units)]
    dmax = [jnp.max(dlog[u], axis=-1, keepdims=True) for u in un]
    qn = [jnp.sum(q[u] * n_vec[u], axis=-1, keepdims=True) for u in un]
    yield
    m_inter = [f_col[u] + m_prev[u] for u in un]
    m_t = [jnp.maximum(m_inter[u], dmax[u]) for u in un]
    w = [jnp.exp(dlog[u] - m_t[u]) * s[u] for u in un]
    inter = [jnp.exp(m_inter[u] - m_t[u]) for u in un]
    yield
    m_new = [m_t[u][L - 1:L, :] for u in un]
    f_last = [f_col[u][L - 1:L, :] for u in un]
    kw = [jnp.exp(f_last[u] - f_col[u] + li_col[u] - m_new[u]) * k[u] for u in un]
    decay = [jnp.exp(f_last[u] + m_prev[u] - m_new[u]) for u in un]
    yield
    wv = [jnp.dot(w[u], v[u], preferred_element_type=F32) for u in un]
    kv = [lax.dot_general(kw[u], v[u], TN, preferred_element_type=F32) for u in un]
    yield
    wsum = [jnp.sum(w[u], axis=-1, keepdims=True) for u in un]
    ksum = [jnp.sum(kw[u], axis=0, keepdims=True) for u in un]
    floor = [jnp.exp(-m_t[u]) for u in un]
    yield
    den = [jnp.maximum(jnp.abs(wsum[u] + inter[u] * qn[u]), floor[u]) for u in un]
    hh = [(wv[u] + inter[u] * qc[u]) / den[u] for u in un]
    ms = [jnp.mean(hh[u] * hh[u], axis=-1, keepdims=True) for u in un]
    yield
    scale = [lax.rsqrt(ms[u] + NORM_EPS) for u in un]
    for u, (bi, h) in enumerate(units):
        gate = jax.nn.sigmoid(a_ref[bi, :, col(3, h)].astype(F32))
        h_ref[bi, :, hsl[h]] = (gate * (hh[u] * scale[u] * ng_ref[:, hsl[h]])).astype(h_ref.dtype)
    yield
    for u, (bi, h) in enumerate(units):
        c_ref[bi, h] = decay[u] * c_mat[u] + kv[u]
        n_ref[bi, h:h + 1, :] = decay[u] * n_vec[u] + ksum[u]
        m_ref[bi, :, h:h + 1] = m_new[u]


def _swa_program(q_ref, k_refs, v_refs, bias_ref, sink_ref, o_ref, first_valid):
    bb, lq, _ = q_ref.shape
    cat = lambda rs, bi: jnp.concatenate([r[bi].astype(F32) for r in rs], axis=0)
    k = [cat(k_refs, bi) for bi in range(bb)]
    v = [cat(v_refs, bi) for bi in range(bb)]
    lk = k[0].shape[0]
    valid = lax.broadcasted_iota(jnp.int32, (B_GROUP * lq, lk), 1) >= first_valid
    units = [(bi, n) for bi in range(bb) for n in range(B_KV)]
    un = range(len(units))
    head = lambda n, g: pl.ds((n * B_GROUP + g) * B_HD, B_HD)
    yield
    qn = [jnp.concatenate([q_ref[bi, :, head(n, g)].astype(F32) for g in range(B_GROUP)], axis=0)
          for bi, n in units]
    yield
    s = [lax.dot_general(qn[u], k[bi][:, n * B_HD:(n + 1) * B_HD], NT, preferred_element_type=F32)
         for u, (bi, n) in enumerate(units)]
    sk = [sink_ref[n] for _, n in units]
    yield
    s = [jnp.where(valid, s[u] * (B_HD ** -0.5) + bias_ref[n], -1e30) for u, (bi, n) in enumerate(units)]
    mx = [jnp.max(s[u], axis=-1, keepdims=True) for u in un]
    yield
    mx = [jnp.maximum(mx[u], sk[u]) for u in un]
    p = [jnp.exp(s[u] - mx[u]) for u in un]
    yield
    pv = [jnp.dot(p[u], v[bi][:, n * B_HD:(n + 1) * B_HD], preferred_element_type=F32)
          for u, (bi, n) in enumerate(units)]
    ones_k = jnp.ones((lk, B_HD), F32)
    psum = [jnp.dot(p[u], ones_k, preferred_element_type=F32) for u in un]
    yield
    den = [psum[u] + jnp.exp(sk[u] - mx[u]) for u in un]
    o = [pv[u] / den[u] for u in un]
    yield
    for u, (bi, n) in enumerate(units):
        for g in range(B_GROUP):
            o_ref[bi, :, head(n, g)] = o[u][g * lq:(g + 1) * lq, :].astype(o_ref.dtype)


def _rwkv_row(bi, c_ref, mu_ref, vec_ref, lora_ref, y_ref, s_ref, sh_ref, L):
    w0, a0, k_k, k_a, r_k, gn_g, gn_b = (vec_ref[i:i + 1, :] for i in range(7))
    ww2 = lora_ref[0:C_DECAY_LORA]
    wa2 = lora_ref[C_DECAY_LORA:C_DECAY_LORA + C_AAA_LORA]
    wg2 = lora_ref[C_DECAY_LORA + C_AAA_LORA:C_DECAY_LORA + C_AAA_LORA + C_GATE_LORA]
    dot = functools.partial(jnp.dot, preferred_element_type=F32)
    dot_nt = lambda a_, b_: lax.dot_general(a_, b_, NT, preferred_element_type=F32)
    dot_tn = lambda a_, b_: lax.dot_general(a_, b_, TN, preferred_element_type=F32)
    heads = range(C_HEADS)
    pairs = range(C_W // LANES)
    psl = [slice(p * LANES, (p + 1) * LANES) for p in pairs]

    upper_r = lax.broadcasted_iota(jnp.int32, (LANES, LANES), 0) >= C_HD
    upper_c = lax.broadcasted_iota(jnp.int32, (LANES, LANES), 1) >= C_HD
    ones_pair = (upper_r == upper_c).astype(BF16)

    def per_head_sum(x):
        return jnp.concatenate([_mxu_row_sums(x[:, psl[p]], ones_pair) for p in pairs], axis=1)

    cc = c_ref[bi][:, :C_COLS].astype(F32)
    row = lax.broadcasted_iota(jnp.int32, (L, C_COLS), 0)
    prev = jnp.where(row == 0, sh_ref[bi], pltpu.roll(cc, 1, axis=0))
    sh_ref[bi] = cc[L - 1:L, :]
    xm = cc + (prev - cc) * mu_ref[...]
    r = xm[:, 0:C_W]
    k = xm[:, C_W:2 * C_W]
    v = xm[:, 2 * C_W:3 * C_W]
    o1 = 3 * C_W
    wl = xm[:, o1:o1 + C_DECAY_LORA]
    al = xm[:, o1 + C_DECAY_LORA:o1 + C_DECAY_LORA + C_AAA_LORA]
    gl = xm[:, o1 + C_DECAY_LORA + C_AAA_LORA:C_COLS]
    yield
    lw = -math.exp(-0.5) * jax.nn.sigmoid(w0 + dot(jnp.tanh(wl), ww2))
    a = jax.nn.sigmoid(a0 + dot(al, wa2))
    g = dot(jax.nn.sigmoid(gl), wg2)
    yield
    cum = _cumsum_rows(lw)
    kkf = k * k_k
    kx = k * (1.0 + (a - 1.0) * k_a)
    kk_sq = per_head_sum(kkf * kkf)
    rk = per_head_sum(r * kx * r_k)
    yield
    cum_last = cum[L - 1:L, :]
    e_in = jnp.exp(cum)
    e_ex = jnp.exp(cum - lw)
    e_neg = jnp.exp(-cum)
    e_last = jnp.exp(cum_last - cum)
    w_chunk = jnp.exp(cum_last)
    kk = kkf * lax.rsqrt(jnp.maximum(kk_sq, 1e-24))
    yield
    b = kk * a
    kt = kk * e_ex
    rt = r * e_in
    bh = b * e_neg
    kh = kx * e_neg
    kl = kx * e_last
    bl = b * e_last
    yield
    low = lax.broadcasted_iota(jnp.int32, (L, LANES), 1) < C_HD
    low2 = lax.broadcasted_iota(jnp.int32, (2 * L, LANES), 1) < C_HD
    lhs = [jnp.concatenate([kt[:, ps], rt[:, ps]], axis=0) for ps in psl]
    lhs = [jnp.concatenate([jnp.where(low2, lhs[p], 0.0), jnp.where(low2, 0.0, lhs[p])], axis=0) for p in pairs]
    s0 = [s_ref[bi, p] for p in pairs]
    rhs = [jnp.concatenate([bh[:, psl[p]], kh[:, psl[p]], s0[p]], axis=0) for p in pairs]
    yield
    zeros = jnp.zeros((L, LANES), F32)
    v_p = [v[:, ps] for ps in psl]
    v0 = [jnp.concatenate([zeros, v_p[p]], axis=0) for p in pairs]
    klbl = [jnp.concatenate([kl[:, ps], bl[:, ps]], axis=0) for ps in psl]
    yield

    gs = [dot_nt(lhs[p], rhs[p]) for p in pairs]
    yield
    r2 = lax.broadcasted_iota(jnp.int32, (L, 2 * L), 0)
    c2 = lax.broadcasted_iota(jnp.int32, (L, 2 * L), 1)
    second = c2 >= L
    cc2 = jnp.where(second, c2 - L, c2)
    eye = (lax.broadcasted_iota(jnp.int32, (L, L), 0) == lax.broadcasted_iota(jnp.int32, (L, L), 1)).astype(F32)
    top = [gs[h // 2][(h % 2) * 2 * L:(h % 2) * 2 * L + L, :2 * L] for h in heads]
    bot = [gs[h // 2][(h % 2) * 2 * L + L:(h % 2 + 1) * 2 * L, :2 * L] for h in heads]
    tb = [jnp.where(_tri(L, strict=True), top[h][:, :L], 0.0) for h in heads]
    tk0 = [jnp.where(second & (cc2 < r2), top[h], 0.0) for h in heads]
    yield
    qq = [jnp.where(cc2 <= r2, jnp.where(second, bot[h], -bot[h]), 0.0) for h in heads]
    ks = [gs[p][:2 * L, 2 * L:] + gs[p][2 * L:, 2 * L:] for p in pairs]
    x = [eye - tb[h] for h in heads]
    p_ = [dot(tb[h], tb[h]) for h in heads]
    tkv = [dot(jnp.concatenate([tk0[2 * p], tk0[2 * p + 1]], axis=0), v0[p]) for p in pairs]
    yield
    n_sq = int(math.log2(L)) - 1
    for i in range(n_sq):
        if i + 1 < n_sq:
            xp = [dot(jnp.concatenate([x[h], p_[h]], axis=0), p_[h]) for h in heads]
            x = [x[h] + xp[h][:L] for h in heads]
            p_ = [xp[h][L:] for h in heads]
        else:
            x = [x[h] + dot(x[h], p_[h]) for h in heads]
        yield
    pick = lambda m: jnp.where(low, m[:L], m[L:])
    rhs_u = [ks[p][:L] + pick(tkv[p]) for p in pairs]
    uu = [pick(dot(jnp.concatenate([x[2 * p], x[2 * p + 1]], axis=0), rhs_u[p])) for p in pairs]
    yield
    y = [ks[p][L:] + pick(dot(jnp.concatenate([qq[2 * p], qq[2 * p + 1]], axis=0),
                              jnp.concatenate([uu[p], v_p[p]], axis=0))) for p in pairs]
    ds = [dot_tn(jnp.concatenate([v_p[p], -uu[p]], axis=0), klbl[p]) for p in pairs]
    yield
    for p in pairs:
        s_ref[bi, p] = s0[p] * w_chunk[:, psl[p]] + jnp.where(ones_pair > 0, ds[p], 0.0)
    mean = [_mxu_row_sums(y[p], ones_pair) * (1.0 / C_HD) for p in pairs]
    yield
    yc = [y[p] - mean[p] for p in pairs]
    var = [_mxu_row_sums(jnp.square(yc[p]), ones_pair) * (1.0 / C_HD) for p in pairs]
    yield
    yn = jnp.concatenate([yc[p] * lax.rsqrt(var[p] + GN_EPS) for p in pairs], axis=1)
    yield
    y_ref[bi] = ((yn * gn_g + gn_b + rk * v) * g).astype(y_ref.dtype)


def _mixers_kernel(*refs, L, n_kv, first_valid_fn):
    a_ref, g_ref, c0_ref, n0_ref, m0_ref, gb_ref, ng_ref, q_ref = refs[:8]
    k_refs = refs[8:8 + n_kv]
    v_refs = refs[8 + n_kv:8 + 2 * n_kv]
    (bias_ref, sink_ref, cc_ref, sh0_ref, s0_ref, mu_ref, vec_ref, lora_ref,
     h_ref, c_ref, n_ref, m_ref, o_ref, y_ref, s_ref, sh_ref, sp_ref) = refs[8 + 2 * n_kv:]
    bb = cc_ref.shape[0]
    head_pairs = [(bi, p) for bi in range(bb) for p in range(C_HEADS // 2)]
    first, second = pl.ds(0, C_HD), pl.ds(C_HD, C_HD)

    @pl.when(pl.program_id(1) == 0)
    def _():
        c_ref[...] = c0_ref[...]
        n_ref[...] = n0_ref[...]
        m_ref[...] = m0_ref[...]
        sh_ref[...] = sh0_ref[...]
        sp_ref[...] = jnp.zeros_like(sp_ref)
        for bi, p in head_pairs:
            sp_ref[bi, p, first, first] = s0_ref[bi, 2 * p]
            sp_ref[bi, p, second, second] = s0_ref[bi, 2 * p + 1]

    entries = [(_rwkv_row(bi, cc_ref, mu_ref, vec_ref, lora_ref, y_ref, sp_ref, sh_ref, L), bi * RWKV_ROW_SKEW, 1)
               for bi in range(bb)]
    entries.append((_mlstm_program(a_ref, g_ref, gb_ref, ng_ref, h_ref, c_ref, n_ref, m_ref, L),
                    MLSTM_FIRST_STEP, MLSTM_STEP_PERIOD))
    entries.append((_swa_program(q_ref, k_refs, v_refs, bias_ref, sink_ref, o_ref,
                                 first_valid_fn(pl.program_id(1))), SWA_FIRST_STEP, SWA_STEP_PERIOD))
    _run_programs(entries)

    @pl.when(pl.program_id(1) == pl.num_programs(1) - 1)
    def _():
        for bi, p in head_pairs:
            s_ref[bi, 2 * p] = sp_ref[bi, p, first, first]
            s_ref[bi, 2 * p + 1] = sp_ref[bi, p, second, second]


def _mixers(cols, side, st, lp, bias, sinks, L, kf=None, vf=None):
    b, t, _ = cols.shape
    nc = t // L
    bb = _batch_block(b, MIXER_ROWS_PER_STEP)
    mc, mn, mm, rs, rshift = st
    kw = B_KV * B_HD
    state = lambda shape: pl.BlockSpec(shape, lambda i, c: (i,) + (0,) * (len(shape) - 1))
    full = lambda arr: pl.BlockSpec(arr.shape, lambda i, c: (0,) * arr.ndim)
    tok = lambda arr_w, blk: pl.BlockSpec((bb, L, arr_w), lambda i, c: (i, c, blk))
    vec = jnp.concatenate([lp['c_w0'], lp['c_a0'], lp['c_k_k'], lp['c_k_a'], lp['c_r_k'], lp['c_gn_g'],
                           lp['c_gn_b'], jnp.zeros_like(lp['c_w0'])], axis=0)
    lora = jnp.concatenate([lp['c_w_w2'], lp['c_w_a2'], lp['c_w_g2']], axis=0)

    if kf is None:
        backs = list(range(WIN_CHUNKS, -1, -1))
        kv_spec = lambda col0, back: pl.BlockSpec(
            (bb, CHUNK, kw), lambda i, c: (i, jnp.maximum(c - back, 0), col0 // kw))
        kv_specs = [kv_spec(S_BK, bk) for bk in backs] + [kv_spec(S_BV, bk) for bk in backs]
        kv_args = [side] * (2 * len(backs))
        n_kv = len(backs)
        first_valid_fn = lambda c: (WIN_CHUNKS - c) * CHUNK
    else:
        lk = kf.shape[1]
        kv_specs = [pl.BlockSpec((bb, lk, kw), lambda i, c: (i, 0, 0))] * 2
        kv_args = [kf, vf]
        n_kv = 1
        first_valid_fn = lambda c: 0

    h_a, mc, mn, mm, h_b, h_c, rs, rshift = pl.pallas_call(
        functools.partial(_mixers_kernel, L=L, n_kv=n_kv, first_valid_fn=first_valid_fn),
        grid=(b // bb, nc),
        in_specs=([tok(4 * A_W, COL_A // (4 * A_W)), tok(LANES, S_AG // LANES),
                   state((bb, A_HEADS, A_DK, A_DK)), state((bb, A_HEADS, A_DK)), state((bb, 1, A_HEADS)),
                   full(lp['a_gate_bias']), full(lp['a_norm']), tok(B_W, S_BQ // B_W)]
                  + kv_specs
                  + [full(bias), full(sinks), tok(COL_C_PAD, COL_C // COL_C_PAD),
                     state((bb, 1, C_COLS)), state((bb, C_HEADS, C_HD, C_HD)),
                     full(lp['c_mu']), full(vec), full(lora)]),
        out_specs=[tok(A_W, 0), state((bb, A_HEADS, A_DK, A_DK)), state((bb, A_HEADS, A_DK)),
                   state((bb, 1, A_HEADS)), tok(B_W, 0), tok(C_W, 0),
                   state((bb, C_HEADS, C_HD, C_HD)), state((bb, 1, C_COLS))],
        out_shape=[jax.ShapeDtypeStruct((b, t, A_W), BF16),
                   jax.ShapeDtypeStruct((b, A_HEADS, A_DK, A_DK), F32),
                   jax.ShapeDtypeStruct((b, A_HEADS, A_DK), F32),
                   jax.ShapeDtypeStruct((b, 1, A_HEADS), F32),
                   jax.ShapeDtypeStruct((b, t, B_W), BF16),
                   jax.ShapeDtypeStruct((b, t, C_W), BF16),
                   jax.ShapeDtypeStruct((b, C_HEADS, C_HD, C_HD), F32),
                   jax.ShapeDtypeStruct((b, 1, C_COLS), F32)],
        scratch_shapes=[pltpu.VMEM((bb, C_HEADS // 2, 2 * C_HD, 2 * C_HD), F32)],
        compiler_params=_cparams("parallel", "arbitrary"),
    )(cols, side, mc, mn, mm, lp['a_gate_bias'], lp['a_norm'], side, *kv_args, bias, sinks, cols, rshift, rs,
      lp['c_mu'], vec, lora)
    return h_a, h_b, h_c, (mc, mn, mm, rs, rshift)


def _merge_kernel(x_ref, gpre_ref, wg_ref, ha_ref, hb_ref, hc_ref, wa_ref, wb_ref, wc_ref, wo_ref,
                  ng_ref, o_ref):
    d = x_ref.shape[1]
    x = x_ref[...]
    xn = (x * lax.rsqrt(jnp.mean(x * x, axis=-1, keepdims=True) + NORM_EPS) * gpre_ref[...]).astype(BF16)

    def branch(idx, h_ref, w_ref):
        gate = jax.nn.sigmoid(jnp.dot(xn, wg_ref[:, idx * d:(idx + 1) * d], preferred_element_type=F32))
        return gate * jnp.dot(h_ref[...].astype(BF16), w_ref[...], preferred_element_type=F32)

    merged = branch(0, ha_ref, wa_ref) + branch(1, hb_ref, wb_ref) + branch(2, hc_ref, wc_ref)
    mix = jnp.dot(merged.astype(BF16), wo_ref[...], preferred_element_type=F32)
    y = mix * lax.rsqrt(jnp.mean(mix * mix, axis=-1, keepdims=True) + NORM_EPS) * ng_ref[...]
    o_ref[...] = x + y


def _layer_weight(arr, layer):
    return pl.BlockSpec((None,) + arr.shape[1:], lambda *_: (layer, 0, 0), pipeline_mode=pl.Buffered(1))


def _merge(x, gpre, wg, h_a, h_b, h_c, wa, wb, wc, wo, ng, layer, tm):
    t, d = x.shape
    tok = lambda w: pl.BlockSpec((tm, w), lambda i: (i, 0))
    full = lambda arr: pl.BlockSpec(arr.shape, lambda i: (0, 0))
    lw = lambda arr: _layer_weight(arr, layer)
    return pl.pallas_call(
        _merge_kernel,
        grid=(t // tm,),
        in_specs=[tok(d), full(gpre), lw(wg), tok(A_W), tok(B_W), tok(C_W),
                  lw(wa), lw(wb), lw(wc), lw(wo), full(ng)],
        out_specs=tok(d),
        out_shape=jax.ShapeDtypeStruct((t, d), F32),
        compiler_params=_cparams("parallel"),
    )(x, gpre, wg, h_a, h_b, h_c, wa, wb, wc, wo, ng)


def _ffn_kernel(x_ref, gpre_ref, wi_ref, wo_ref, gpost_ref, o_ref, *, tf):
    x = x_ref[...]
    xn = (x * lax.rsqrt(jnp.mean(x * x, axis=-1, keepdims=True) + NORM_EPS) * gpre_ref[...]).astype(BF16)
    f = None
    for c in range(D_FF // tf):
        gate = jnp.dot(xn, wi_ref[:, c * tf:(c + 1) * tf], preferred_element_type=F32)
        up = jnp.dot(xn, wi_ref[:, D_FF + c * tf:D_FF + (c + 1) * tf], preferred_element_type=F32)
        act = (gate * jax.nn.sigmoid(gate) * up).astype(BF16)
        part = jnp.dot(act, wo_ref[c * tf:(c + 1) * tf, :], preferred_element_type=F32)
        f = part if f is None else f + part
    y = f * lax.rsqrt(jnp.mean(f * f, axis=-1, keepdims=True) + NORM_EPS) * gpost_ref[...]
    o_ref[...] = x + y


def _ffn(x, gpre, w_in, w_out, gpost, layer, tm, tf):
    t, d = x.shape
    full = lambda arr: pl.BlockSpec(arr.shape, lambda i: (0, 0))
    return pl.pallas_call(
        functools.partial(_ffn_kernel, tf=tf),
        grid=(t // tm,),
        in_specs=[pl.BlockSpec((tm, d), lambda i: (i, 0)), full(gpre), _layer_weight(w_in, layer),
                  _layer_weight(w_out, layer), full(gpost)],
        out_specs=pl.BlockSpec((tm, d), lambda i: (i, 0)),
        out_shape=jax.ShapeDtypeStruct((t, d), F32),
        compiler_params=_cparams("parallel"),
    )(x, gpre, w_in, w_out, gpost)


def _rel_bucket(rel):
    half = NUM_BUCKETS // 2
    exact = half // 2
    n = jnp.abs(rel)
    far = exact + (jnp.log(jnp.maximum(n, 1).astype(F32) / exact)
                   / math.log(REL_MAX_DIST / exact) * (half - exact)).astype(jnp.int32)
    far = jnp.minimum(far, half - 1)
    return jnp.where(rel > 0, half, 0) + jnp.where(n < exact, n, far)


def _rel_bias(table, n_q, n_before):
    rel = (jnp.arange(n_before + n_q)[None, :] - n_before) - jnp.arange(n_q)[:, None]
    onehot = (_rel_bucket(rel)[None] == jnp.arange(NUM_BUCKETS)[:, None, None]).astype(F32)
    bias = jnp.einsum('bh,bqk->hqk', table.astype(F32), onehot, precision=HI)
    return bias.reshape(B_KV, B_GROUP * n_q, n_before + n_q)


def _layer(x, st, lp, bias, L, tm):
    b, t, d = x.shape
    swa_k, swa_v, mc, mn, mm, rs, rshift = st
    x2 = x.reshape(b * t, d)
    layer = lp['layer']
    cols2, side2 = _proj_in(x2, lp['norm_mix_pre'], lp['w_in'], layer, tm, PROJ_IN_TN)
    cols = cols2.reshape(b, t, N_COLS)
    side = side2.reshape(b, t, SIDE_W)

    lq = bias.shape[1] // B_GROUP
    sinks = jnp.repeat(lp['b_sinks'][0], lq).reshape(B_KV, B_GROUP * lq, 1)
    keep = min(WINDOW, t)
    k_new = side[:, t - keep:, S_BK:S_BK + B_KV * B_HD]
    v_new = side[:, t - keep:, S_BV:S_BV + B_KV * B_HD]
    if swa_k is None:
        kf = vf = None
        swa_k, swa_v = k_new, v_new
    else:
        kf = jnp.concatenate([swa_k, k_new], axis=1)
        vf = jnp.concatenate([swa_v, v_new], axis=1)
        swa_k, swa_v = kf[:, t:], vf[:, t:]
    h_a, h_b, h_c, (mc, mn, mm, rs, rshift) = _mixers(cols, side, (mc, mn, mm, rs, rshift), lp, bias, sinks, L,
                                                      kf, vf)

    x2 = _merge(x2, lp['norm_mix_pre'], lp['w_gate'], h_a.reshape(b * t, A_W), h_b.reshape(b * t, B_W),
                h_c.reshape(b * t, C_W), lp['w_branch_a'], lp['w_branch_b'], lp['w_branch_c'], lp['w_out'],
                lp['norm_mix_post'], layer, tm)
    x2 = _ffn(x2, lp['norm_ffn_pre'], lp['w_ffn_in'], lp['w_ffn_out'], lp['norm_ffn_post'], layer, tm, 256)
    return x2.reshape(b, t, d), (swa_k, swa_v, mc, mn, mm, rs, rshift)


def _trunk(x, layer_states, layers, bias, L, tm):
    new = []
    for l in range(DEPTH):
        x, st = _layer(x, layer_states[l], layers[l], bias, L, tm)
        new.append(st)
    return x, [jnp.stack([s[i] for s in new]) for i in range(7)]


def kernel(x_prompt, x_sample, cache_swa_k, cache_swa_v, state_mlstm_c, state_mlstm_n, state_mlstm_m, state_rwkv_s, state_rwkv_shift, w_in, norm_mix_pre, norm_mix_post, norm_ffn_pre, norm_ffn_post, a_gate_bias, a_norm, rel_bias, b_sinks, c_mu, c_w0, c_w_w2, c_a0, c_w_a2, c_w_g2, c_k_k, c_k_a, c_r_k, c_gn_g, c_gn_b, w_branch_a, w_branch_b, w_branch_c, w_out, w_ffn_in, w_ffn_out):
    bp, tp, _ = x_prompt.shape
    bs, ts, _ = x_sample.shape
    kw = B_KV * B_HD
    w_in_p, w_gate_p = _split_w_in(w_in, 256)
    stacked = {
        'w_in': w_in_p, 'w_gate': w_gate_p,
        'w_branch_a': _to_bf16(w_branch_a, A_W), 'w_branch_b': _to_bf16(w_branch_b, B_W),
        'w_branch_c': _to_bf16(w_branch_c, C_W), 'w_out': _to_bf16(w_out, D_MODEL),
        'w_ffn_in': _to_bf16(w_ffn_in, D_MODEL // 2), 'w_ffn_out': _to_bf16(w_ffn_out, D_FF // 2),
    }
    gate_bias = jnp.pad(a_gate_bias, ((0, 0), (0, LANES - 2 * A_HEADS)))
    row = lambda p, l: p[l][None, :]
    layers = []
    for l in range(DEPTH):
        layers.append({
            **stacked, 'layer': l, 'norm_mix_pre': row(norm_mix_pre, l), 'norm_mix_post': row(norm_mix_post, l),
            'norm_ffn_pre': row(norm_ffn_pre, l), 'norm_ffn_post': row(norm_ffn_post, l),
            'a_gate_bias': row(gate_bias, l), 'a_norm': row(a_norm, l), 'b_sinks': row(b_sinks, l),
            'c_mu': row(c_mu, l), 'c_w0': row(c_w0, l), 'c_w_w2': c_w_w2[l], 'c_a0': row(c_a0, l),
            'c_w_a2': c_w_a2[l], 'c_w_g2': c_w_g2[l], 'c_k_k': row(c_k_k, l), 'c_k_a': row(c_k_a, l),
            'c_r_k': row(c_r_k, l), 'c_gn_g': row(c_gn_g, l), 'c_gn_b': row(c_gn_b, l),
        })

    fresh = (None, None,
             jnp.zeros((bp, A_HEADS, A_DK, A_DK), F32), jnp.zeros((bp, A_HEADS, A_DK), F32),
             jnp.zeros((bp, 1, A_HEADS), F32), jnp.zeros((bp, C_HEADS, C_HD, C_HD), F32),
             jnp.zeros((bp, 1, C_COLS), F32))
    lp_chunk = min(CHUNK, tp)
    y_prompt, p_st = _trunk(x_prompt, [fresh] * DEPTH, layers,
                            _rel_bias(rel_bias, lp_chunk, WIN_CHUNKS * CHUNK), lp_chunk, min(1024, bp * tp))

    n_before = cache_swa_k.shape[2]
    carried = [(cache_swa_k[l].reshape(bs, n_before, kw), cache_swa_v[l].reshape(bs, n_before, kw),
                state_mlstm_c[l], state_mlstm_n[l], state_mlstm_m[l][:, None, :],
                state_rwkv_s[l], state_rwkv_shift[l]) for l in range(DEPTH)]
    ls_chunk = min(CHUNK, ts)
    y_sample, s_st = _trunk(x_sample, carried, layers, _rel_bias(rel_bias, ts, n_before), ls_chunk,
                            min(1024, bs * ts))

    def finish(st, b):
        swa_k, swa_v, mc, mn, mm, rs, rshift = st
        n_rows = swa_k.shape[2]
        return (swa_k.reshape(DEPTH, b, n_rows, B_KV, B_HD), swa_v.reshape(DEPTH, b, n_rows, B_KV, B_HD),
                mc, mn, mm.reshape(DEPTH, b, A_HEADS), rs, rshift)

    return (y_prompt, y_sample) + finish(p_st, bp) + finish(s_st, bs)
```

```python
import functools
import math

import jax
import jax.numpy as jnp
from jax import lax
from jax.experimental import pallas as pl
from jax.experimental.pallas import tpu as pltpu

F32 = jnp.float32
BF16 = jnp.bfloat16

D_MODEL = 1024
DEPTH = 4
CHUNK = 64
NORM_EPS = 1e-6

A_HEADS = 4
A_DK = 128
A_W = 512
B_HEADS = 8
B_KV = 2
B_GROUP = 4
B_HD = 64
B_W = 512
WINDOW = 128
WIN_CHUNKS = 2
NUM_BUCKETS = 32
REL_MAX_DIST = 256
C_HEADS = 8
C_HD = 64
C_W = 512
C_DECAY_LORA = 64
C_AAA_LORA = 64
C_GATE_LORA = 128
C_COLS = 1792
GN_EPS = 64e-5
D_FF = 2816

COL_A = 0
COL_C = 2048
COL_C_PAD = 2048
COL_BQ = 4096
COL_BK = 4608
COL_BV = 4736
COL_AG = 4864
N_COLS = 5120
PROJ_IN_TN = 2560

LANES = 128
SIDE_COL0 = COL_BQ
SIDE_W = COL_AG + LANES - COL_BQ
S_BQ, S_BK, S_BV, S_AG = 0, COL_BK - COL_BQ, COL_BV - COL_BQ, COL_AG - COL_BQ
VMEM_LIMIT = 56 * 1024 * 1024
HI = lax.Precision.HIGHEST

NT = (((1,), (1,)), ((), ()))
TN = (((0,), (0,)), ((), ()))


MIXER_ROWS_PER_STEP = 4
RWKV_ROW_SKEW = 8
MLSTM_FIRST_STEP, MLSTM_STEP_PERIOD = 3, 4
SWA_FIRST_STEP, SWA_STEP_PERIOD = 5, 5


def _batch_block(b, want):
    return max(d for d in range(1, want + 1) if b % d == 0)


def _cparams(*sem):
    return pltpu.CompilerParams(dimension_semantics=sem, vmem_limit_bytes=VMEM_LIMIT)


def _tri(n, strict=False):
    r = lax.broadcasted_iota(jnp.int32, (n, n), 0)
    c = lax.broadcasted_iota(jnp.int32, (n, n), 1)
    return (c < r) if strict else (c <= r)


def _mxu_row_sums(x, ones):
    hi = x.astype(BF16)
    lo = (x - hi.astype(F32)).astype(BF16)
    return jnp.dot(hi, ones, preferred_element_type=F32) + jnp.dot(lo, ones, preferred_element_type=F32)


def _cumsum_rows(x):
    row = lax.broadcasted_iota(jnp.int32, x.shape, 0)
    shift = 1
    while shift < x.shape[0]:
        x = x + jnp.where(row >= shift, pltpu.roll(x, shift, axis=0), 0.0)
        shift *= 2
    return x


def _cast_kernel(w_ref, o_ref):
    o_ref[...] = w_ref[...].astype(o_ref.dtype)


def _to_bf16(w, tr):
    depth, rows, cols = w.shape
    spec = pl.BlockSpec((1, tr, cols), lambda l, i: (l, i, 0))
    return pl.pallas_call(
        _cast_kernel, grid=(depth, rows // tr), in_specs=[spec], out_specs=spec,
        out_shape=jax.ShapeDtypeStruct(w.shape, BF16), compiler_params=_cparams("parallel", "parallel"),
    )(w)


def _split_w_in_kernel(w_ref, cols_ref, gate_ref):
    a_main = 4 * A_W
    a_all = a_main + 2 * A_HEADS
    b_all = B_W + 2 * B_KV * B_HD
    c0 = a_all + b_all
    dt = cols_ref.dtype
    cols_ref[0, :, COL_A:COL_A + a_main] = w_ref[0, :, 0:a_main].astype(dt)
    cols_ref[0, :, COL_C:COL_C + C_COLS] = w_ref[0, :, c0:c0 + C_COLS].astype(dt)
    cols_ref[0, :, COL_C + C_COLS:COL_BQ] = jnp.zeros((w_ref.shape[1], COL_BQ - COL_C - C_COLS), dt)
    cols_ref[0, :, COL_BQ:COL_BQ + b_all] = w_ref[0, :, a_all:a_all + b_all].astype(dt)
    cols_ref[0, :, COL_AG:N_COLS] = jnp.zeros((w_ref.shape[1], N_COLS - COL_AG), dt)
    cols_ref[0, :, COL_AG:COL_AG + 2 * A_HEADS] = w_ref[0, :, a_main:a_all].astype(dt)
    gate_ref[0] = w_ref[0, :, c0 + C_COLS:].astype(dt)


def _split_w_in(w, tr):
    depth, d, n_in = w.shape
    n_gate = n_in - (4 * A_W + 2 * A_HEADS + B_W + 2 * B_KV * B_HD + C_COLS)
    spec = lambda n: pl.BlockSpec((1, tr, n), lambda l, i: (l, i, 0))
    return pl.pallas_call(
        _split_w_in_kernel, grid=(depth, d // tr), in_specs=[spec(n_in)], out_specs=[spec(N_COLS), spec(n_gate)],
        out_shape=[jax.ShapeDtypeStruct((depth, d, N_COLS), BF16), jax.ShapeDtypeStruct((depth, d, n_gate), BF16)],
        compiler_params=_cparams("parallel", "parallel"),
    )(w)


def _proj_in_kernel(x_ref, g_ref, w_ref, o_ref, gate_ref, xn_ref, *, tn):
    j = pl.program_id(1)

    @pl.when(j == 0)
    def _():
        x = x_ref[...]
        y = x * lax.rsqrt(jnp.mean(x * x, axis=-1, keepdims=True) + NORM_EPS) * g_ref[...]
        xn_ref[...] = y.astype(BF16)

    acc = jnp.dot(xn_ref[...], w_ref[...], preferred_element_type=F32)
    o_ref[...] = acc.astype(BF16)

    @pl.when(j == SIDE_COL0 // tn)
    def _():
        gate_ref[...] = acc[:, SIDE_COL0 % tn:SIDE_COL0 % tn + SIDE_W]


def _proj_in(x, g, w, layer, tm, tn):
    assert SIDE_COL0 // tn == (SIDE_COL0 + SIDE_W - 1) // tn
    t, d = x.shape
    n = w.shape[2]
    return pl.pallas_call(
        functools.partial(_proj_in_kernel, tn=tn),
        grid=(t // tm, n // tn),
        in_specs=[
            pl.BlockSpec((tm, d), lambda i, j: (i, 0)),
            pl.BlockSpec((1, d), lambda i, j: (0, 0)),
            pl.BlockSpec((None, d, tn), lambda i, j: (layer, 0, j)),
        ],
        out_specs=[pl.BlockSpec((tm, tn), lambda i, j: (i, j)),
                   pl.BlockSpec((tm, SIDE_W), lambda i, j: (i, 0))],
        out_shape=[jax.ShapeDtypeStruct((t, n), BF16), jax.ShapeDtypeStruct((t, SIDE_W), F32)],
        scratch_shapes=[pltpu.VMEM((tm, d), BF16)],
        compiler_params=_cparams("parallel", "arbitrary"),
    )(x, g, w)


def _run_programs(entries):
    live = list(entries)
    step = 0
    while live:
        for entry in list(live):
            prog, first, period = entry
            if step >= first and (step - first) % period == 0:
                if next(prog, StopIteration) is StopIteration:
                    live.remove(entry)
        step += 1


def _mlstm_program(a_ref, g_ref, gb_ref, ng_ref, h_ref, c_ref, n_ref, m_ref, L):
    bb = a_ref.shape[0]
    lane = lax.broadcasted_iota(jnp.int32, (L, LANES), 1)
    sel = (lax.broadcasted_iota(jnp.int32, (8, LANES), 0)
           == lax.broadcasted_iota(jnp.int32, (8, LANES), 1)).astype(F32)
    tri_f = _tri(L).astype(F32)
    z, zrow = [], []
    for bi in range(bb):
        gates = g_ref[bi] + gb_ref[...]
        lf = jnp.minimum(gates, 0.0) - jnp.log1p(jnp.exp(-jnp.abs(gates)))
        cum = jnp.dot(tri_f, lf, precision=HI, preferred_element_type=F32)
        z.append(jnp.where(lane < A_HEADS, gates, cum))
        zrow.append(lax.dot_general(sel, z[bi], NT, precision=HI, preferred_element_type=F32))
    yield
    causal = _tri(L)
    col = lambda part, h: pl.ds(part * A_W + h * A_DK, A_DK)
    hsl = [pl.ds(h * A_DK, A_DK) for h in range(A_HEADS)]
    units = [(bi, h) for bi in range(bb) for h in range(A_HEADS)]
    un = range(len(units))

    q = [a_ref[bi, :, col(0, h)].astype(F32) for bi, h in units]
    k = [a_ref[bi, :, col(1, h)].astype(F32) * (A_DK ** -0.5) for bi, h in units]
    v = [a_ref[bi, :, col(2, h)].astype(F32) for bi, h in units]
    c_mat = [c_ref[bi, h] for bi, h in units]
    n_vec = [n_ref[bi, h:h + 1, :] for bi, h in units]
    m_prev = [m_ref[bi, :, h:h + 1] for bi, h in units]
    f_col = [z[bi][:, A_HEADS + h:A_HEADS + h + 1] for bi, h in units]
    li_col = [z[bi][:, h:h + 1] for bi, h in units]
    yield
    s = [lax.dot_general(q[u], k[u], NT, preferred_element_type=F32) for u in un]
    qc = [jnp.dot(q[u], c_mat[u], preferred_element_type=F32) for u in un]
    yield
    dlog = [jnp.where(causal, f_col[u] - zrow[bi][A_HEADS + h:A_HEADS + h + 1, :] + zrow[bi][h:h + 1, :], -jnp.inf)
            for u, (bi, h) in enumerate(units)]
    dmax = [jnp.max(dlog[u], axis=-1, keepdims=True) for u in un]
    qn = [jnp.sum(q[u] * n_vec[u], axis=-1, keepdims=True) for u in un]
    yield
    m_inter = [f_col[u] + m_prev[u] for u in un]
    m_t = [jnp.maximum(m_inter[u], dmax[u]) for u in un]
    w = [jnp.exp(dlog[u] - m_t[u]) * s[u] for u in un]
    inter = [jnp.exp(m_inter[u] - m_t[u]) for u in un]
    yield
    m_new = [m_t[u][L - 1:L, :] for u in un]
    f_last = [f_col[u][L - 1:L, :] for u in un]
    kw = [jnp.exp(f_last[u] - f_col[u] + li_col[u] - m_new[u]) * k[u] for u in un]
    decay = [jnp.exp(f_last[u] + m_prev[u] - m_new[u]) for u in un]
    yield
    wv = [jnp.dot(w[u], v[u], preferred_element_type=F32) for u in un]
    kv = [lax.dot_general(kw[u], v[u], TN, preferred_element_type=F32) for u in un]
    yield
    wsum = [jnp.sum(w[u], axis=-1, keepdims=True) for u in un]
    ksum = [jnp.sum(kw[u], axis=0, keepdims=True) for u in un]
    floor = [jnp.exp(-m_t[u]) for u in un]
    yield
    den = [jnp.maximum(jnp.abs(wsum[u] + inter[u] * qn[u]), floor[u]) for u in un]
    hh = [(wv[u] + inter[u] * qc[u]) / den[u] for u in un]
    ms = [jnp.mean(hh[u] * hh[u], axis=-1, keepdims=True) for u in un]
    yield
    scale = [lax.rsqrt(ms[u] + NORM_EPS) for u in un]
    for u, (bi, h) in enumerate(units):
        gate = jax.nn.sigmoid(a_ref[bi, :, col(3, h)].astype(F32))
        h_ref[bi, :, hsl[h]] = (gate * (hh[u] * scale[u] * ng_ref[:, hsl[h]])).astype(h_ref.dtype)
    yield
    for u, (bi, h) in enumerate(units):
        c_ref[bi, h] = decay[u] * c_mat[u] + kv[u]
        n_ref[bi, h:h + 1, :] = decay[u] * n_vec[u] + ksum[u]
        m_ref[bi, :, h:h + 1] = m_new[u]


def _swa_program(q_ref, k_refs, v_refs, bias_ref, sink_ref, o_ref, first_valid):
    bb, lq, _ = q_ref.shape
    cat = lambda rs, bi: jnp.concatenate([r[bi].astype(F32) for r in rs], axis=0)
    k = [cat(k_refs, bi) for bi in range(bb)]
    v = [cat(v_refs, bi) for bi in range(bb)]
    lk = k[0].shape[0]
    valid = lax.broadcasted_iota(jnp.int32, (B_GROUP * lq, lk), 1) >= first_valid
    units = [(bi, n) for bi in range(bb) for n in range(B_KV)]
    un = range(len(units))
    head = lambda n, g: pl.ds((n * B_GROUP + g) * B_HD, B_HD)
    yield
    qn = [jnp.concatenate([q_ref[bi, :, head(n, g)].astype(F32) for g in range(B_GROUP)], axis=0)
          for bi, n in units]
    yield
    s = [lax.dot_general(qn[u], k[bi][:, n * B_HD:(n + 1) * B_HD], NT, preferred_element_type=F32)
         for u, (bi, n) in enumerate(units)]
    sk = [sink_ref[n] for _, n in units]
    yield
    s = [jnp.where(valid, s[u] * (B_HD ** -0.5) + bias_ref[n], -1e30) for u, (bi, n) in enumerate(units)]
    mx = [jnp.max(s[u], axis=-1, keepdims=True) for u in un]
    yield
    mx = [jnp.maximum(mx[u], sk[u]) for u in un]
    p = [jnp.exp(s[u] - mx[u]) for u in un]
    yield
    pv = [jnp.dot(p[u], v[bi][:, n * B_HD:(n + 1) * B_HD], preferred_element_type=F32)
          for u, (bi, n) in enumerate(units)]
    ones_k = jnp.ones((lk, B_HD), F32)
    psum = [jnp.dot(p[u], ones_k, preferred_element_type=F32) for u in un]
    yield
    den = [psum[u] + jnp.exp(sk[u] - mx[u]) for u in un]
    o = [pv[u] / den[u] for u in un]
    yield
    for u, (bi, n) in enumerate(units):
        for g in range(B_GROUP):
            o_ref[bi, :, head(n, g)] = o[u][g * lq:(g + 1) * lq, :].astype(o_ref.dtype)


def _rwkv_row(bi, c_ref, mu_ref, vec_ref, lora_ref, y_ref, s_ref, sh_ref, L):
    w0, a0, k_k, k_a, r_k, gn_g, gn_b = (vec_ref[i:i + 1, :] for i in range(7))
    ww2 = lora_ref[0:C_DECAY_LORA]
    wa2 = lora_ref[C_DECAY_LORA:C_DECAY_LORA + C_AAA_LORA]
    wg2 = lora_ref[C_DECAY_LORA + C_AAA_LORA:C_DECAY_LORA + C_AAA_LORA + C_GATE_LORA]
    dot = functools.partial(jnp.dot, preferred_element_type=F32)
    dot_nt = lambda a_, b_: lax.dot_general(a_, b_, NT, preferred_element_type=F32)
    dot_tn = lambda a_, b_: lax.dot_general(a_, b_, TN, preferred_element_type=F32)
    heads = range(C_HEADS)
    pairs = range(C_W // LANES)
    psl = [slice(p * LANES, (p + 1) * LANES) for p in pairs]

    upper_r = lax.broadcasted_iota(jnp.int32, (LANES, LANES), 0) >= C_HD
    upper_c = lax.broadcasted_iota(jnp.int32, (LANES, LANES), 1) >= C_HD
    ones_pair = (upper_r == upper_c).astype(BF16)

    def per_head_sum(x):
        return jnp.concatenate([_mxu_row_sums(x[:, psl[p]], ones_pair) for p in pairs], axis=1)

    cc = c_ref[bi][:, :C_COLS].astype(F32)
    row = lax.broadcasted_iota(jnp.int32, (L, C_COLS), 0)
    prev = jnp.where(row == 0, sh_ref[bi], pltpu.roll(cc, 1, axis=0))
    sh_ref[bi] = cc[L - 1:L, :]
    xm = cc + (prev - cc) * mu_ref[...]
    r = xm[:, 0:C_W]
    k = xm[:, C_W:2 * C_W]
    v = xm[:, 2 * C_W:3 * C_W]
    o1 = 3 * C_W
    wl = xm[:, o1:o1 + C_DECAY_LORA]
    al = xm[:, o1 + C_DECAY_LORA:o1 + C_DECAY_LORA + C_AAA_LORA]
    gl = xm[:, o1 + C_DECAY_LORA + C_AAA_LORA:C_COLS]
    yield
    lw = -math.exp(-0.5) * jax.nn.sigmoid(w0 + dot(jnp.tanh(wl), ww2))
    a = jax.nn.sigmoid(a0 + dot(al, wa2))
    g = dot(jax.nn.sigmoid(gl), wg2)
    yield
    cum = _cumsum_rows(lw)
    kkf = k * k_k
    kx = k * (1.0 + (a - 1.0) * k_a)
    kk_sq = per_head_sum(kkf * kkf)
    rk = per_head_sum(r * kx * r_k)
    yield
    cum_last = cum[L - 1:L, :]
    e_in = jnp.exp(cum)
    e_ex = jnp.exp(cum - lw)
    e_neg = jnp.exp(-cum)
    e_last = jnp.exp(cum_last - cum)
    w_chunk = jnp.exp(cum_last)
    kk = kkf * lax.rsqrt(jnp.maximum(kk_sq, 1e-24))
    yield
    b = kk * a
    kt = kk * e_ex
    rt = r * e_in
    bh = b * e_neg
    kh = kx * e_neg
    kl = kx * e_last
    bl = b * e_last
    yield
    low = lax.broadcasted_iota(jnp.int32, (L, LANES), 1) < C_HD
    low2 = lax.broadcasted_iota(jnp.int32, (2 * L, LANES), 1) < C_HD
    lhs = [jnp.concatenate([kt[:, ps], rt[:, ps]], axis=0) for ps in psl]
    lhs = [jnp.concatenate([jnp.where(low2, lhs[p], 0.0), jnp.where(low2, 0.0, lhs[p])], axis=0) for p in pairs]
    s0 = [s_ref[bi, p] for p in pairs]
    rhs = [jnp.concatenate([bh[:, psl[p]], kh[:, psl[p]], s0[p]], axis=0) for p in pairs]
    yield
    zeros = jnp.zeros((L, LANES), F32)
    v_p = [v[:, ps] for ps in psl]
    v0 = [jnp.concatenate([zeros, v_p[p]], axis=0) for p in pairs]
    klbl = [jnp.concatenate([kl[:, ps], bl[:, ps]], axis=0) for ps in psl]
    yield

    gs = [dot_nt(lhs[p], rhs[p]) for p in pairs]
    yield
    r2 = lax.broadcasted_iota(jnp.int32, (L, 2 * L), 0)
    c2 = lax.broadcasted_iota(jnp.int32, (L, 2 * L), 1)
    second = c2 >= L
    cc2 = jnp.where(second, c2 - L, c2)
    eye = (lax.broadcasted_iota(jnp.int32, (L, L), 0) == lax.broadcasted_iota(jnp.int32, (L, L), 1)).astype(F32)
    top = [gs[h // 2][(h % 2) * 2 * L:(h % 2) * 2 * L + L, :2 * L] for h in heads]
    bot = [gs[h // 2][(h % 2) * 2 * L + L:(h % 2 + 1) * 2 * L, :2 * L] for h in heads]
    tb = [jnp.where(_tri(L, strict=True), top[h][:, :L], 0.0) for h in heads]
    tk0 = [jnp.where(second & (cc2 < r2), top[h], 0.0) for h in heads]
    yield
    qq = [jnp.where(cc2 <= r2, jnp.where(second, bot[h], -bot[h]), 0.0) for h in heads]
    ks = [gs[p][:2 * L, 2 * L:] + gs[p][2 * L:, 2 * L:] for p in pairs]
    x = [eye - tb[h] for h in heads]
    p_ = [dot(tb[h], tb[h]) for h in heads]
    tkv = [dot(jnp.concatenate([tk0[2 * p], tk0[2 * p + 1]], axis=0), v0[p]) for p in pairs]
    yield
    n_sq = int(math.log2(L)) - 1
    for i in range(n_sq):
        if i + 1 < n_sq:
            xp = [dot(jnp.concatenate([x[h], p_[h]], axis=0), p_[h]) for h in heads]
            x = [x[h] + xp[h][:L] for h in heads]
            p_ = [xp[h][L:] for h in heads]
        else:
            x = [x[h] + dot(x[h], p_[h]) for h in heads]
        yield
    pick = lambda m: jnp.where(low, m[:L], m[L:])
    rhs_u = [ks[p][:L] + pick(tkv[p]) for p in pairs]
    uu = [pick(dot(jnp.concatenate([x[2 * p], x[2 * p + 1]], axis=0), rhs_u[p])) for p in pairs]
    yield
    y = [ks[p][L:] + pick(dot(jnp.concatenate([qq[2 * p], qq[2 * p + 1]], axis=0),
                              jnp.concatenate([uu[p], v_p[p]], axis=0))) for p in pairs]
    ds = [dot_tn(jnp.concatenate([v_p[p], -uu[p]], axis=0), klbl[p]) for p in pairs]
    yield
    for p in pairs:
        s_ref[bi, p] = s0[p] * w_chunk[:, psl[p]] + jnp.where(ones_pair > 0, ds[p], 0.0)
    mean = [_mxu_row_sums(y[p], ones_pair) * (1.0 / C_HD) for p in pairs]
    yield
    yc = [y[p] - mean[p] for p in pairs]
    var = [_mxu_row_sums(jnp.square(yc[p]), ones_pair) * (1.0 / C_HD) for p in pairs]
    yield
    yn = jnp.concatenate([yc[p] * lax.rsqrt(var[p] + GN_EPS) for p in pairs], axis=1)
    yield
    y_ref[bi] = ((yn * gn_g + gn_b + rk * v) * g).astype(y_ref.dtype)


def _mixers_kernel(*refs, L, n_kv, first_valid_fn):
    a_ref, g_ref, c0_ref, n0_ref, m0_ref, gb_ref, ng_ref, q_ref = refs[:8]
    k_refs = refs[8:8 + n_kv]
    v_refs = refs[8 + n_kv:8 + 2 * n_kv]
    (bias_ref, sink_ref, cc_ref, sh0_ref, s0_ref, mu_ref, vec_ref, lora_ref,
     h_ref, c_ref, n_ref, m_ref, o_ref, y_ref, s_ref, sh_ref, sp_ref) = refs[8 + 2 * n_kv:]
    bb = cc_ref.shape[0]
    head_pairs = [(bi, p) for bi in range(bb) for p in range(C_HEADS // 2)]
    first, second = pl.ds(0, C_HD), pl.ds(C_HD, C_HD)

    @pl.when(pl.program_id(1) == 0)
    def _():
        c_ref[...] = c0_ref[...]
        n_ref[...] = n0_ref[...]
        m_ref[...] = m0_ref[...]
        sh_ref[...] = sh0_ref[...]
        sp_ref[...] = jnp.zeros_like(sp_ref)
        for bi, p in head_pairs:
            sp_ref[bi, p, first, first] = s0_ref[bi, 2 * p]
            sp_ref[bi, p, second, second] = s0_ref[bi, 2 * p + 1]

    entries = [(_rwkv_row(bi, cc_ref, mu_ref, vec_ref, lora_ref, y_ref, sp_ref, sh_ref, L), bi * RWKV_ROW_SKEW, 1)
               for bi in range(bb)]
    entries.append((_mlstm_program(a_ref, g_ref, gb_ref, ng_ref, h_ref, c_ref, n_ref, m_ref, L),
                    MLSTM_FIRST_STEP, MLSTM_STEP_PERIOD))
    entries.append((_swa_program(q_ref, k_refs, v_refs, bias_ref, sink_ref, o_ref,
                                 first_valid_fn(pl.program_id(1))), SWA_FIRST_STEP, SWA_STEP_PERIOD))
    _run_programs(entries)

    @pl.when(pl.program_id(1) == pl.num_programs(1) - 1)
    def _():
        for bi, p in head_pairs:
            s_ref[bi, 2 * p] = sp_ref[bi, p, first, first]
            s_ref[bi, 2 * p + 1] = sp_ref[bi, p, second, second]


def _mixers(cols, side, st, lp, bias, sinks, L, kf=None, vf=None):
    b, t, _ = cols.shape
    nc = t // L
    bb = _batch_block(b, MIXER_ROWS_PER_STEP)
    mc, mn, mm, rs, rshift = st
    kw = B_KV * B_HD
    state = lambda shape: pl.BlockSpec(shape, lambda i, c: (i,) + (0,) * (len(shape) - 1))
    full = lambda arr: pl.BlockSpec(arr.shape, lambda i, c: (0,) * arr.ndim)
    tok = lambda arr_w, blk: pl.BlockSpec((bb, L, arr_w), lambda i, c: (i, c, blk))
    vec = jnp.concatenate([lp['c_w0'], lp['c_a0'], lp['c_k_k'], lp['c_k_a'], lp['c_r_k'], lp['c_gn_g'],
                           lp['c_gn_b'], jnp.zeros_like(lp['c_w0'])], axis=0)
    lora = jnp.concatenate([lp['c_w_w2'], lp['c_w_a2'], lp['c_w_g2']], axis=0)

    if kf is None:
        backs = list(range(WIN_CHUNKS, -1, -1))
        kv_spec = lambda col0, back: pl.BlockSpec(
            (bb, CHUNK, kw), lambda i, c: (i, jnp.maximum(c - back, 0), col0 // kw))
        kv_specs = [kv_spec(S_BK, bk) for bk in backs] + [kv_spec(S_BV, bk) for bk in backs]
        kv_args = [side] * (2 * len(backs))
        n_kv = len(backs)
        first_valid_fn = lambda c: (WIN_CHUNKS - c) * CHUNK
    else:
        lk = kf.shape[1]
        kv_specs = [pl.BlockSpec((bb, lk, kw), lambda i, c: (i, 0, 0))] * 2
        kv_args = [kf, vf]
        n_kv = 1
        first_valid_fn = lambda c: 0

    h_a, mc, mn, mm, h_b, h_c, rs, rshift = pl.pallas_call(
        functools.partial(_mixers_kernel, L=L, n_kv=n_kv, first_valid_fn=first_valid_fn),
        grid=(b // bb, nc),
        in_specs=([tok(4 * A_W, COL_A // (4 * A_W)), tok(LANES, S_AG // LANES),
                   state((bb, A_HEADS, A_DK, A_DK)), state((bb, A_HEADS, A_DK)), state((bb, 1, A_HEADS)),
                   full(lp['a_gate_bias']), full(lp['a_norm']), tok(B_W, S_BQ // B_W)]
                  + kv_specs
                  + [full(bias), full(sinks), tok(COL_C_PAD, COL_C // COL_C_PAD),
                     state((bb, 1, C_COLS)), state((bb, C_HEADS, C_HD, C_HD)),
                     full(lp['c_mu']), full(vec), full(lora)]),
        out_specs=[tok(A_W, 0), state((bb, A_HEADS, A_DK, A_DK)), state((bb, A_HEADS, A_DK)),
                   state((bb, 1, A_HEADS)), tok(B_W, 0), tok(C_W, 0),
                   state((bb, C_HEADS, C_HD, C_HD)), state((bb, 1, C_COLS))],
        out_shape=[jax.ShapeDtypeStruct((b, t, A_W), BF16),
                   jax.ShapeDtypeStruct((b, A_HEADS, A_DK, A_DK), F32),
                   jax.ShapeDtypeStruct((b, A_HEADS, A_DK), F32),
                   jax.ShapeDtypeStruct((b, 1, A_HEADS), F32),
                   jax.ShapeDtypeStruct((b, t, B_W), BF16),
                   jax.ShapeDtypeStruct((b, t, C_W), BF16),
                   jax.ShapeDtypeStruct((b, C_HEADS, C_HD, C_HD), F32),
                   jax.ShapeDtypeStruct((b, 1, C_COLS), F32)],
        scratch_shapes=[pltpu.VMEM((bb, C_HEADS // 2, 2 * C_HD, 2 * C_HD), F32)],
        compiler_params=_cparams("parallel", "arbitrary"),
    )(cols, side, mc, mn, mm, lp['a_gate_bias'], lp['a_norm'], side, *kv_args, bias, sinks, cols, rshift, rs,
      lp['c_mu'], vec, lora)
    return h_a, h_b, h_c, (mc, mn, mm, rs, rshift)


def _merge_kernel(x_ref, gpre_ref, wg_ref, ha_ref, hb_ref, hc_ref, wa_ref, wb_ref, wc_ref, wo_ref,
                  ng_ref, o_ref):
    d = x_ref.shape[1]
    x = x_ref[...]
    xn = (x * lax.rsqrt(jnp.mean(x * x, axis=-1, keepdims=True) + NORM_EPS) * gpre_ref[...]).astype(BF16)

    def branch(idx, h_ref, w_ref):
        gate = jax.nn.sigmoid(jnp.dot(xn, wg_ref[:, idx * d:(idx + 1) * d], preferred_element_type=F32))
        return gate * jnp.dot(h_ref[...].astype(BF16), w_ref[...], preferred_element_type=F32)

    merged = branch(0, ha_ref, wa_ref) + branch(1, hb_ref, wb_ref) + branch(2, hc_ref, wc_ref)
    mix = jnp.dot(merged.astype(BF16), wo_ref[...], preferred_element_type=F32)
    y = mix * lax.rsqrt(jnp.mean(mix * mix, axis=-1, keepdims=True) + NORM_EPS) * ng_ref[...]
    o_ref[...] = x + y


def _layer_weight(arr, layer):
    return pl.BlockSpec((None,) + arr.shape[1:], lambda *_: (layer, 0, 0), pipeline_mode=pl.Buffered(1))


def _merge(x, gpre, wg, h_a, h_b, h_c, wa, wb, wc, wo, ng, layer, tm):
    t, d = x.shape
    tok = lambda w: pl.BlockSpec((tm, w), lambda i: (i, 0))
    full = lambda arr: pl.BlockSpec(arr.shape, lambda i: (0, 0))
    lw = lambda arr: _layer_weight(arr, layer)
    return pl.pallas_call(
        _merge_kernel,
        grid=(t // tm,),
        in_specs=[tok(d), full(gpre), lw(wg), tok(A_W), tok(B_W), tok(C_W),
                  lw(wa), lw(wb), lw(wc), lw(wo), full(ng)],
        out_specs=tok(d),
        out_shape=jax.ShapeDtypeStruct((t, d), F32),
        compiler_params=_cparams("parallel"),
    )(x, gpre, wg, h_a, h_b, h_c, wa, wb, wc, wo, ng)


def _ffn_kernel(x_ref, gpre_ref, wi_ref, wo_ref, gpost_ref, o_ref, *, tf):
    x = x_ref[...]
    xn = (x * lax.rsqrt(jnp.mean(x * x, axis=-1, keepdims=True) + NORM_EPS) * gpre_ref[...]).astype(BF16)
    f = None
    for c in range(D_FF // tf):
        gate = jnp.dot(xn, wi_ref[:, c * tf:(c + 1) * tf], preferred_element_type=F32)
        up = jnp.dot(xn, wi_ref[:, D_FF + c * tf:D_FF + (c + 1) * tf], preferred_element_type=F32)
        act = (gate * jax.nn.sigmoid(gate) * up).astype(BF16)
        part = jnp.dot(act, wo_ref[c * tf:(c + 1) * tf, :], preferred_element_type=F32)
        f = part if f is None else f + part
    y = f * lax.rsqrt(jnp.mean(f * f, axis=-1, keepdims=True) + NORM_EPS) * gpost_ref[...]
    o_ref[...] = x + y


def _ffn(x, gpre, w_in, w_out, gpost, layer, tm, tf):
    t, d = x.shape
    full = lambda arr: pl.BlockSpec(arr.shape, lambda i: (0, 0))
    return pl.pallas_call(
        functools.partial(_ffn_kernel, tf=tf),
        grid=(t // tm,),
        in_specs=[pl.BlockSpec((tm, d), lambda i: (i, 0)), full(gpre), _layer_weight(w_in, layer),
                  _layer_weight(w_out, layer), full(gpost)],
        out_specs=pl.BlockSpec((tm, d), lambda i: (i, 0)),
        out_shape=jax.ShapeDtypeStruct((t, d), F32),
        compiler_params=_cparams("parallel"),
    )(x, gpre, w_in, w_out, gpost)


def _rel_bucket(rel):
    half = NUM_BUCKETS // 2
    exact = half // 2
    n = jnp.abs(rel)
    far = exact + (jnp.log(jnp.maximum(n, 1).astype(F32) / exact)
                   / math.log(REL_MAX_DIST / exact) * (half - exact)).astype(jnp.int32)
    far = jnp.minimum(far, half - 1)
    return jnp.where(rel > 0, half, 0) + jnp.where(n < exact, n, far)


def _rel_bias(table, n_q, n_before):
    rel = (jnp.arange(n_before + n_q)[None, :] - n_before) - jnp.arange(n_q)[:, None]
    onehot = (_rel_bucket(rel)[None] == jnp.arange(NUM_BUCKETS)[:, None, None]).astype(F32)
    bias = jnp.einsum('bh,bqk->hqk', table.astype(F32), onehot, precision=HI)
    return bias.reshape(B_KV, B_GROUP * n_q, n_before + n_q)


def _layer(x, st, lp, bias, L, tm):
    b, t, d = x.shape
    swa_k, swa_v, mc, mn, mm, rs, rshift = st
    x2 = x.reshape(b * t, d)
    layer = lp['layer']
    cols2, side2 = _proj_in(x2, lp['norm_mix_pre'], lp['w_in'], layer, tm, PROJ_IN_TN)
    cols = cols2.reshape(b, t, N_COLS)
    side = side2.reshape(b, t, SIDE_W)

    lq = bias.shape[1] // B_GROUP
    sinks = jnp.repeat(lp['b_sinks'][0], lq).reshape(B_KV, B_GROUP * lq, 1)
    keep = min(WINDOW, t)
    k_new = side[:, t - keep:, S_BK:S_BK + B_KV * B_HD]
    v_new = side[:, t - keep:, S_BV:S_BV + B_KV * B_HD]
    if swa_k is None:
        kf = vf = None
        swa_k, swa_v = k_new, v_new
    else:
        kf = jnp.concatenate([swa_k, k_new], axis=1)
        vf = jnp.concatenate([swa_v, v_new], axis=1)
        swa_k, swa_v = kf[:, t:], vf[:, t:]
    h_a, h_b, h_c, (mc, mn, mm, rs, rshift) = _mixers(cols, side, (mc, mn, mm, rs, rshift), lp, bias, sinks, L,
                                                      kf, vf)

    x2 = _merge(x2, lp['norm_mix_pre'], lp['w_gate'], h_a.reshape(b * t, A_W), h_b.reshape(b * t, B_W),
                h_c.reshape(b * t, C_W), lp['w_branch_a'], lp['w_branch_b'], lp['w_branch_c'], lp['w_out'],
                lp['norm_mix_post'], layer, tm)
    x2 = _ffn(x2, lp['norm_ffn_pre'], lp['w_ffn_in'], lp['w_ffn_out'], lp['norm_ffn_post'], layer, tm, 256)
    return x2.reshape(b, t, d), (swa_k, swa_v, mc, mn, mm, rs, rshift)


def _trunk(x, layer_states, layers, bias, L, tm):
    new = []
    for l in range(DEPTH):
        x, st = _layer(x, layer_states[l], layers[l], bias, L, tm)
        new.append(st)
    return x, [jnp.stack([s[i] for s in new]) for i in range(7)]


def kernel(x_prompt, x_sample, cache_swa_k, cache_swa_v, state_mlstm_c, state_mlstm_n, state_mlstm_m, state_rwkv_s, state_rwkv_shift, w_in, norm_mix_pre, norm_mix_post, norm_ffn_pre, norm_ffn_post, a_gate_bias, a_norm, rel_bias, b_sinks, c_mu, c_w0, c_w_w2, c_a0, c_w_a2, c_w_g2, c_k_k, c_k_a, c_r_k, c_gn_g, c_gn_b, w_branch_a, w_branch_b, w_branch_c, w_out, w_ffn_in, w_ffn_out):
    bp, tp, _ = x_prompt.shape
    bs, ts, _ = x_sample.shape
    kw = B_KV * B_HD
    w_in_p, w_gate_p = _split_w_in(w_in, 256)
    stacked = {
        'w_in': w_in_p, 'w_gate': w_gate_p,
        'w_branch_a': _to_bf16(w_branch_a, A_W), 'w_branch_b': _to_bf16(w_branch_b, B_W),
        'w_branch_c': _to_bf16(w_branch_c, C_W), 'w_out': _to_bf16(w_out, D_MODEL),
        'w_ffn_in': _to_bf16(w_ffn_in, D_MODEL // 2), 'w_ffn_out': _to_bf16(w_ffn_out, D_FF // 2),
    }
    gate_bias = jnp.pad(a_gate_bias, ((0, 0), (0, LANES - 2 * A_HEADS)))
    row = lambda p, l: p[l][None, :]
    layers = []
    for l in range(DEPTH):
        layers.append({
            **stacked, 'layer': l, 'norm_mix_pre': row(norm_mix_pre, l), 'norm_mix_post': row(norm_mix_post, l),
            'norm_ffn_pre': row(norm_ffn_pre, l), 'norm_ffn_post': row(norm_ffn_post, l),
            'a_gate_bias': row(gate_bias, l), 'a_norm': row(a_norm, l), 'b_sinks': row(b_sinks, l),
            'c_mu': row(c_mu, l), 'c_w0': row(c_w0, l), 'c_w_w2': c_w_w2[l], 'c_a0': row(c_a0, l),
            'c_w_a2': c_w_a2[l], 'c_w_g2': c_w_g2[l], 'c_k_k': row(c_k_k, l), 'c_k_a': row(c_k_a, l),
            'c_r_k': row(c_r_k, l), 'c_gn_g': row(c_gn_g, l), 'c_gn_b': row(c_gn_b, l),
        })

    fresh = (None, None,
             jnp.zeros((bp, A_HEADS, A_DK, A_DK), F32), jnp.zeros((bp, A_HEADS, A_DK), F32),
             jnp.zeros((bp, 1, A_HEADS), F32), jnp.zeros((bp, C_HEADS, C_HD, C_HD), F32),
             jnp.zeros((bp, 1, C_COLS), F32))
    lp_chunk = min(CHUNK, tp)
    y_prompt, p_st = _trunk(x_prompt, [fresh] * DEPTH, layers,
                            _rel_bias(rel_bias, lp_chunk, WIN_CHUNKS * CHUNK), lp_chunk, min(1024, bp * tp))

    n_before = cache_swa_k.shape[2]
    carried = [(cache_swa_k[l].reshape(bs, n_before, kw), cache_swa_v[l].reshape(bs, n_before, kw),
                state_mlstm_c[l], state_mlstm_n[l], state_mlstm_m[l][:, None, :],
                state_rwkv_s[l], state_rwkv_shift[l]) for l in range(DEPTH)]
    ls_chunk = min(CHUNK, ts)
    y_sample, s_st = _trunk(x_sample, carried, layers, _rel_bias(rel_bias, ts, n_before), ls_chunk,
                            min(1024, bs * ts))

    def finish(st, b):
        swa_k, swa_v, mc, mn, mm, rs, rshift = st
        n_rows = swa_k.shape[2]
        return (swa_k.reshape(DEPTH, b, n_rows, B_KV, B_HD), swa_v.reshape(DEPTH, b, n_rows, B_KV, B_HD),
                mc, mn, mm.reshape(DEPTH, b, A_HEADS), rs, rshift)

    return (y_prompt, y_sample) + finish(p_st, bp) + finish(s_st, bs)
```
